```python
import jax, jax.numpy as jnp
from jax import lax
import numpy as np

D_MODEL = 2048
BATCH = 4
SEQ = 2048
DEPTH = 1
DEC_BATCH = 128
DEC_SEQ = 8
PAST_LEN = 16384
PAGE_SIZE = 128

C_CONV = D_MODEL // 4
CONV_K = 31
DN_HEAD_DIM = 128
DN_HEADS = D_MODEL // 256
DN_WIDTH = DN_HEADS * DN_HEAD_DIM
DN_CONV_K = 4
DN_CHUNK = 64
MEM_HEADS = 4
MEM_HEAD_DIM = D_MODEL // 16
MEM_WIDTH = MEM_HEADS * MEM_HEAD_DIM
MEM_TOKENS = 256
MIX_WIDTH = C_CONV + DN_WIDTH + MEM_WIDTH
IN_SPLIT_SIZES = (C_CONV, C_CONV, C_CONV, 3 * DN_WIDTH, DN_WIDTH, DN_HEADS, DN_HEADS, MEM_WIDTH, MEM_WIDTH)
N_IN = 3 * C_CONV + 4 * DN_WIDTH + 2 * DN_HEADS + 2 * MEM_WIDTH
DEEPNORM_ALPHA = (2 * DEPTH) ** 0.25
DEEPNORM_BETA = (8 * DEPTH) ** -0.25
LN_EPS = 1e-5
NORM_EPS = 1e-6

kernel_name = "hymba_conformer_gdn_memory_step"


def layer_norm(x, g, b):
    xf = x.astype(jnp.float32)
    mu = jnp.mean(xf, axis=-1, keepdims=True)
    xc = xf - mu
    var = jnp.mean(xc * xc, axis=-1, keepdims=True)
    y = xc * lax.rsqrt(var + LN_EPS) * g.astype(jnp.float32) + b.astype(jnp.float32)
    return y.astype(x.dtype)


def l2_normalize(x):
    xf = x.astype(jnp.float32)
    return xf * lax.rsqrt(jnp.sum(xf * xf, axis=-1, keepdims=True) + NORM_EPS)


def causal_dwconv(x, prev, w):
    xp = jnp.concatenate([prev.astype(x.dtype), x], axis=1)
    y = lax.conv_general_dilated(
        xp, w[:, None, :].astype(x.dtype), window_strides=(1,), padding="VALID",
        dimension_numbers=("NWC", "WIO", "NWC"), feature_group_count=x.shape[-1])
    return y, xp[:, xp.shape[1] - (w.shape[0] - 1):]


def gated_delta_rule(q, k, v, g, beta, s0):
    b, l, h, _ = q.shape
    dv = v.shape[-1]
    c = min(DN_CHUNK, l)
    n = -(-l // c)
    pad = n * c - l

    def blocks(t):
        t = t.astype(jnp.float32)
        t = jnp.pad(t, [(0, 0), (0, pad)] + [(0, 0)] * (t.ndim - 2))
        t = t.reshape((b, n, c) + t.shape[2:])
        return jnp.swapaxes(t, 2, 3)

    q, k, v, g, beta = (blocks(t) for t in (q, k, v, g, beta))
    gcum = jnp.cumsum(g, axis=-1)
    pos = jnp.arange(c)
    incl = pos[:, None] >= pos[None, :]
    strict = pos[:, None] > pos[None, :]
    decay = jnp.exp(jnp.where(incl, gcum[..., :, None] - gcum[..., None, :], -jnp.inf))
    kb = k * beta[..., None]
    m = jnp.where(strict, jnp.einsum("bnhik,bnhjk->bnhij", kb, k) * decay, 0.0)
    eye = jnp.eye(c, dtype=jnp.float32)
    tinv = lax.linalg.triangular_solve(eye + m, jnp.broadcast_to(eye, m.shape),
                                       left_side=True, lower=True, unit_diagonal=True)
    u0 = tinv @ (v * beta[..., None])
    w = tinv @ (kb * jnp.exp(gcum)[..., None])
    a_qk = jnp.einsum("bnhik,bnhjk->bnhij", q, k) * decay
    q_dec = q * jnp.exp(gcum)[..., None]
    k_dec = k * jnp.exp(gcum[..., -1:] - gcum)[..., None]
    g_last = jnp.exp(gcum[..., -1])[..., None, None]

    def step(s, xs):
        u0_c, w_c, aqk_c, qd_c, kd_c, gl_c = xs
        u = u0_c - jnp.einsum("bhck,bhkv->bhcv", w_c, s)
        o = jnp.einsum("bhck,bhkv->bhcv", qd_c, s) + jnp.einsum("bhij,bhjv->bhiv", aqk_c, u)
        s = s * gl_c + jnp.einsum("bhck,bhcv->bhkv", kd_c, u)
        return s, o

    xs = tuple(jnp.moveaxis(t, 1, 0) for t in (u0, w, a_qk, q_dec, k_dec, g_last))
    s_final, o = lax.scan(step, s0.astype(jnp.float32), xs)
    o = jnp.transpose(o, (1, 0, 3, 2, 4)).reshape(b, n * c, h, dv)[:, :l]
    return o, s_final


def hybrid_layer(x, conv_prev, qkv_prev, s0, mem_k, mem_v, w_in, conv_w, conv_b, conv_ln_g, conv_ln_b,
                 qkv_conv_w, a_log, dt_bias, delta_norm_w, w_out, ln_g, ln_b):
    b, l, _ = x.shape
    dt = x.dtype
    f32 = jnp.float32
    proj = jnp.einsum("bld,de->ble", x, w_in)
    splits = np.cumsum(IN_SPLIT_SIZES)[:-1].tolist()
    glu_a, glu_b, conv_gate, qkv, dn_gate, beta_in, decay_in, mem_q, mem_gate = jnp.split(proj, splits, axis=-1)

    u = glu_a * jax.nn.sigmoid(glu_b)
    hc, conv_state = causal_dwconv(u, conv_prev, conv_w)
    hc = layer_norm(hc + conv_b.astype(dt), conv_ln_g, conv_ln_b)
    out_conv = (jax.nn.silu(hc) * jax.nn.silu(conv_gate)).astype(dt)

    qkv_c, qkv_state = causal_dwconv(qkv, qkv_prev, qkv_conv_w)
    qkv_c = jax.nn.silu(qkv_c)
    q, k, v = jnp.split(qkv_c, 3, axis=-1)
    q = l2_normalize(q.reshape(b, l, DN_HEADS, DN_HEAD_DIM)) * (DN_HEAD_DIM ** -0.5)
    k = l2_normalize(k.reshape(b, l, DN_HEADS, DN_HEAD_DIM))
    v = v.reshape(b, l, DN_HEADS, DN_HEAD_DIM)
    beta = jax.nn.sigmoid(beta_in.astype(f32))
    g = -jnp.exp(a_log.astype(f32)) * jax.nn.softplus(decay_in.astype(f32) + dt_bias.astype(f32))
    o, s_final = gated_delta_rule(q, k, v, g, beta, s0)
    o = o * lax.rsqrt(jnp.mean(o * o, axis=-1, keepdims=True) + NORM_EPS) * delta_norm_w.astype(f32)
    o = o * jax.nn.silu(dn_gate.astype(f32).reshape(b, l, DN_HEADS, DN_HEAD_DIM))
    out_dn = o.reshape(b, l, DN_WIDTH).astype(dt)

    mq = mem_q.reshape(b, l, MEM_HEADS, MEM_HEAD_DIM)
    scores = jnp.einsum("blhd,bmhd->bhlm", mq, mem_k.astype(dt)).astype(f32) * (MEM_HEAD_DIM ** -0.5)
    p = jax.nn.softmax(scores, axis=-1).astype(dt)
    om = jnp.einsum("bhlm,bmhd->blhd", p, mem_v.astype(dt)).reshape(b, l, MEM_WIDTH)
    out_mem = (om * jax.nn.silu(mem_gate)).astype(dt)

    mixed = jnp.concatenate([out_conv, out_dn, out_mem], axis=-1)
    h = jnp.einsum("ble,ed->bld", mixed, w_out)
    y = layer_norm(DEEPNORM_ALPHA * x + h, ln_g, ln_b)
    return y, conv_state, qkv_state, s_final


def setup_inputs(seed: int = 0) -> dict:
    key = jax.random.key(seed)
    ks = jax.random.split(key, 22)
    f32 = jnp.float32
    nrm = lambda k, shape, scale: jax.random.normal(k, shape, f32) * scale
    return {
        "x_prompt": nrm(ks[0], (BATCH, SEQ, D_MODEL), 1.0),
        "x_sample": nrm(ks[1], (DEC_BATCH, DEC_SEQ, D_MODEL), 1.0),
        "mem_prompt": nrm(ks[2], (BATCH, MEM_TOKENS, D_MODEL), 1.0),
        "state_conv": nrm(ks[3], (DEPTH, DEC_BATCH, CONV_K - 1, C_CONV), 0.5),
        "state_qkv_conv": nrm(ks[4], (DEPTH, DEC_BATCH, DN_CONV_K - 1, 3 * DN_WIDTH), 1.0),
        "state_delta": nrm(ks[5], (DEPTH, DEC_BATCH, DN_HEADS, DN_HEAD_DIM, DN_HEAD_DIM), DN_HEAD_DIM ** -0.5),
        "cache_mem_k": nrm(ks[6], (DEPTH, DEC_BATCH, MEM_TOKENS, MEM_HEADS, MEM_HEAD_DIM), 1.0),
        "cache_mem_v": nrm(ks[7], (DEPTH, DEC_BATCH, MEM_TOKENS, MEM_HEADS, MEM_HEAD_DIM), 1.0),
        "w_in": nrm(ks[8], (DEPTH, D_MODEL, N_IN), D_MODEL ** -0.5),
        "conv_w": nrm(ks[9], (DEPTH, CONV_K, C_CONV), CONV_K ** -0.5),
        "conv_b": nrm(ks[10], (DEPTH, C_CONV), 0.01),
        "conv_ln_g": 1.0 + nrm(ks[11], (DEPTH, C_CONV), 0.01),
        "conv_ln_b": nrm(ks[12], (DEPTH, C_CONV), 0.01),
        "qkv_conv_w": nrm(ks[13], (DEPTH, DN_CONV_K, 3 * DN_WIDTH), DN_CONV_K ** -0.5),
        "a_log": jnp.log(jax.random.uniform(ks[14], (DEPTH, DN_HEADS), f32, 1.0, 16.0)),
        "dt_bias": nrm(ks[15], (DEPTH, DN_HEADS), 0.1),
        "delta_norm_w": 1.0 + nrm(ks[16], (DEPTH, DN_HEAD_DIM), 0.01),
        "w_mem_k": nrm(ks[17], (DEPTH, D_MODEL, MEM_WIDTH), D_MODEL ** -0.5),
        "w_mem_v": nrm(ks[18], (DEPTH, D_MODEL, MEM_WIDTH), D_MODEL ** -0.5),
        "w_out": nrm(ks[19], (DEPTH, MIX_WIDTH, D_MODEL), (MIX_WIDTH ** -0.5) * DEEPNORM_BETA),
        "ln_g": 1.0 + nrm(ks[20], (DEPTH, D_MODEL), 0.01),
        "ln_b": nrm(ks[21], (DEPTH, D_MODEL), 0.01),
    }


def reference(x_prompt, x_sample, mem_prompt, state_conv, state_qkv_conv, state_delta, cache_mem_k, cache_mem_v,
              w_in, conv_w, conv_b, conv_ln_g, conv_ln_b, qkv_conv_w, a_log, dt_bias, delta_norm_w,
              w_mem_k, w_mem_v, w_out, ln_g, ln_b):
    hp, hs = x_prompt, x_sample
    bp, bs = x_prompt.shape[0], x_sample.shape[0]
    conv_p, qkv_p, delta_p, memk_p, memv_p = [], [], [], [], []
    conv_s, qkv_s, delta_s = [], [], []
    for i in range(DEPTH):
        lw = (w_in[i], conv_w[i], conv_b[i], conv_ln_g[i], conv_ln_b[i], qkv_conv_w[i], a_log[i], dt_bias[i],
              delta_norm_w[i], w_out[i], ln_g[i], ln_b[i])
        mk = jnp.einsum("bmd,de->bme", mem_prompt, w_mem_k[i]).reshape(bp, MEM_TOKENS, MEM_HEADS, MEM_HEAD_DIM)
        mv = jnp.einsum("bmd,de->bme", mem_prompt, w_mem_v[i]).reshape(bp, MEM_TOKENS, MEM_HEADS, MEM_HEAD_DIM)
        hp, c_st, q_st, s_st = hybrid_layer(
            hp, jnp.zeros((bp, CONV_K - 1, C_CONV), hp.dtype), jnp.zeros((bp, DN_CONV_K - 1, 3 * DN_WIDTH), hp.dtype),
            jnp.zeros((bp, DN_HEADS, DN_HEAD_DIM, DN_HEAD_DIM), jnp.float32), mk, mv, *lw)
        conv_p.append(c_st); qkv_p.append(q_st); delta_p.append(s_st); memk_p.append(mk); memv_p.append(mv)
        hs, c_st, q_st, s_st = hybrid_layer(
            hs, state_conv[i], state_qkv_conv[i], state_delta[i], cache_mem_k[i], cache_mem_v[i], *lw)
        conv_s.append(c_st); qkv_s.append(q_st); delta_s.append(s_st)
    return (hp, hs, jnp.stack(conv_p), jnp.stack(qkv_p), jnp.stack(delta_p), jnp.stack(memk_p), jnp.stack(memv_p),
            jnp.stack(conv_s), jnp.stack(qkv_s), jnp.stack(delta_s))
```

```python
import functools

import jax
import jax.numpy as jnp
from jax import lax
from jax.experimental import pallas as pl
from jax.experimental.pallas import tpu as pltpu

F32 = jnp.float32
BF16 = jnp.bfloat16

D_MODEL = 2048
C_CONV = 512
CONV_K = 31
DN_HEADS = 8
DN_HEAD_DIM = 128
DN_WIDTH = DN_HEADS * DN_HEAD_DIM
DN_CONV_K = 4
MEM_HEADS = 4
MEM_HEAD_DIM = 128
MEM_WIDTH = MEM_HEADS * MEM_HEAD_DIM
MEM_TOKENS = 256
DEPTH = 1
DEEPNORM_ALPHA = (2 * DEPTH) ** 0.25
LN_EPS = 1e-5
NORM_EPS = 1e-6

N_MAIN = 3 * DN_WIDTH + DN_WIDTH + 3 * C_CONV + 2 * MEM_WIDTH
COL_BLOCK = 512
QKV_BLK = 0
DNGATE_BLK = 3
GLUA_BLK, GLUB_BLK, CGATE_BLK, MEMQ_BLK, MEMG_BLK = 8, 9, 10, 11, 12
SMALL_PAD = 128
CONV_HALO = 32
QKV_HALO = 8

VMEM_LIMIT = 56 * 1024 * 1024


def _sigmoid(x):
    return 1.0 / (1.0 + jnp.exp(-x))


def _silu(x):
    return x * _sigmoid(x)


def _softplus(x):
    return jnp.maximum(x, 0.0) + jnp.log(1.0 + jnp.exp(-jnp.abs(x)))


def _cparams(sem):
    return pltpu.CompilerParams(dimension_semantics=sem, vmem_limit_bytes=VMEM_LIMIT)


def _proj_kernel(x_ref, w_ref, ws_ref, wst_ref, o_ref, bdc_ref, bdr_ref, xb_ref):
    @pl.when(pl.program_id(1) == 0)
    def _():
        xb = x_ref[...].astype(BF16)
        xb_ref[...] = xb
        bdc_ref[...] = jnp.dot(xb, ws_ref[...], preferred_element_type=F32)
        bdr_ref[...] = lax.dot_general(wst_ref[...], xb, (((1,), (1,)), ((), ())),
                                       preferred_element_type=F32)

    o_ref[...] = jnp.dot(xb_ref[...], w_ref[...], preferred_element_type=F32)


def _in_proj(x2, w_main, w_small, w_small_t, bm):
    m = x2.shape[0]
    bn = COL_BLOCK
    return pl.pallas_call(
        _proj_kernel,
        grid=(m // bm, N_MAIN // bn),
        in_specs=[
            pl.BlockSpec((bm, D_MODEL), lambda i, j: (i, 0)),
            pl.BlockSpec((D_MODEL, bn), lambda i, j: (0, j)),
            pl.BlockSpec((D_MODEL, SMALL_PAD), lambda i, j: (0, 0)),
            pl.BlockSpec((2 * DN_HEADS, D_MODEL), lambda i, j: (0, 0)),
        ],
        out_specs=[
            pl.BlockSpec((bm, bn), lambda i, j: (i, j)),
            pl.BlockSpec((bm, SMALL_PAD), lambda i, j: (i, 0)),
            pl.BlockSpec((2 * DN_HEADS, bm), lambda i, j: (0, i)),
        ],
        out_shape=[
            jax.ShapeDtypeStruct((m, N_MAIN), F32),
            jax.ShapeDtypeStruct((m, SMALL_PAD), F32),
            jax.ShapeDtypeStruct((2 * DN_HEADS, m), F32),
        ],
        scratch_shapes=[pltpu.VMEM((bm, D_MODEL), BF16)],
        compiler_params=_cparams(("arbitrary", "arbitrary")),
        name="in_proj",
    )(x2, w_main, w_small, w_small_t)


def _mm_kernel(x_ref, w_ref, o_ref):
    o_ref[...] = jnp.dot(x_ref[...].astype(BF16), w_ref[...], preferred_element_type=F32)


def _matmul(x2, w, bm, bn):
    m, k = x2.shape
    n = w.shape[1]
    return pl.pallas_call(
        _mm_kernel,
        grid=(m // bm, n // bn),
        in_specs=[pl.BlockSpec((bm, k), lambda i, j: (i, 0)),
                  pl.BlockSpec((k, bn), lambda i, j: (0, j))],
        out_specs=pl.BlockSpec((bm, bn), lambda i, j: (i, j)),
        out_shape=jax.ShapeDtypeStruct((m, n), F32),
        compiler_params=_cparams(("arbitrary", "arbitrary")),
        name="mem_kv_proj",
    )(x2, w)


def _conv_kernel(*refs, tile, nb, has_state):
    if has_state:
        (a_ref, b_ref, g_ref, st_ref, w_ref, p_ref, o_ref, so_ref, xp_ref) = refs
    else:
        (a_ref, b_ref, g_ref, w_ref, p_ref, o_ref, so_ref, xp_ref) = refs
        st_ref = None
    t = pl.program_id(1)
    last = pl.num_programs(1) - 1
    lo = CONV_HALO - (CONV_K - 1)
    w = w_ref[...]
    conv_b = p_ref[0:1, :]
    ln_g = p_ref[1:2, :]
    ln_b = p_ref[2:3, :]
    for i in range(nb):
        @pl.when(t == 0)
        def _():
            xp_ref[i, 0:CONV_HALO, :] = jnp.zeros((CONV_HALO, C_CONV), F32)
            if has_state:
                xp_ref[i, lo:CONV_HALO, :] = st_ref[i]

        u = a_ref[i] * _sigmoid(b_ref[i])
        xp_ref[i, CONV_HALO:CONV_HALO + tile, :] = u
        acc = jnp.zeros((tile, C_CONV), F32)
        for j in range(CONV_K):
            acc = acc + xp_ref[i, lo + j:lo + j + tile, :] * w[j:j + 1, :]
        hc = acc + conv_b
        mu = jnp.mean(hc, axis=-1, keepdims=True)
        xc = hc - mu
        var = jnp.mean(xc * xc, axis=-1, keepdims=True)
        hn = xc * lax.rsqrt(var + LN_EPS) * ln_g + ln_b
        o_ref[i] = (_silu(hn) * _silu(g_ref[i])).astype(o_ref.dtype)

        @pl.when(t == last)
        def _():
            so_ref[i] = xp_ref[i, tile + lo:tile + CONV_HALO, :]

        xp_ref[i, 0:CONV_HALO, :] = xp_ref[i, tile:tile + CONV_HALO, :]


def _conv_group(proj3, state, conv_w, conv_p, tile, nb):
    b, l, _ = proj3.shape
    has_state = state is not None
    cb = COL_BLOCK
    in_specs = [
        pl.BlockSpec((nb, tile, cb), lambda i, t: (i, t, GLUA_BLK)),
        pl.BlockSpec((nb, tile, cb), lambda i, t: (i, t, GLUB_BLK)),
        pl.BlockSpec((nb, tile, cb), lambda i, t: (i, t, CGATE_BLK)),
    ]
    args = [proj3, proj3, proj3]
    if has_state:
        in_specs.append(pl.BlockSpec((nb, CONV_K - 1, C_CONV), lambda i, t: (i, 0, 0)))
        args.append(state)
    in_specs += [pl.BlockSpec((CONV_K, C_CONV), lambda i, t: (0, 0)),
                 pl.BlockSpec((8, C_CONV), lambda i, t: (0, 0))]
    args += [conv_w, conv_p]
    return pl.pallas_call(
        functools.partial(_conv_kernel, tile=tile, nb=nb, has_state=has_state),
        grid=(b // nb, l // tile),
        in_specs=in_specs,
        out_specs=[pl.BlockSpec((nb, tile, C_CONV), lambda i, t: (i, t, 0)),
                   pl.BlockSpec((nb, CONV_K - 1, C_CONV), lambda i, t: (i, 0, 0))],
        out_shape=[jax.ShapeDtypeStruct((b, l, C_CONV), BF16),
                   jax.ShapeDtypeStruct((b, CONV_K - 1, C_CONV), F32)],
        scratch_shapes=[pltpu.VMEM((nb, tile + CONV_HALO, C_CONV), F32)],
        compiler_params=_cparams(("arbitrary", "arbitrary")),
        name="conv_group",
    )(*args)


def _unit_lower_inverse(m, c):
    hi = lax.Precision.HIGHEST
    row = lax.broadcasted_iota(jnp.int32, (c, c), 0)
    col = lax.broadcasted_iota(jnp.int32, (c, c), 1)
    eye = (row == col).astype(F32)
    p = -m
    t = eye + p
    span = 2
    while span < c:
        p = jnp.dot(p, p, precision=hi, preferred_element_type=F32)
        t = t + jnp.dot(t, p, precision=hi, preferred_element_type=F32)
        span *= 2
    return t


def _dn_kernel(*refs, tile, chunk, has_state):
    if has_state:
        (qkv_ref, gate_ref, bdc_ref, bdr_ref, prev_ref, s0_ref, w_ref, pr_ref, pc_ref, nw_ref,
         o_ref, st_ref, s_ref, xp_ref, qkvc_ref, osc_ref) = refs
    else:
        (qkv_ref, gate_ref, bdc_ref, bdr_ref, w_ref, pr_ref, pc_ref, nw_ref,
         o_ref, st_ref, s_ref, xp_ref, qkvc_ref, osc_ref) = refs
        prev_ref = s0_ref = None
    c = chunk
    nchunk = tile // c
    hd = DN_HEAD_DIM
    t = pl.program_id(1)
    last = pl.num_programs(1) - 1
    lo = QKV_HALO - (DN_CONV_K - 1)
    hi = lax.Precision.HIGHEST

    @pl.when(t == 0)
    def _():
        xp_ref[0:QKV_HALO, :] = jnp.zeros((QKV_HALO, 3 * DN_WIDTH), F32)
        if has_state:
            xp_ref[lo:QKV_HALO, :] = prev_ref[0]
            s_ref[0] = s0_ref[0]
        else:
            s_ref[0] = jnp.zeros((DN_HEADS, hd, hd), F32)

    xp_ref[QKV_HALO:QKV_HALO + tile, :] = qkv_ref[0]
    w = w_ref[...]
    acc = xp_ref[lo:lo + tile, :] * w[0:1, :]
    for j in range(1, DN_CONV_K):
        acc = acc + xp_ref[lo + j:lo + j + tile, :] * w[j:j + 1, :]
    qkvc_ref[...] = _silu(acc)

    @pl.when(t == last)
    def _():
        st_ref[0] = xp_ref[tile + lo:tile + QKV_HALO, :]

    xp_ref[0:QKV_HALO, :] = xp_ref[tile:tile + QKV_HALO, :]

    neg_a_row = -jnp.exp(pr_ref[0:1, :])
    dtb_row = pr_ref[1:2, :]
    neg_a_col = -jnp.exp(pc_ref[:, 0:1])
    dtb_col = pc_ref[:, 1:2]
    norm_w = nw_ref[...]

    row = lax.broadcasted_iota(jnp.int32, (c, c), 0)
    col = lax.broadcasted_iota(jnp.int32, (c, c), 1)
    incl = row >= col
    strict = row > col
    tri_l = incl.astype(F32)
    tri_u = (row <= col).astype(F32)

    def chunk_body(n, carry):
        r0 = pl.multiple_of(n * c, c)
        bd = bdc_ref[0, pl.ds(r0, c), :]
        beta = _sigmoid(bd[:, 0:DN_HEADS])
        g_col = neg_a_row * _softplus(bd[:, DN_HEADS:2 * DN_HEADS] + dtb_row)
        gcum_col = jnp.dot(tri_l, g_col, precision=hi, preferred_element_type=F32)
        br = bdr_ref[0, n]
        g_row = neg_a_col * _softplus(br[DN_HEADS:2 * DN_HEADS, :] + dtb_col)
        gcum_row = jnp.dot(g_row, tri_u, precision=hi, preferred_element_type=F32)
        for h in range(DN_HEADS):
            q = qkvc_ref[pl.ds(r0, c), h * hd:(h + 1) * hd]
            k = qkvc_ref[pl.ds(r0, c), DN_WIDTH + h * hd:DN_WIDTH + (h + 1) * hd]
            v = qkvc_ref[pl.ds(r0, c), 2 * DN_WIDTH + h * hd:2 * DN_WIDTH + (h + 1) * hd]
            q = q * (lax.rsqrt(jnp.sum(q * q, axis=-1, keepdims=True) + NORM_EPS) * (hd ** -0.5))
            k = k * lax.rsqrt(jnp.sum(k * k, axis=-1, keepdims=True) + NORM_EPS)
            gc = gcum_col[:, h:h + 1]
            gr = gcum_row[h:h + 1, :]
            bt = beta[:, h:h + 1]
            g_last = gcum_col[c - 1:c, h:h + 1]
            decay = jnp.exp(jnp.where(incl, gc - gr, -jnp.inf))
            e_gc = jnp.exp(gc)
            kb = k * bt
            k16 = k.astype(BF16)
            nt = (((1,), (1,)), ((), ()))
            kk = lax.dot_general(kb.astype(BF16), k16, nt, preferred_element_type=F32)
            m = jnp.where(strict, kk * decay, 0.0)
            tinv = _unit_lower_inverse(m, c)
            t16 = tinv.astype(BF16)
            u0 = jnp.dot(t16, (v * bt).astype(BF16), preferred_element_type=F32)
            wm = jnp.dot(t16, (kb * e_gc).astype(BF16), preferred_element_type=F32)
            aqk = lax.dot_general(q.astype(BF16), k16, nt, preferred_element_type=F32) * decay
            qd = q * e_gc
            kd = k * jnp.exp(g_last - gc)
            s = s_ref[0, h]
            s16 = s.astype(BF16)
            u = u0 - jnp.dot(wm.astype(BF16), s16, preferred_element_type=F32)
            u16 = u.astype(BF16)
            o = (jnp.dot(qd.astype(BF16), s16, preferred_element_type=F32)
                 + jnp.dot(aqk.astype(BF16), u16, preferred_element_type=F32))
            s_ref[0, h] = s * jnp.exp(g_last) + lax.dot_general(
                kd.astype(BF16), u16, (((0,), (0,)), ((), ())), preferred_element_type=F32)
            o = o * lax.rsqrt(jnp.mean(o * o, axis=-1, keepdims=True) + NORM_EPS) * norm_w
            osc_ref[pl.ds(r0, c), h * hd:(h + 1) * hd] = o
        return carry

    lax.fori_loop(0, nchunk, chunk_body, 0)
    o_ref[0] = (osc_ref[...] * _silu(gate_ref[0])).astype(o_ref.dtype)


def _dn_group(proj3, bdc3, bdr4, prev, s0, qkv_w, p_row, p_col, norm_w, tile, chunk):
    b, l, _ = proj3.shape
    has_state = s0 is not None
    nchunk = tile // chunk
    hd = DN_HEAD_DIM
    in_specs = [
        pl.BlockSpec((1, tile, 3 * DN_WIDTH), lambda i, t: (i, t, QKV_BLK)),
        pl.BlockSpec((1, tile, DN_WIDTH), lambda i, t: (i, t, DNGATE_BLK)),
        pl.BlockSpec((1, tile, SMALL_PAD), lambda i, t: (i, t, 0)),
        pl.BlockSpec((1, nchunk, 2 * DN_HEADS, chunk), lambda i, t: (i, t, 0, 0)),
    ]
    args = [proj3, proj3, bdc3, bdr4]
    if has_state:
        in_specs += [pl.BlockSpec((1, DN_CONV_K - 1, 3 * DN_WIDTH), lambda i, t: (i, 0, 0)),
                     pl.BlockSpec((1, DN_HEADS, hd, hd), lambda i, t: (i, 0, 0, 0))]
        args += [prev, s0]
    in_specs += [pl.BlockSpec((DN_CONV_K, 3 * DN_WIDTH), lambda i, t: (0, 0)),
                 pl.BlockSpec((2, DN_HEADS), lambda i, t: (0, 0)),
                 pl.BlockSpec((DN_HEADS, 2), lambda i, t: (0, 0)),
                 pl.BlockSpec((1, hd), lambda i, t: (0, 0))]
    args += [qkv_w, p_row, p_col, norm_w]
    return pl.pallas_call(
        functools.partial(_dn_kernel, tile=tile, chunk=chunk, has_state=has_state),
        grid=(b, l // tile),
        in_specs=in_specs,
        out_specs=[pl.BlockSpec((1, tile, DN_WIDTH), lambda i, t: (i, t, 0)),
                   pl.BlockSpec((1, DN_CONV_K - 1, 3 * DN_WIDTH), lambda i, t: (i, 0, 0)),
                   pl.BlockSpec((1, DN_HEADS, hd, hd), lambda i, t: (i, 0, 0, 0))],
        out_shape=[jax.ShapeDtypeStruct((b, l, DN_WIDTH), BF16),
                   jax.ShapeDtypeStruct((b, DN_CONV_K - 1, 3 * DN_WIDTH), F32),
                   jax.ShapeDtypeStruct((b, DN_HEADS, hd, hd), F32)],
        scratch_shapes=[pltpu.VMEM((tile + QKV_HALO, 3 * DN_WIDTH), F32),
                        pltpu.VMEM((tile, 3 * DN_WIDTH), F32),
                        pltpu.VMEM((tile, DN_WIDTH), F32)],
        compiler_params=_cparams(("arbitrary", "arbitrary")),
        name="deltanet_group",
    )(*args)


def _mem_kernel(q_ref, g_ref, k_ref, v_ref, o_ref):
    hd = MEM_HEAD_DIM
    scale = hd ** -0.5
    for h in range(MEM_HEADS):
        sl = slice(h * hd, (h + 1) * hd)
        q = q_ref[0, :, sl].astype(BF16)
        k = k_ref[0, :, sl].astype(BF16)
        v = v_ref[0, :, sl].astype(BF16)
        s = lax.dot_general(q, k, (((1,), (1,)), ((), ())), preferred_element_type=F32) * scale
        e = jnp.exp(s - jnp.max(s, axis=-1, keepdims=True))
        denom = jnp.sum(e, axis=-1, keepdims=True)
        om = jnp.dot(e.astype(BF16), v, preferred_element_type=F32) / denom
        o_ref[0, :, sl] = (om * _silu(g_ref[0, :, sl])).astype(o_ref.dtype)


def _mem_group(proj3, mem_k, mem_v, tile):
    b, l, _ = proj3.shape
    cb = COL_BLOCK
    return pl.pallas_call(
        _mem_kernel,
        grid=(b, l // tile),
        in_specs=[pl.BlockSpec((1, tile, cb), lambda i, t: (i, t, MEMQ_BLK)),
                  pl.BlockSpec((1, tile, cb), lambda i, t: (i, t, MEMG_BLK)),
                  pl.BlockSpec((1, MEM_TOKENS, MEM_WIDTH), lambda i, t: (i, 0, 0)),
                  pl.BlockSpec((1, MEM_TOKENS, MEM_WIDTH), lambda i, t: (i, 0, 0))],
        out_specs=pl.BlockSpec((1, tile, MEM_WIDTH), lambda i, t: (i, t, 0)),
        out_shape=jax.ShapeDtypeStruct((b, l, MEM_WIDTH), BF16),
        compiler_params=_cparams(("arbitrary", "arbitrary")),
        name="memory_group",
    )(proj3, proj3, mem_k, mem_v)


def _out_kernel(oc_ref, od_ref, om_ref, x_ref, w_ref, p_ref, y_ref):
    h = jnp.dot(oc_ref[...], w_ref[0:C_CONV, :], preferred_element_type=F32)
    h = h + jnp.dot(od_ref[...], w_ref[C_CONV:C_CONV + DN_WIDTH, :], preferred_element_type=F32)
    h = h + jnp.dot(om_ref[...], w_ref[C_CONV + DN_WIDTH:, :], preferred_element_type=F32)
    z = DEEPNORM_ALPHA * x_ref[...] + h
    mu = jnp.mean(z, axis=-1, keepdims=True)
    zc = z - mu
    var = jnp.mean(zc * zc, axis=-1, keepdims=True)
    y_ref[...] = zc * lax.rsqrt(var + LN_EPS) * p_ref[0:1, :] + p_ref[1:2, :]


def _out_proj(oc, od, om, x2, w_out, ln_p, bm):
    m = x2.shape[0]
    return pl.pallas_call(
        _out_kernel,
        grid=(m // bm,),
        in_specs=[pl.BlockSpec((bm, C_CONV), lambda i: (i, 0)),
                  pl.BlockSpec((bm, DN_WIDTH), lambda i: (i, 0)),
                  pl.BlockSpec((bm, MEM_WIDTH), lambda i: (i, 0)),
                  pl.BlockSpec((bm, D_MODEL), lambda i: (i, 0)),
                  pl.BlockSpec((D_MODEL, D_MODEL), lambda i: (0, 0)),
                  pl.BlockSpec((8, D_MODEL), lambda i: (0, 0))],
        out_specs=pl.BlockSpec((bm, D_MODEL), lambda i: (i, 0)),
        out_shape=jax.ShapeDtypeStruct((m, D_MODEL), F32),
        compiler_params=_cparams(("arbitrary",)),
        name="out_proj_ln",
    )(oc, od, om, x2, w_out, ln_p)


def _pad_rows(rows, n):
    a = jnp.stack(rows).astype(F32)
    return jnp.pad(a, ((0, n - a.shape[0]), (0, 0)))


def _layer(x, conv_prev, qkv_prev, s0, mem_k, mem_v, wts, *, bm, conv_tile, conv_nb, dn_tile, dn_chunk,
           mem_tile, out_bm):
    (w_main, w_small, w_small_t, conv_w, conv_p, qkv_w, p_row, p_col, norm_w, w_out, ln_p) = wts
    b, l, _ = x.shape
    x2 = x.reshape(b * l, D_MODEL)
    proj, bdc, bdr = _in_proj(x2, w_main, w_small, w_small_t, bm)
    proj3 = proj.reshape(b, l, N_MAIN)
    bdc3 = bdc.reshape(b, l, SMALL_PAD)
    bdr4 = bdr.reshape(2 * DN_HEADS, b, l // dn_chunk, dn_chunk).transpose(1, 2, 0, 3)

    oc, conv_state = _conv_group(proj3, conv_prev, conv_w, conv_p, conv_tile, conv_nb)
    od, qkv_state, s_final = _dn_group(proj3, bdc3, bdr4, qkv_prev, s0, qkv_w, p_row, p_col, norm_w,
                                       dn_tile, dn_chunk)
    om = _mem_group(proj3, mem_k, mem_v, mem_tile)
    y = _out_proj(oc.reshape(b * l, C_CONV), od.reshape(b * l, DN_WIDTH), om.reshape(b * l, MEM_WIDTH),
                  x2, w_out, ln_p, out_bm)
    return y.reshape(b, l, D_MODEL), conv_state, qkv_state, s_final


def kernel(x_prompt, x_sample, mem_prompt, state_conv, state_qkv_conv, state_delta, cache_mem_k, cache_mem_v,
           w_in, conv_w, conv_b, conv_ln_g, conv_ln_b, qkv_conv_w, a_log, dt_bias, delta_norm_w,
           w_mem_k, w_mem_v, w_out, ln_g, ln_b):
    bp = x_prompt.shape[0]
    bs = x_sample.shape[0]
    hp, hs = x_prompt, x_sample
    outs = [[] for _ in range(8)]
    for i in range(DEPTH):
        wi = w_in[i]
        c3 = 3 * C_CONV
        o_qkv, o_gate = c3, c3 + 3 * DN_WIDTH
        o_beta = o_gate + DN_WIDTH
        o_memq = o_beta + 2 * DN_HEADS
        w_main = jnp.concatenate(
            [wi[:, o_qkv:o_gate], wi[:, o_gate:o_beta], wi[:, 0:c3], wi[:, o_memq:]], axis=1).astype(BF16)
        w_bd = wi[:, o_beta:o_memq]
        w_small = jnp.pad(w_bd, ((0, 0), (0, SMALL_PAD - 2 * DN_HEADS))).astype(BF16)
        w_small_t = w_bd.T.astype(BF16)
        conv_p = _pad_rows([conv_b[i], conv_ln_g[i], conv_ln_b[i]], 8)
        p_row = jnp.stack([a_log[i], dt_bias[i]]).astype(F32)
        wts = (w_main, w_small, w_small_t, conv_w[i], conv_p, qkv_conv_w[i], p_row, p_row.T,
               delta_norm_w[i].reshape(1, DN_HEAD_DIM), w_out[i].astype(BF16),
               _pad_rows([ln_g[i], ln_b[i]], 8))

        w_kv = jnp.concatenate([w_mem_k[i], w_mem_v[i]], axis=1).astype(BF16)
        kv = _matmul(mem_prompt.reshape(bp * MEM_TOKENS, D_MODEL), w_kv, 512, 512)
        mk = kv[:, :MEM_WIDTH].reshape(bp, MEM_TOKENS, MEM_WIDTH)
        mv = kv[:, MEM_WIDTH:].reshape(bp, MEM_TOKENS, MEM_WIDTH)
        hp, c_st, q_st, s_st = _layer(hp, None, None, None, mk, mv, wts, bm=1024, conv_tile=512, conv_nb=1,
                                      dn_tile=256, dn_chunk=64, mem_tile=512, out_bm=512)
        outs[0].append(c_st); outs[1].append(q_st); outs[2].append(s_st)
        outs[3].append(mk.reshape(bp, MEM_TOKENS, MEM_HEADS, MEM_HEAD_DIM))
        outs[4].append(mv.reshape(bp, MEM_TOKENS, MEM_HEADS, MEM_HEAD_DIM))

        ls = hs.shape[1]
        hs, c_st, q_st, s_st = _layer(
            hs, state_conv[i], state_qkv_conv[i], state_delta[i],
            cache_mem_k[i].reshape(bs, MEM_TOKENS, MEM_WIDTH), cache_mem_v[i].reshape(bs, MEM_TOKENS, MEM_WIDTH),
            wts, bm=bs * ls, conv_tile=ls, conv_nb=8, dn_tile=ls, dn_chunk=ls, mem_tile=ls, out_bm=512)
        outs[5].append(c_st); outs[6].append(q_st); outs[7].append(s_st)
    return (hp, hs) + tuple(jnp.stack(o) for o in outs)
```

```python
import functools

import jax
import jax.numpy as jnp
from jax import lax
from jax.experimental import pallas as pl
from jax.experimental.pallas import tpu as pltpu

F32 = jnp.float32
BF16 = jnp.bfloat16

D_MODEL = 2048
C_CONV = 512
CONV_K = 31
DN_HEADS = 8
DN_HEAD_DIM = 128
DN_WIDTH = DN_HEADS * DN_HEAD_DIM
DN_CONV_K = 4
MEM_HEADS = 4
MEM_HEAD_DIM = 128
MEM_WIDTH = MEM_HEADS * MEM_HEAD_DIM
MEM_TOKENS = 256
DEPTH = 1
DEEPNORM_ALPHA = (2 * DEPTH) ** 0.25
LN_EPS = 1e-5
NORM_EPS = 1e-6

N_MAIN = 3 * DN_WIDTH + DN_WIDTH + 3 * C_CONV + 2 * MEM_WIDTH
COL_BLOCK = 512
QKV_BLK = 0
DNGATE_BLK = 3
GLUA_BLK, GLUB_BLK, CGATE_BLK, MEMQ_BLK, MEMG_BLK = 8, 9, 10, 11, 12
SMALL_PAD = 128
CONV_HALO = 32
QKV_HALO = 8

VMEM_LIMIT = 56 * 1024 * 1024


def _sigmoid(x):
    return 1.0 / (1.0 + jnp.exp(-x))


def _silu(x):
    return x * _sigmoid(x)


def _softplus(x):
    return jnp.maximum(x, 0.0) + jnp.log(1.0 + jnp.exp(-jnp.abs(x)))


def _cparams(sem):
    return pltpu.CompilerParams(dimension_semantics=sem, vmem_limit_bytes=VMEM_LIMIT)


def _proj_kernel(x_ref, w_ref, ws_ref, wst_ref, o_ref, bdc_ref, bdr_ref, xb_ref):
    @pl.when(pl.program_id(1) == 0)
    def _():
        xb = x_ref[...].astype(BF16)
        xb_ref[...] = xb
        bdc_ref[...] = jnp.dot(xb, ws_ref[...], preferred_element_type=F32)
        bdr_ref[...] = lax.dot_general(wst_ref[...], xb, (((1,), (1,)), ((), ())),
                                       preferred_element_type=F32)

    o_ref[...] = jnp.dot(xb_ref[...], w_ref[...], preferred_element_type=F32)


def _in_proj(x2, w_main, w_small, w_small_t, bm):
    m = x2.shape[0]
    bn = COL_BLOCK
    return pl.pallas_call(
        _proj_kernel,
        grid=(m // bm, N_MAIN // bn),
        in_specs=[
            pl.BlockSpec((bm, D_MODEL), lambda i, j: (i, 0)),
            pl.BlockSpec((D_MODEL, bn), lambda i, j: (0, j)),
            pl.BlockSpec((D_MODEL, SMALL_PAD), lambda i, j: (0, 0)),
            pl.BlockSpec((2 * DN_HEADS, D_MODEL), lambda i, j: (0, 0)),
        ],
        out_specs=[
            pl.BlockSpec((bm, bn), lambda i, j: (i, j)),
            pl.BlockSpec((bm, SMALL_PAD), lambda i, j: (i, 0)),
            pl.BlockSpec((2 * DN_HEADS, bm), lambda i, j: (0, i)),
        ],
        out_shape=[
            jax.ShapeDtypeStruct((m, N_MAIN), F32),
            jax.ShapeDtypeStruct((m, SMALL_PAD), F32),
            jax.ShapeDtypeStruct((2 * DN_HEADS, m), F32),
        ],
        scratch_shapes=[pltpu.VMEM((bm, D_MODEL), BF16)],
        compiler_params=_cparams(("arbitrary", "arbitrary")),
        name="in_proj",
    )(x2, w_main, w_small, w_small_t)


def _mm_kernel(x_ref, w_ref, o_ref):
    o_ref[...] = jnp.dot(x_ref[...].astype(BF16), w_ref[...], preferred_element_type=F32)


def _matmul(x2, w, bm, bn):
    m, k = x2.shape
    n = w.shape[1]
    return pl.pallas_call(
        _mm_kernel,
        grid=(m // bm, n // bn),
        in_specs=[pl.BlockSpec((bm, k), lambda i, j: (i, 0)),
                  pl.BlockSpec((k, bn), lambda i, j: (0, j))],
        out_specs=pl.BlockSpec((bm, bn), lambda i, j: (i, j)),
        out_shape=jax.ShapeDtypeStruct((m, n), F32),
        compiler_params=_cparams(("arbitrary", "arbitrary")),
        name="mem_kv_proj",
    )(x2, w)


def _conv_kernel(*refs, tile, nb, has_state):
    if has_state:
        (a_ref, b_ref, g_ref, st_ref, w_ref, p_ref, o_ref, so_ref, xp_ref) = refs
    else:
        (a_ref, b_ref, g_ref, w_ref, p_ref, o_ref, so_ref, xp_ref) = refs
        st_ref = None
    t = pl.program_id(1)
    last = pl.num_programs(1) - 1
    lo = CONV_HALO - (CONV_K - 1)
    w = w_ref[...]
    conv_b = p_ref[0:1, :]
    ln_g = p_ref[1:2, :]
    ln_b = p_ref[2:3, :]
    for i in range(nb):
        @pl.when(t == 0)
        def _():
            xp_ref[i, 0:CONV_HALO, :] = jnp.zeros((CONV_HALO, C_CONV), F32)
            if has_state:
                xp_ref[i, lo:CONV_HALO, :] = st_ref[i]

        u = a_ref[i] * _sigmoid(b_ref[i])
        xp_ref[i, CONV_HALO:CONV_HALO + tile, :] = u
        acc = jnp.zeros((tile, C_CONV), F32)
        for j in range(CONV_K):
            acc = acc + xp_ref[i, lo + j:lo + j + tile, :] * w[j:j + 1, :]
        hc = acc + conv_b
        mu = jnp.mean(hc, axis=-1, keepdims=True)
        xc = hc - mu
        var = jnp.mean(xc * xc, axis=-1, keepdims=True)
        hn = xc * lax.rsqrt(var + LN_EPS) * ln_g + ln_b
        o_ref[i] = (_silu(hn) * _silu(g_ref[i])).astype(o_ref.dtype)

        @pl.when(t == last)
        def _():
            so_ref[i] = xp_ref[i, tile + lo:tile + CONV_HALO, :]

        xp_ref[i, 0:CONV_HALO, :] = xp_ref[i, tile:tile + CONV_HALO, :]


def _conv_group(proj3, state, conv_w, conv_p, tile, nb):
    b, l, _ = proj3.shape
    has_state = state is not None
    cb = COL_BLOCK
    in_specs = [
        pl.BlockSpec((nb, tile, cb), lambda i, t: (i, t, GLUA_BLK)),
        pl.BlockSpec((nb, tile, cb), lambda i, t: (i, t, GLUB_BLK)),
        pl.BlockSpec((nb, tile, cb), lambda i, t: (i, t, CGATE_BLK)),
    ]
    args = [proj3, proj3, proj3]
    if has_state:
        in_specs.append(pl.BlockSpec((nb, CONV_K - 1, C_CONV), lambda i, t: (i, 0, 0)))
        args.append(state)
    in_specs += [pl.BlockSpec((CONV_K, C_CONV), lambda i, t: (0, 0)),
                 pl.BlockSpec((8, C_CONV), lambda i, t: (0, 0))]
    args += [conv_w, conv_p]
    return pl.pallas_call(
        functools.partial(_conv_kernel, tile=tile, nb=nb, has_state=has_state),
        grid=(b // nb, l // tile),
        in_specs=in_specs,
        out_specs=[pl.BlockSpec((nb, tile, C_CONV), lambda i, t: (i, t, 0)),
                   pl.BlockSpec((nb, CONV_K - 1, C_CONV), lambda i, t: (i, 0, 0))],
        out_shape=[jax.ShapeDtypeStruct((b, l, C_CONV), BF16),
                   jax.ShapeDtypeStruct((b, CONV_K - 1, C_CONV), F32)],
        scratch_shapes=[pltpu.VMEM((nb, tile + CONV_HALO, C_CONV), F32)],
        compiler_params=_cparams(("arbitrary", "arbitrary")),
        name="conv_group",
    )(*args)


def _bdot(a, b, dims=None):
    a = a.astype(BF16)
    b = b.astype(BF16)
    if dims is None:
        return jnp.dot(a, b, preferred_element_type=F32)
    return lax.dot_general(a, b, (dims, ((), ())), preferred_element_type=F32)


_NT = ((1,), (1,))
_TN = ((0,), (0,))


def _split_bf16(a):
    hi = a.astype(BF16)
    return hi, (a - hi.astype(F32)).astype(BF16)


def _unit_lower_inverses(ms, c):
    row = lax.broadcasted_iota(jnp.int32, (c, c), 0)
    col = lax.broadcasted_iota(jnp.int32, (c, c), 1)
    eye = (row == col).astype(F32)
    ps = [-m for m in ms]
    ts = [eye + p for p in ps]
    span = 2
    while span < c:
        ps = [_bdot(p, p) for p in ps]
        ts = [t + _bdot(t, p) for t, p in zip(ts, ps)]
        span *= 2
    msp = [_split_bf16(m) for m in ms]
    tsp = [_split_bf16(t) for t in ts]
    mts = [_bdot(mh, th) + _bdot(mh, tl) + _bdot(ml, th) for (mh, ml), (th, tl) in zip(msp, tsp)]
    rs = [eye - t - mt for t, mt in zip(ts, mts)]
    return [t + _bdot(th, r) for t, (th, _), r in zip(ts, tsp, rs)]


def _dn_kernel(*refs, tile, chunk, nb, has_state):
    if has_state:
        (qkv_ref, gate_ref, bdc_ref, bdr_ref, prev_ref, s0_ref, w_ref, pr_ref, pc_ref, nw_ref,
         o_ref, st_ref, s_ref, xp_ref, qkvc_ref, osc_ref) = refs
    else:
        (qkv_ref, gate_ref, bdc_ref, bdr_ref, w_ref, pr_ref, pc_ref, nw_ref,
         o_ref, st_ref, s_ref, xp_ref, qkvc_ref, osc_ref) = refs
        prev_ref = s0_ref = None
    c = chunk
    nchunk = tile // c
    hd = DN_HEAD_DIM
    t = pl.program_id(1)
    last = pl.num_programs(1) - 1
    lo = QKV_HALO - (DN_CONV_K - 1)
    hi = lax.Precision.HIGHEST
    w = w_ref[...]

    for i in range(nb):
        @pl.when(t == 0)
        def _():
            xp_ref[i, 0:QKV_HALO, :] = jnp.zeros((QKV_HALO, 3 * DN_WIDTH), F32)
            if has_state:
                xp_ref[i, lo:QKV_HALO, :] = prev_ref[i]
                s_ref[i] = s0_ref[i]
            else:
                s_ref[i] = jnp.zeros((DN_HEADS, hd, hd), F32)

        xp_ref[i, QKV_HALO:QKV_HALO + tile, :] = qkv_ref[i]
        acc = xp_ref[i, lo:lo + tile, :] * w[0:1, :]
        for j in range(1, DN_CONV_K):
            acc = acc + xp_ref[i, lo + j:lo + j + tile, :] * w[j:j + 1, :]
        qkvc_ref[i] = _silu(acc)

        @pl.when(t == last)
        def _():
            st_ref[i] = xp_ref[i, tile + lo:tile + QKV_HALO, :]

        xp_ref[i, 0:QKV_HALO, :] = xp_ref[i, tile:tile + QKV_HALO, :]

    neg_a_row = -jnp.exp(pr_ref[0:1, :])
    dtb_row = pr_ref[1:2, :]
    neg_a_col = -jnp.exp(pc_ref[:, 0:1])
    dtb_col = pc_ref[:, 1:2]
    norm_w = nw_ref[...]

    row = lax.broadcasted_iota(jnp.int32, (c, c), 0)
    col = lax.broadcasted_iota(jnp.int32, (c, c), 1)
    incl = row >= col
    strict = row > col
    tri_l = incl.astype(F32)
    tri_u = (row <= col).astype(F32)

    probs = [(i, h) for i in range(nb) for h in range(DN_HEADS)]

    def chunk_body(n, carry):
        r0 = n * c if isinstance(n, int) else pl.multiple_of(n * c, c)
        rows = pl.ds(r0, c)
        beta, gcum_col, gcum_row = [], [], []
        for i in range(nb):
            bd = bdc_ref[i, rows, :]
            beta.append(_sigmoid(bd[:, 0:DN_HEADS]))
            g_col = neg_a_row * _softplus(bd[:, DN_HEADS:2 * DN_HEADS] + dtb_row)
            gcum_col.append(jnp.dot(tri_l, g_col, precision=hi, preferred_element_type=F32))
            br = bdr_ref[i, n]
            g_row = neg_a_col * _softplus(br[DN_HEADS:2 * DN_HEADS, :] + dtb_col)
            gcum_row.append(jnp.dot(g_row, tri_u, precision=hi, preferred_element_type=F32))

        def head_lanes(i, h, part):
            return qkvc_ref[i, rows, part * DN_WIDTH + h * hd:part * DN_WIDTH + (h + 1) * hd]

        qs = [head_lanes(i, h, 0) for i, h in probs]
        ks = [head_lanes(i, h, 1) for i, h in probs]
        vs = [head_lanes(i, h, 2) for i, h in probs]
        qs = [q * (lax.rsqrt(jnp.sum(q * q, axis=-1, keepdims=True) + NORM_EPS) * (hd ** -0.5)) for q in qs]
        ks = [k * lax.rsqrt(jnp.sum(k * k, axis=-1, keepdims=True) + NORM_EPS) for k in ks]
        gcs = [gcum_col[i][:, h:h + 1] for i, h in probs]
        grs = [gcum_row[i][h:h + 1, :] for i, h in probs]
        bts = [beta[i][:, h:h + 1] for i, h in probs]
        gls = [gcum_col[i][c - 1:c, h:h + 1] for i, h in probs]
        decays = [jnp.exp(jnp.where(incl, gc - gr, -jnp.inf)) for gc, gr in zip(gcs, grs)]
        egcs = [jnp.exp(gc) for gc in gcs]
        kbs = [k * bt for k, bt in zip(ks, bts)]
        k16s = [k.astype(BF16) for k in ks]
        kks = [_bdot(kb, k16, _NT) for kb, k16 in zip(kbs, k16s)]
        aqks = [_bdot(q, k16, _NT) * dc for q, k16, dc in zip(qs, k16s, decays)]
        ms = [jnp.where(strict, kk * dc, 0.0) for kk, dc in zip(kks, decays)]
        t16s = [tinv.astype(BF16) for tinv in _unit_lower_inverses(ms, c)]
        u0s = [_bdot(t16, v * bt) for t16, v, bt in zip(t16s, vs, bts)]
        wms = [_bdot(t16, kb * e) for t16, kb, e in zip(t16s, kbs, egcs)]
        qds = [(q * e).astype(BF16) for q, e in zip(qs, egcs)]
        kds = [(k * jnp.exp(gl - gc)).astype(BF16) for k, gl, gc in zip(ks, gls, gcs)]
        ss = [s_ref[i, h] for i, h in probs]
        s16s = [s.astype(BF16) for s in ss]
        u16s = [(u0 - _bdot(wm, s16)).astype(BF16) for u0, wm, s16 in zip(u0s, wms, s16s)]
        outs = [_bdot(qd, s16) + _bdot(aqk, u16) for qd, s16, aqk, u16 in zip(qds, s16s, aqks, u16s)]
        for (i, h), s, gl, kd, u16 in zip(probs, ss, gls, kds, u16s):
            s_ref[i, h] = s * jnp.exp(gl) + _bdot(kd, u16, _TN)
        for (i, h), o in zip(probs, outs):
            o = o * lax.rsqrt(jnp.mean(o * o, axis=-1, keepdims=True) + NORM_EPS) * norm_w
            osc_ref[i, rows, h * hd:(h + 1) * hd] = o
        return carry

    if nchunk == 1:
        chunk_body(0, 0)
    else:
        lax.fori_loop(0, nchunk, chunk_body, 0)
    for i in range(nb):
        o_ref[i] = (osc_ref[i] * _silu(gate_ref[i])).astype(o_ref.dtype)


def _dn_group(proj3, bdc3, bdr4, prev, s0, qkv_w, p_row, p_col, norm_w, tile, chunk, nb):
    b, l, _ = proj3.shape
    has_state = s0 is not None
    nchunk = tile // chunk
    hd = DN_HEAD_DIM
    in_specs = [
        pl.BlockSpec((nb, tile, 3 * DN_WIDTH), lambda i, t: (i, t, QKV_BLK)),
        pl.BlockSpec((nb, tile, DN_WIDTH), lambda i, t: (i, t, DNGATE_BLK)),
        pl.BlockSpec((nb, tile, SMALL_PAD), lambda i, t: (i, t, 0)),
        pl.BlockSpec((nb, nchunk, 2 * DN_HEADS, chunk), lambda i, t: (i, t, 0, 0)),
    ]
    args = [proj3, proj3, bdc3, bdr4]
    if has_state:
        in_specs += [pl.BlockSpec((nb, DN_CONV_K - 1, 3 * DN_WIDTH), lambda i, t: (i, 0, 0)),
                     pl.BlockSpec((nb, DN_HEADS, hd, hd), lambda i, t: (i, 0, 0, 0))]
        args += [prev, s0]
    in_specs += [pl.BlockSpec((DN_CONV_K, 3 * DN_WIDTH), lambda i, t: (0, 0)),
                 pl.BlockSpec((2, DN_HEADS), lambda i, t: (0, 0)),
                 pl.BlockSpec((DN_HEADS, 2), lambda i, t: (0, 0)),
                 pl.BlockSpec((1, hd), lambda i, t: (0, 0))]
    args += [qkv_w, p_row, p_col, norm_w]
    return pl.pallas_call(
        functools.partial(_dn_kernel, tile=tile, chunk=chunk, nb=nb, has_state=has_state),
        grid=(b // nb, l // tile),
        in_specs=in_specs,
        out_specs=[pl.BlockSpec((nb, tile, DN_WIDTH), lambda i, t: (i, t, 0)),
                   pl.BlockSpec((nb, DN_CONV_K - 1, 3 * DN_WIDTH), lambda i, t: (i, 0, 0)),
                   pl.BlockSpec((nb, DN_HEADS, hd, hd), lambda i, t: (i, 0, 0, 0))],
        out_shape=[jax.ShapeDtypeStruct((b, l, DN_WIDTH), BF16),
                   jax.ShapeDtypeStruct((b, DN_CONV_K - 1, 3 * DN_WIDTH), F32),
                   jax.ShapeDtypeStruct((b, DN_HEADS, hd, hd), F32)],
        scratch_shapes=[pltpu.VMEM((nb, tile + QKV_HALO, 3 * DN_WIDTH), F32),
                        pltpu.VMEM((nb, tile, 3 * DN_WIDTH), F32),
                        pltpu.VMEM((nb, tile, DN_WIDTH), F32)],
        compiler_params=_cparams(("arbitrary", "arbitrary")),
        name="deltanet_group",
    )(*args)


def _mem_kernel(q_ref, g_ref, k_ref, v_ref, o_ref, *, nb):
    hd = MEM_HEAD_DIM
    scale = hd ** -0.5
    probs = [(i, h) for i in range(nb) for h in range(MEM_HEADS)]
    lanes = [slice(h * hd, (h + 1) * hd) for h in range(MEM_HEADS)]
    qs = [q_ref[i, :, lanes[h]] for i, h in probs]
    ks = [k_ref[i, pl.ds(h, MEM_TOKENS, stride=MEM_HEADS), :] for i, h in probs]
    vs = [v_ref[i, pl.ds(h, MEM_TOKENS, stride=MEM_HEADS), :] for i, h in probs]
    ss = [_bdot(q, k, _NT) * scale for q, k in zip(qs, ks)]
    es = [jnp.exp(s - jnp.max(s, axis=-1, keepdims=True)) for s in ss]
    oms = [_bdot(e, v) / jnp.sum(e, axis=-1, keepdims=True) for e, v in zip(es, vs)]
    for (i, h), om in zip(probs, oms):
        o_ref[i, :, lanes[h]] = (om * _silu(g_ref[i, :, lanes[h]])).astype(o_ref.dtype)


def _mem_group(proj3, mem_k, mem_v, tile, nb):
    b, l, _ = proj3.shape
    cb = COL_BLOCK
    rows = MEM_TOKENS * MEM_HEADS
    return pl.pallas_call(
        functools.partial(_mem_kernel, nb=nb),
        grid=(b // nb, l // tile),
        in_specs=[pl.BlockSpec((nb, tile, cb), lambda i, t: (i, t, MEMQ_BLK)),
                  pl.BlockSpec((nb, tile, cb), lambda i, t: (i, t, MEMG_BLK)),
                  pl.BlockSpec((nb, rows, MEM_HEAD_DIM), lambda i, t: (i, 0, 0)),
                  pl.BlockSpec((nb, rows, MEM_HEAD_DIM), lambda i, t: (i, 0, 0))],
        out_specs=pl.BlockSpec((nb, tile, MEM_WIDTH), lambda i, t: (i, t, 0)),
        out_shape=jax.ShapeDtypeStruct((b, l, MEM_WIDTH), BF16),
        compiler_params=_cparams(("arbitrary", "arbitrary")),
        name="memory_group",
    )(proj3, proj3, mem_k, mem_v)


def _out_kernel(oc_ref, od_ref, om_ref, x_ref, w_ref, p_ref, y_ref):
    h = jnp.dot(oc_ref[...], w_ref[0:C_CONV, :], preferred_element_type=F32)
    h = h + jnp.dot(od_ref[...], w_ref[C_CONV:C_CONV + DN_WIDTH, :], preferred_element_type=F32)
    h = h + jnp.dot(om_ref[...], w_ref[C_CONV + DN_WIDTH:, :], preferred_element_type=F32)
    z = DEEPNORM_ALPHA * x_ref[...] + h
    mu = jnp.mean(z, axis=-1, keepdims=True)
    zc = z - mu
    var = jnp.mean(zc * zc, axis=-1, keepdims=True)
    y_ref[...] = zc * lax.rsqrt(var + LN_EPS) * p_ref[0:1, :] + p_ref[1:2, :]


def _out_proj(oc, od, om, x2, w_out, ln_p, bm):
    m = x2.shape[0]
    return pl.pallas_call(
        _out_kernel,
        grid=(m // bm,),
        in_specs=[pl.BlockSpec((bm, C_CONV), lambda i: (i, 0)),
                  pl.BlockSpec((bm, DN_WIDTH), lambda i: (i, 0)),
                  pl.BlockSpec((bm, MEM_WIDTH), lambda i: (i, 0)),
                  pl.BlockSpec((bm, D_MODEL), lambda i: (i, 0)),
                  pl.BlockSpec((D_MODEL, D_MODEL), lambda i: (0, 0)),
                  pl.BlockSpec((8, D_MODEL), lambda i: (0, 0))],
        out_specs=pl.BlockSpec((bm, D_MODEL), lambda i: (i, 0)),
        out_shape=jax.ShapeDtypeStruct((m, D_MODEL), F32),
        compiler_params=_cparams(("arbitrary",)),
        name="out_proj_ln",
    )(oc, od, om, x2, w_out, ln_p)


def _pad_rows(rows, n):
    a = jnp.stack(rows).astype(F32)
    return jnp.pad(a, ((0, n - a.shape[0]), (0, 0)))


def _layer(x, conv_prev, qkv_prev, s0, mem_k, mem_v, wts, *, bm, conv_tile, conv_nb, dn_tile, dn_chunk, dn_nb,
           mem_tile, mem_nb, out_bm):
    (w_main, w_small, w_small_t, conv_w, conv_p, qkv_w, p_row, p_col, norm_w, w_out, ln_p) = wts
    b, l, _ = x.shape
    x2 = x.reshape(b * l, D_MODEL)
    proj, bdc, bdr = _in_proj(x2, w_main, w_small, w_small_t, bm)
    proj3 = proj.reshape(b, l, N_MAIN)
    bdc3 = bdc.reshape(b, l, SMALL_PAD)
    bdr4 = bdr.reshape(2 * DN_HEADS, b, l // dn_chunk, dn_chunk).transpose(1, 2, 0, 3)

    oc, conv_state = _conv_group(proj3, conv_prev, conv_w, conv_p, conv_tile, conv_nb)
    od, qkv_state, s_final = _dn_group(proj3, bdc3, bdr4, qkv_prev, s0, qkv_w, p_row, p_col, norm_w,
                                       dn_tile, dn_chunk, dn_nb)
    om = _mem_group(proj3, mem_k, mem_v, mem_tile, mem_nb)
    y = _out_proj(oc.reshape(b * l, C_CONV), od.reshape(b * l, DN_WIDTH), om.reshape(b * l, MEM_WIDTH),
                  x2, w_out, ln_p, out_bm)
    return y.reshape(b, l, D_MODEL), conv_state, qkv_state, s_final


def kernel(x_prompt, x_sample, mem_prompt, state_conv, state_qkv_conv, state_delta, cache_mem_k, cache_mem_v,
           w_in, conv_w, conv_b, conv_ln_g, conv_ln_b, qkv_conv_w, a_log, dt_bias, delta_norm_w,
           w_mem_k, w_mem_v, w_out, ln_g, ln_b):
    bp = x_prompt.shape[0]
    bs = x_sample.shape[0]
    hp, hs = x_prompt, x_sample
    outs = [[] for _ in range(8)]
    for i in range(DEPTH):
        wi = w_in[i]
        c3 = 3 * C_CONV
        o_qkv, o_gate = c3, c3 + 3 * DN_WIDTH
        o_beta = o_gate + DN_WIDTH
        o_memq = o_beta + 2 * DN_HEADS
        w_main = jnp.concatenate(
            [wi[:, o_qkv:o_gate], wi[:, o_gate:o_beta], wi[:, 0:c3], wi[:, o_memq:]], axis=1).astype(BF16)
        w_bd = wi[:, o_beta:o_memq]
        w_small = jnp.pad(w_bd, ((0, 0), (0, SMALL_PAD - 2 * DN_HEADS))).astype(BF16)
        w_small_t = w_bd.T.astype(BF16)
        conv_p = _pad_rows([conv_b[i], conv_ln_g[i], conv_ln_b[i]], 8)
        p_row = jnp.stack([a_log[i], dt_bias[i]]).astype(F32)
        wts = (w_main, w_small, w_small_t, conv_w[i], conv_p, qkv_conv_w[i], p_row, p_row.T,
               delta_norm_w[i].reshape(1, DN_HEAD_DIM), w_out[i].astype(BF16),
               _pad_rows([ln_g[i], ln_b[i]], 8))

        w_kv = jnp.concatenate([w_mem_k[i], w_mem_v[i]], axis=1).astype(BF16)
        kv = _matmul(mem_prompt.reshape(bp * MEM_TOKENS, D_MODEL), w_kv, 512, 512)
        mk = kv[:, :MEM_WIDTH].reshape(bp, MEM_TOKENS, MEM_HEADS, MEM_HEAD_DIM)
        mv = kv[:, MEM_WIDTH:].reshape(bp, MEM_TOKENS, MEM_HEADS, MEM_HEAD_DIM)
        kv_rows = MEM_TOKENS * MEM_HEADS
        hp, c_st, q_st, s_st = _layer(
            hp, None, None, None, mk.reshape(bp, kv_rows, MEM_HEAD_DIM), mv.reshape(bp, kv_rows, MEM_HEAD_DIM),
            wts, bm=1024, conv_tile=512, conv_nb=1, dn_tile=256, dn_chunk=64, dn_nb=1, mem_tile=512, mem_nb=1,
            out_bm=512)
        outs[0].append(c_st); outs[1].append(q_st); outs[2].append(s_st)
        outs[3].append(mk)
        outs[4].append(mv)

        ls = hs.shape[1]
        hs, c_st, q_st, s_st = _layer(
            hs, state_conv[i], state_qkv_conv[i], state_delta[i],
            cache_mem_k[i].reshape(bs, kv_rows, MEM_HEAD_DIM), cache_mem_v[i].reshape(bs, kv_rows, MEM_HEAD_DIM),
            wts, bm=bs * ls, conv_tile=ls, conv_nb=8, dn_tile=ls, dn_chunk=ls, dn_nb=4, mem_tile=ls, mem_nb=8,
            out_bm=512)
        outs[5].append(c_st); outs[6].append(q_st); outs[7].append(s_st)
    return (hp, hs) + tuple(jnp.stack(o) for o in outs)
```

```python
import functools

import jax
import jax.numpy as jnp
from jax import lax
from jax.experimental import pallas as pl
from jax.experimental.pallas import tpu as pltpu

F32 = jnp.float32
BF16 = jnp.bfloat16

D_MODEL = 2048
C_CONV = 512
CONV_K = 31
DN_HEADS = 8
DN_HEAD_DIM = 128
DN_WIDTH = DN_HEADS * DN_HEAD_DIM
DN_CONV_K = 4
MEM_HEADS = 4
MEM_HEAD_DIM = 128
MEM_WIDTH = MEM_HEADS * MEM_HEAD_DIM
MEM_TOKENS = 256
DEPTH = 1
DEEPNORM_ALPHA = (2 * DEPTH) ** 0.25
LN_EPS = 1e-5
NORM_EPS = 1e-6

N_MAIN = 3 * DN_WIDTH + DN_WIDTH + 3 * C_CONV + 2 * MEM_WIDTH
COL_BLOCK = 512
QKV_BLK = 0
DNGATE_BLK = 3
GLUA_BLK, GLUB_BLK, CGATE_BLK, MEMQ_BLK, MEMG_BLK = 8, 9, 10, 11, 12
SMALL_PAD = 128
N_DIRECT_BLOCKS = (3 * C_CONV + 4 * DN_WIDTH) // COL_BLOCK
CONV_HALO = 32
QKV_HALO = 8

VMEM_LIMIT = 56 * 1024 * 1024


def _sigmoid(x):
    return 1.0 / (1.0 + jnp.exp(-x))


def _silu(x):
    return x * _sigmoid(x)


def _softplus(x):
    return jnp.maximum(x, 0.0) + jnp.log(1.0 + jnp.exp(-jnp.abs(x)))


def _cparams(sem):
    return pltpu.CompilerParams(dimension_semantics=sem, vmem_limit_bytes=VMEM_LIMIT)


def _proj_kernel(x_ref, wa_ref, wb_ref, o_ref, bdc_ref, bdr_ref, xb_ref):
    j = pl.program_id(1)
    nt = (((1,), (1,)), ((), ()))
    n_bd = 2 * DN_HEADS

    @pl.when(j == 0)
    def _():
        xb_ref[...] = x_ref[...].astype(BF16)

    @pl.when(j < N_DIRECT_BLOCKS)
    def _():
        o_ref[...] = lax.dot_general(xb_ref[...], wa_ref[...].astype(BF16), nt, preferred_element_type=F32)

    @pl.when(j >= N_DIRECT_BLOCKS)
    def _():
        w = jnp.concatenate([wa_ref[n_bd:, :], wb_ref[0:n_bd, :]], axis=0).astype(BF16)
        o_ref[...] = lax.dot_general(xb_ref[...], w, nt, preferred_element_type=F32)

    @pl.when(j == N_DIRECT_BLOCKS)
    def _():
        xb = xb_ref[...]
        bdc_ref[...] = lax.dot_general(xb, wa_ref[0:SMALL_PAD, :].astype(BF16), nt, preferred_element_type=F32)
        bdr_ref[...] = lax.dot_general(wa_ref[0:n_bd, :].astype(BF16), xb, nt, preferred_element_type=F32)


def _proj_out_block(j):
    n_conv = 3 * C_CONV // COL_BLOCK
    return jnp.where(j < n_conv, j + GLUA_BLK, jnp.where(j < N_DIRECT_BLOCKS, j - n_conv, j))


def _in_proj(x2, w_in_t, layer, bm):
    m = x2.shape[0]
    bn = COL_BLOCK
    n_blocks = N_MAIN // bn
    return pl.pallas_call(
        _proj_kernel,
        grid=(m // bm, n_blocks),
        in_specs=[
            pl.BlockSpec((bm, D_MODEL), lambda i, j: (i, 0)),
            pl.BlockSpec((None, bn, D_MODEL), lambda i, j: (layer, j, 0)),
            pl.BlockSpec((None, bn, D_MODEL), lambda i, j: (layer, jnp.maximum(j + 1, n_blocks - 1), 0)),
        ],
        out_specs=[
            pl.BlockSpec((bm, bn), lambda i, j: (i, _proj_out_block(j))),
            pl.BlockSpec((bm, SMALL_PAD), lambda i, j: (i, 0)),
            pl.BlockSpec((2 * DN_HEADS, bm), lambda i, j: (0, i)),
        ],
        out_shape=[
            jax.ShapeDtypeStruct((m, N_MAIN), F32),
            jax.ShapeDtypeStruct((m, SMALL_PAD), F32),
            jax.ShapeDtypeStruct((2 * DN_HEADS, m), F32),
        ],
        scratch_shapes=[pltpu.VMEM((bm, D_MODEL), BF16)],
        compiler_params=_cparams(("arbitrary", "arbitrary")),
        name="in_proj",
    )(x2, w_in_t, w_in_t)


def _mm_kernel(x_ref, w_ref, o_ref):
    o_ref[...] = jnp.dot(x_ref[...].astype(BF16), w_ref[...], preferred_element_type=F32)


def _matmul(x2, w, bm, bn):
    m, k = x2.shape
    n = w.shape[1]
    return pl.pallas_call(
        _mm_kernel,
        grid=(m // bm, n // bn),
        in_specs=[pl.BlockSpec((bm, k), lambda i, j: (i, 0)),
                  pl.BlockSpec((k, bn), lambda i, j: (0, j))],
        out_specs=pl.BlockSpec((bm, bn), lambda i, j: (i, j)),
        out_shape=jax.ShapeDtypeStruct((m, n), F32),
        compiler_params=_cparams(("arbitrary", "arbitrary")),
        name="mem_kv_proj",
    )(x2, w)


def _conv_kernel(*refs, tile, nb, has_state):
    if has_state:
        (a_ref, b_ref, g_ref, st_ref, w_ref, p_ref, o_ref, so_ref, xp_ref) = refs
    else:
        (a_ref, b_ref, g_ref, w_ref, p_ref, o_ref, so_ref, xp_ref) = refs
        st_ref = None
    t = pl.program_id(1)
    last = pl.num_programs(1) - 1
    lo = CONV_HALO - (CONV_K - 1)
    w = w_ref[...]
    conv_b = p_ref[0:1, :]
    ln_g = p_ref[1:2, :]
    ln_b = p_ref[2:3, :]
    for i in range(nb):
        @pl.when(t == 0)
        def _():
            xp_ref[i, 0:CONV_HALO, :] = jnp.zeros((CONV_HALO, C_CONV), F32)
            if has_state:
                xp_ref[i, lo:CONV_HALO, :] = st_ref[i]

        u = a_ref[i] * _sigmoid(b_ref[i])
        xp_ref[i, CONV_HALO:CONV_HALO + tile, :] = u
        acc = jnp.zeros((tile, C_CONV), F32)
        for j in range(CONV_K):
            acc = acc + xp_ref[i, lo + j:lo + j + tile, :] * w[j:j + 1, :]
        hc = acc + conv_b
        mu = jnp.mean(hc, axis=-1, keepdims=True)
        xc = hc - mu
        var = jnp.mean(xc * xc, axis=-1, keepdims=True)
        hn = xc * lax.rsqrt(var + LN_EPS) * ln_g + ln_b
        o_ref[i] = (_silu(hn) * _silu(g_ref[i])).astype(o_ref.dtype)

        @pl.when(t == last)
        def _():
            so_ref[i] = xp_ref[i, tile + lo:tile + CONV_HALO, :]

        xp_ref[i, 0:CONV_HALO, :] = xp_ref[i, tile:tile + CONV_HALO, :]


def _conv_group(proj3, state, conv_w, conv_p, tile, nb):
    b, l, _ = proj3.shape
    has_state = state is not None
    cb = COL_BLOCK
    in_specs = [
        pl.BlockSpec((nb, tile, cb), lambda i, t: (i, t, GLUA_BLK)),
        pl.BlockSpec((nb, tile, cb), lambda i, t: (i, t, GLUB_BLK)),
        pl.BlockSpec((nb, tile, cb), lambda i, t: (i, t, CGATE_BLK)),
    ]
    args = [proj3, proj3, proj3]
    if has_state:
        in_specs.append(pl.BlockSpec((nb, CONV_K - 1, C_CONV), lambda i, t: (i, 0, 0)))
        args.append(state)
    in_specs += [pl.BlockSpec((CONV_K, C_CONV), lambda i, t: (0, 0)),
                 pl.BlockSpec((8, C_CONV), lambda i, t: (0, 0))]
    args += [conv_w, conv_p]
    return pl.pallas_call(
        functools.partial(_conv_kernel, tile=tile, nb=nb, has_state=has_state),
        grid=(b // nb, l // tile),
        in_specs=in_specs,
        out_specs=[pl.BlockSpec((nb, tile, C_CONV), lambda i, t: (i, t, 0)),
                   pl.BlockSpec((nb, CONV_K - 1, C_CONV), lambda i, t: (i, 0, 0))],
        out_shape=[jax.ShapeDtypeStruct((b, l, C_CONV), BF16),
                   jax.ShapeDtypeStruct((b, CONV_K - 1, C_CONV), F32)],
        scratch_shapes=[pltpu.VMEM((nb, tile + CONV_HALO, C_CONV), F32)],
        compiler_params=_cparams(("arbitrary", "arbitrary")),
        name="conv_group",
    )(*args)


def _bdot(a, b, dims=None):
    a = a.astype(BF16)
    b = b.astype(BF16)
    if dims is None:
        return jnp.dot(a, b, preferred_element_type=F32)
    return lax.dot_general(a, b, (dims, ((), ())), preferred_element_type=F32)


_NT = ((1,), (1,))
_TN = ((0,), (0,))


def _split_bf16(a):
    hi = a.astype(BF16)
    return hi, (a - hi.astype(F32)).astype(BF16)


def _unit_lower_inverses(ms, c):
    row = lax.broadcasted_iota(jnp.int32, (c, c), 0)
    col = lax.broadcasted_iota(jnp.int32, (c, c), 1)
    eye = (row == col).astype(F32)
    ps = [-m for m in ms]
    ts = [eye + p for p in ps]
    span = 2
    while span < c:
        ps = [_bdot(p, p) for p in ps]
        ts = [t + _bdot(t, p) for t, p in zip(ts, ps)]
        span *= 2
    msp = [_split_bf16(m) for m in ms]
    tsp = [_split_bf16(t) for t in ts]
    mts = [_bdot(mh, th) + _bdot(mh, tl) + _bdot(ml, th) for (mh, ml), (th, tl) in zip(msp, tsp)]
    rs = [eye - t - mt for t, mt in zip(ts, mts)]
    return [t + _bdot(th, r) for t, (th, _), r in zip(ts, tsp, rs)]


def _dn_kernel(*refs, tile, chunk, nb, has_state):
    if has_state:
        (qkv_ref, gate_ref, bdc_ref, bdr_ref, prev_ref, s0_ref, w_ref, pr_ref, pc_ref, nw_ref,
         o_ref, st_ref, s_ref, xp_ref, osc_ref) = refs
    else:
        (qkv_ref, gate_ref, bdc_ref, bdr_ref, w_ref, pr_ref, pc_ref, nw_ref,
         o_ref, st_ref, s_ref, xp_ref, osc_ref) = refs
        prev_ref = s0_ref = None
    c = chunk
    nchunk = tile // c
    hd = DN_HEAD_DIM
    t = pl.program_id(1)
    last = pl.num_programs(1) - 1
    lo = QKV_HALO - (DN_CONV_K - 1)
    hi = lax.Precision.HIGHEST
    w = w_ref[...]

    for i in range(nb):
        @pl.when(t == 0)
        def _():
            xp_ref[i, 0:QKV_HALO, :] = jnp.zeros((QKV_HALO, 3 * DN_WIDTH), F32)
            if has_state:
                xp_ref[i, lo:QKV_HALO, :] = prev_ref[i]
                s_ref[i] = s0_ref[i]
            else:
                s_ref[i] = jnp.zeros((DN_HEADS, hd, hd), F32)

        xp_ref[i, QKV_HALO:QKV_HALO + tile, :] = qkv_ref[i]

        @pl.when(t == last)
        def _():
            st_ref[i] = xp_ref[i, tile + lo:tile + QKV_HALO, :]

    neg_a_row = -jnp.exp(pr_ref[0:1, :])
    dtb_row = pr_ref[1:2, :]
    neg_a_col = -jnp.exp(pc_ref[:, 0:1])
    dtb_col = pc_ref[:, 1:2]
    norm_w = nw_ref[...]

    row = lax.broadcasted_iota(jnp.int32, (c, c), 0)
    col = lax.broadcasted_iota(jnp.int32, (c, c), 1)
    incl = row >= col
    strict = row > col
    tri_l = incl.astype(F32)
    tri_u = (row <= col).astype(F32)

    seq_chunks = [(i, n) for i in range(nb) for n in range(nchunk)]
    probs = [(i, n, h) for i, n in seq_chunks for h in range(DN_HEADS)]
    beta, gcum_col, gcum_row, qkvc = {}, {}, {}, {}
    for i, n in seq_chunks:
        rows = slice(n * c, (n + 1) * c)
        bd = bdc_ref[i, rows, :]
        beta[i, n] = _sigmoid(bd[:, 0:DN_HEADS])
        g_col = neg_a_row * _softplus(bd[:, DN_HEADS:2 * DN_HEADS] + dtb_row)
        gcum_col[i, n] = jnp.dot(tri_l, g_col, precision=hi, preferred_element_type=F32)
        br = bdr_ref[i, n]
        g_row = neg_a_col * _softplus(br[DN_HEADS:2 * DN_HEADS, :] + dtb_col)
        gcum_row[i, n] = jnp.dot(g_row, tri_u, precision=hi, preferred_element_type=F32)
        base = lo + n * c
        acc = xp_ref[i, base:base + c, :] * w[0:1, :]
        for j in range(1, DN_CONV_K):
            acc = acc + xp_ref[i, base + j:base + j + c, :] * w[j:j + 1, :]
        qkvc[i, n] = _silu(acc)

    def head_lanes(i, n, h, part):
        return qkvc[i, n][:, part * DN_WIDTH + h * hd:part * DN_WIDTH + (h + 1) * hd]

    qs = [head_lanes(i, n, h, 0) for i, n, h in probs]
    ks = [head_lanes(i, n, h, 1) for i, n, h in probs]
    vs = [head_lanes(i, n, h, 2) for i, n, h in probs]
    qs = [q * (lax.rsqrt(jnp.sum(q * q, axis=-1, keepdims=True) + NORM_EPS) * (hd ** -0.5)) for q in qs]
    ks = [k * lax.rsqrt(jnp.sum(k * k, axis=-1, keepdims=True) + NORM_EPS) for k in ks]
    gcs = [gcum_col[i, n][:, h:h + 1] for i, n, h in probs]
    grs = [gcum_row[i, n][h:h + 1, :] for i, n, h in probs]
    bts = [beta[i, n][:, h:h + 1] for i, n, h in probs]
    gls = [gcum_col[i, n][c - 1:c, h:h + 1] for i, n, h in probs]
    decays = [jnp.exp(jnp.where(incl, gc - gr, -jnp.inf)) for gc, gr in zip(gcs, grs)]
    egcs = [jnp.exp(gc) for gc in gcs]
    kbs = [k * bt for k, bt in zip(ks, bts)]
    kqs = [_bdot(jnp.concatenate([kb, q], axis=0), k, _NT) for kb, q, k in zip(kbs, qs, ks)]
    aqks = [(kq[c:] * dc).astype(BF16) for kq, dc in zip(kqs, decays)]
    ms = [jnp.where(strict, kq[:c] * dc, 0.0) for kq, dc in zip(kqs, decays)]
    tinvs = _unit_lower_inverses(ms, c)
    uws = [_bdot(tinv, jnp.concatenate([v * bt, kb * e], axis=1))
           for tinv, v, bt, kb, e in zip(tinvs, vs, bts, kbs, egcs)]
    wqs = [jnp.concatenate([uw[:, hd:], q * e], axis=0).astype(BF16) for uw, q, e in zip(uws, qs, egcs)]
    kds = [(k * jnp.exp(gl - gc)).astype(BF16) for k, gl, gc in zip(ks, gls, gcs)]
    egls = [jnp.exp(gl) for gl in gls]
    local = {p: v for p, v in zip(probs, zip(uws, wqs, aqks, kds, egls))}

    heads = [(i, h) for i in range(nb) for h in range(DN_HEADS)]
    state = {(i, h): s_ref[i, h] for i, h in heads}
    for n in range(nchunk):
        rows = slice(n * c, (n + 1) * c)
        items = [(i, h) + local[i, n, h] for i, h in heads]
        s16s = [state[i, h].astype(BF16) for i, h in heads]
        wss = [_bdot(it[3], s16) for it, s16 in zip(items, s16s)]
        u16s = [(it[2][:, :hd] - ws[:c]).astype(BF16) for it, ws in zip(items, wss)]
        outs = [ws[c:] + _bdot(it[4], u16) for it, ws, u16 in zip(items, wss, u16s)]
        for it, u16 in zip(items, u16s):
            i, h = it[0], it[1]
            state[i, h] = state[i, h] * it[6] + _bdot(it[5], u16, _TN)
        for it, o in zip(items, outs):
            i, h = it[0], it[1]
            o = o * lax.rsqrt(jnp.mean(o * o, axis=-1, keepdims=True) + NORM_EPS) * norm_w
            osc_ref[i, rows, h * hd:(h + 1) * hd] = o
    for i, h in heads:
        s_ref[i, h] = state[i, h]

    for i in range(nb):
        xp_ref[i, 0:QKV_HALO, :] = xp_ref[i, tile:tile + QKV_HALO, :]
        o_ref[i] = (osc_ref[i] * _silu(gate_ref[i])).astype(o_ref.dtype)


def _dn_group(proj3, bdc3, bdr4, prev, s0, qkv_w, p_row, p_col, norm_w, tile, chunk, nb):
    b, l, _ = proj3.shape
    has_state = s0 is not None
    nchunk = tile // chunk
    hd = DN_HEAD_DIM
    in_specs = [
        pl.BlockSpec((nb, tile, 3 * DN_WIDTH), lambda i, t: (i, t, QKV_BLK)),
        pl.BlockSpec((nb, tile, DN_WIDTH), lambda i, t: (i, t, DNGATE_BLK)),
        pl.BlockSpec((nb, tile, SMALL_PAD), lambda i, t: (i, t, 0)),
        pl.BlockSpec((nb, nchunk, 2 * DN_HEADS, chunk), lambda i, t: (i, t, 0, 0)),
    ]
    args = [proj3, proj3, bdc3, bdr4]
    if has_state:
        in_specs += [pl.BlockSpec((nb, DN_CONV_K - 1, 3 * DN_WIDTH), lambda i, t: (i, 0, 0)),
                     pl.BlockSpec((nb, DN_HEADS, hd, hd), lambda i, t: (i, 0, 0, 0))]
        args += [prev, s0]
    in_specs += [pl.BlockSpec((DN_CONV_K, 3 * DN_WIDTH), lambda i, t: (0, 0)),
                 pl.BlockSpec((2, DN_HEADS), lambda i, t: (0, 0)),
                 pl.BlockSpec((DN_HEADS, 2), lambda i, t: (0, 0)),
                 pl.BlockSpec((1, hd), lambda i, t: (0, 0))]
    args += [qkv_w, p_row, p_col, norm_w]
    return pl.pallas_call(
        functools.partial(_dn_kernel, tile=tile, chunk=chunk, nb=nb, has_state=has_state),
        grid=(b // nb, l // tile),
        in_specs=in_specs,
        out_specs=[pl.BlockSpec((nb, tile, DN_WIDTH), lambda i, t: (i, t, 0)),
                   pl.BlockSpec((nb, DN_CONV_K - 1, 3 * DN_WIDTH), lambda i, t: (i, 0, 0)),
                   pl.BlockSpec((nb, DN_HEADS, hd, hd), lambda i, t: (i, 0, 0, 0))],
        out_shape=[jax.ShapeDtypeStruct((b, l, DN_WIDTH), BF16),
                   jax.ShapeDtypeStruct((b, DN_CONV_K - 1, 3 * DN_WIDTH), F32),
                   jax.ShapeDtypeStruct((b, DN_HEADS, hd, hd), F32)],
        scratch_shapes=[pltpu.VMEM((nb, tile + QKV_HALO, 3 * DN_WIDTH), F32),
                        pltpu.VMEM((nb, tile, DN_WIDTH), F32)],
        compiler_params=_cparams(("arbitrary", "arbitrary")),
        name="deltanet_group",
    )(*args)


def _mem_kernel(q_ref, g_ref, k_ref, v_ref, o_ref, *, nb):
    hd = MEM_HEAD_DIM
    scale = hd ** -0.5
    probs = [(i, h) for i in range(nb) for h in range(MEM_HEADS)]
    lanes = [slice(h * hd, (h + 1) * hd) for h in range(MEM_HEADS)]
    qs = [q_ref[i, :, lanes[h]] for i, h in probs]
    ks = [k_ref[i, pl.ds(h, MEM_TOKENS, stride=MEM_HEADS), :] for i, h in probs]
    vs = [v_ref[i, pl.ds(h, MEM_TOKENS, stride=MEM_HEADS), :] for i, h in probs]
    ss = [_bdot(q, k, _NT) * scale for q, k in zip(qs, ks)]
    es = [jnp.exp(s - jnp.max(s, axis=-1, keepdims=True)) for s in ss]
    oms = [_bdot(e, v) / jnp.sum(e, axis=-1, keepdims=True) for e, v in zip(es, vs)]
    for (i, h), om in zip(probs, oms):
        o_ref[i, :, lanes[h]] = (om * _silu(g_ref[i, :, lanes[h]])).astype(o_ref.dtype)


def _mem_group(proj3, mem_k, mem_v, tile, nb):
    b, l, _ = proj3.shape
    cb = COL_BLOCK
    rows = MEM_TOKENS * MEM_HEADS
    return pl.pallas_call(
        functools.partial(_mem_kernel, nb=nb),
        grid=(b // nb, l // tile),
        in_specs=[pl.BlockSpec((nb, tile, cb), lambda i, t: (i, t, MEMQ_BLK)),
                  pl.BlockSpec((nb, tile, cb), lambda i, t: (i, t, MEMG_BLK)),
                  pl.BlockSpec((nb, rows, MEM_HEAD_DIM), lambda i, t: (i, 0, 0)),
                  pl.BlockSpec((nb, rows, MEM_HEAD_DIM), lambda i, t: (i, 0, 0))],
        out_specs=pl.BlockSpec((nb, tile, MEM_WIDTH), lambda i, t: (i, t, 0)),
        out_shape=jax.ShapeDtypeStruct((b, l, MEM_WIDTH), BF16),
        compiler_params=_cparams(("arbitrary", "arbitrary")),
        name="memory_group",
    )(proj3, proj3, mem_k, mem_v)


def _out_kernel(oc_ref, od_ref, om_ref, x_ref, w_ref, p_ref, y_ref):
    h = jnp.dot(oc_ref[...], w_ref[0:C_CONV, :], preferred_element_type=F32)
    h = h + jnp.dot(od_ref[...], w_ref[C_CONV:C_CONV + DN_WIDTH, :], preferred_element_type=F32)
    h = h + jnp.dot(om_ref[...], w_ref[C_CONV + DN_WIDTH:, :], preferred_element_type=F32)
    z = DEEPNORM_ALPHA * x_ref[...] + h
    mu = jnp.mean(z, axis=-1, keepdims=True)
    zc = z - mu
    var = jnp.mean(zc * zc, axis=-1, keepdims=True)
    y_ref[...] = zc * lax.rsqrt(var + LN_EPS) * p_ref[0:1, :] + p_ref[1:2, :]


def _out_proj(oc, od, om, x2, w_out, ln_p, bm):
    m = x2.shape[0]
    return pl.pallas_call(
        _out_kernel,
        grid=(m // bm,),
        in_specs=[pl.BlockSpec((bm, C_CONV), lambda i: (i, 0)),
                  pl.BlockSpec((bm, DN_WIDTH), lambda i: (i, 0)),
                  pl.BlockSpec((bm, MEM_WIDTH), lambda i: (i, 0)),
                  pl.BlockSpec((bm, D_MODEL), lambda i: (i, 0)),
                  pl.BlockSpec((D_MODEL, D_MODEL), lambda i: (0, 0)),
                  pl.BlockSpec((8, D_MODEL), lambda i: (0, 0))],
        out_specs=pl.BlockSpec((bm, D_MODEL), lambda i: (i, 0)),
        out_shape=jax.ShapeDtypeStruct((m, D_MODEL), F32),
        compiler_params=_cparams(("arbitrary",)),
        name="out_proj_ln",
    )(oc, od, om, x2, w_out, ln_p)


def _pad_rows(rows, n):
    a = jnp.stack(rows).astype(F32)
    return jnp.pad(a, ((0, n - a.shape[0]), (0, 0)))


def _layer(x, conv_prev, qkv_prev, s0, mem_k, mem_v, wts, *, bm, conv_tile, conv_nb, dn_tile, dn_chunk, dn_nb,
           mem_tile, mem_nb, out_bm):
    (w_in_t, layer, conv_w, conv_p, qkv_w, p_row, p_col, norm_w, w_out, ln_p) = wts
    b, l, _ = x.shape
    x2 = x.reshape(b * l, D_MODEL)
    proj, bdc, bdr = _in_proj(x2, w_in_t, layer, bm)
    proj3 = proj.reshape(b, l, N_MAIN)
    bdc3 = bdc.reshape(b, l, SMALL_PAD)
    bdr4 = bdr.reshape(2 * DN_HEADS, b, l // dn_chunk, dn_chunk).transpose(1, 2, 0, 3)

    oc, conv_state = _conv_group(proj3, conv_prev, conv_w, conv_p, conv_tile, conv_nb)
    od, qkv_state, s_final = _dn_group(proj3, bdc3, bdr4, qkv_prev, s0, qkv_w, p_row, p_col, norm_w,
                                       dn_tile, dn_chunk, dn_nb)
    om = _mem_group(proj3, mem_k, mem_v, mem_tile, mem_nb)
    y = _out_proj(oc.reshape(b * l, C_CONV), od.reshape(b * l, DN_WIDTH), om.reshape(b * l, MEM_WIDTH),
                  x2, w_out, ln_p, out_bm)
    return y.reshape(b, l, D_MODEL), conv_state, qkv_state, s_final


def kernel(x_prompt, x_sample, mem_prompt, state_conv, state_qkv_conv, state_delta, cache_mem_k, cache_mem_v,
           w_in, conv_w, conv_b, conv_ln_g, conv_ln_b, qkv_conv_w, a_log, dt_bias, delta_norm_w,
           w_mem_k, w_mem_v, w_out, ln_g, ln_b):
    bp = x_prompt.shape[0]
    bs = x_sample.shape[0]
    hp, hs = x_prompt, x_sample
    outs = [[] for _ in range(8)]
    w_in_t = jnp.swapaxes(w_in, 1, 2)
    for i in range(DEPTH):
        conv_p = _pad_rows([conv_b[i], conv_ln_g[i], conv_ln_b[i]], 8)
        p_row = jnp.stack([a_log[i], dt_bias[i]]).astype(F32)
        wts = (w_in_t, i, conv_w[i], conv_p, qkv_conv_w[i], p_row, p_row.T,
               delta_norm_w[i].reshape(1, DN_HEAD_DIM), w_out[i].astype(BF16),
               _pad_rows([ln_g[i], ln_b[i]], 8))

        w_kv = jnp.concatenate([w_mem_k[i], w_mem_v[i]], axis=1).astype(BF16)
        kv = _matmul(mem_prompt.reshape(bp * MEM_TOKENS, D_MODEL), w_kv, 512, 512)
        mk = kv[:, :MEM_WIDTH].reshape(bp, MEM_TOKENS, MEM_HEADS, MEM_HEAD_DIM)
        mv = kv[:, MEM_WIDTH:].reshape(bp, MEM_TOKENS, MEM_HEADS, MEM_HEAD_DIM)
        kv_rows = MEM_TOKENS * MEM_HEADS
        hp, c_st, q_st, s_st = _layer(
            hp, None, None, None, mk.reshape(bp, kv_rows, MEM_HEAD_DIM), mv.reshape(bp, kv_rows, MEM_HEAD_DIM),
            wts, bm=1024, conv_tile=512, conv_nb=1, dn_tile=256, dn_chunk=64, dn_nb=1, mem_tile=512, mem_nb=1,
            out_bm=512)
        outs[0].append(c_st); outs[1].append(q_st); outs[2].append(s_st)
        outs[3].append(mk)
        outs[4].append(mv)

        ls = hs.shape[1]
        hs, c_st, q_st, s_st = _layer(
            hs, state_conv[i], state_qkv_conv[i], state_delta[i],
            cache_mem_k[i].reshape(bs, kv_rows, MEM_HEAD_DIM), cache_mem_v[i].reshape(bs, kv_rows, MEM_HEAD_DIM),
            wts, bm=bs * ls, conv_tile=ls, conv_nb=8, dn_tile=ls, dn_chunk=ls, dn_nb=4, mem_tile=ls, mem_nb=8,
            out_bm=512)
        outs[5].append(c_st); outs[6].append(q_st); outs[7].append(s_st)
    return (hp, hs) + tuple(jnp.stack(o) for o in outs)
```

```python
import functools

import jax
import jax.numpy as jnp
from jax import lax
from jax.experimental import pallas as pl
from jax.experimental.pallas import tpu as pltpu

F32 = jnp.float32
BF16 = jnp.bfloat16

D_MODEL = 2048
C_CONV = 512
CONV_K = 31
DN_HEADS = 8
DN_HEAD_DIM = 128
DN_WIDTH = DN_HEADS * DN_HEAD_DIM
DN_CONV_K = 4
MEM_HEADS = 4
MEM_HEAD_DIM = 128
MEM_WIDTH = MEM_HEADS * MEM_HEAD_DIM
MEM_TOKENS = 256
DEPTH = 1
DEEPNORM_ALPHA = (2 * DEPTH) ** 0.25
LN_EPS = 1e-5
NORM_EPS = 1e-6

N_MAIN = 3 * DN_WIDTH + DN_WIDTH + 3 * C_CONV + 2 * MEM_WIDTH
COL_BLOCK = 512
QKV_BLK = 0
DNGATE_BLK = 3
GLUA_BLK, GLUB_BLK, CGATE_BLK, MEMQ_BLK, MEMG_BLK = 8, 9, 10, 11, 12
SMALL_PAD = 128
N_DIRECT_BLOCKS = (3 * C_CONV + 4 * DN_WIDTH) // COL_BLOCK
CONV_HALO = 32
QKV_HALO = 8
SUBLANES = 8
CONV_ROW_BLOCK = 64

VMEM_LIMIT = 56 * 1024 * 1024


def _sigmoid(x):
    return 1.0 / (1.0 + jnp.exp(-x))


def _silu(x):
    return x * _sigmoid(x)


def _softplus(x):
    return jnp.maximum(x, 0.0) + jnp.log(1.0 + jnp.exp(-jnp.abs(x)))


def _cparams(sem):
    return pltpu.CompilerParams(dimension_semantics=sem, vmem_limit_bytes=VMEM_LIMIT)


def _proj_kernel(x_ref, wa_ref, wb_ref, o_ref, bdc_ref, bdr_ref, xb_ref):
    j = pl.program_id(1)
    nt = (((1,), (1,)), ((), ()))
    n_bd = 2 * DN_HEADS

    @pl.when(j == 0)
    def _():
        xb_ref[...] = x_ref[...].astype(BF16)

    @pl.when(j < N_DIRECT_BLOCKS)
    def _():
        o_ref[...] = lax.dot_general(xb_ref[...], wa_ref[...].astype(BF16), nt, preferred_element_type=F32)

    @pl.when(j >= N_DIRECT_BLOCKS)
    def _():
        w = jnp.concatenate([wa_ref[n_bd:, :], wb_ref[0:n_bd, :]], axis=0).astype(BF16)
        o_ref[...] = lax.dot_general(xb_ref[...], w, nt, preferred_element_type=F32)

    @pl.when(j == N_DIRECT_BLOCKS)
    def _():
        xb = xb_ref[...]
        bdc_ref[...] = lax.dot_general(xb, wa_ref[0:SMALL_PAD, :].astype(BF16), nt, preferred_element_type=F32)
        bdr_ref[...] = lax.dot_general(wa_ref[0:n_bd, :].astype(BF16), xb, nt, preferred_element_type=F32)


def _proj_out_block(j):
    n_conv = 3 * C_CONV // COL_BLOCK
    return jnp.where(j < n_conv, j + GLUA_BLK, jnp.where(j < N_DIRECT_BLOCKS, j - n_conv, j))


def _in_proj(x2, w_in_t, layer, bm):
    m = x2.shape[0]
    bn = COL_BLOCK
    n_blocks = N_MAIN // bn
    return pl.pallas_call(
        _proj_kernel,
        grid=(m // bm, n_blocks),
        in_specs=[
            pl.BlockSpec((bm, D_MODEL), lambda i, j: (i, 0)),
            pl.BlockSpec((None, bn, D_MODEL), lambda i, j: (layer, j, 0)),
            pl.BlockSpec((None, bn, D_MODEL), lambda i, j: (layer, jnp.maximum(j + 1, n_blocks - 1), 0)),
        ],
        out_specs=[
            pl.BlockSpec((bm, bn), lambda i, j: (i, _proj_out_block(j))),
            pl.BlockSpec((bm, SMALL_PAD), lambda i, j: (i, 0)),
            pl.BlockSpec((2 * DN_HEADS, bm), lambda i, j: (0, i)),
        ],
        out_shape=[
            jax.ShapeDtypeStruct((m, N_MAIN), F32),
            jax.ShapeDtypeStruct((m, SMALL_PAD), F32),
            jax.ShapeDtypeStruct((2 * DN_HEADS, m), F32),
        ],
        scratch_shapes=[pltpu.VMEM((bm, D_MODEL), BF16)],
        compiler_params=_cparams(("arbitrary", "arbitrary")),
        name="in_proj",
    )(x2, w_in_t, w_in_t)


def _mm_kernel(x_ref, w_ref, o_ref):
    o_ref[...] = jnp.dot(x_ref[...].astype(BF16), w_ref[...], preferred_element_type=F32)


def _matmul(x2, w, bm, bn):
    m, k = x2.shape
    n = w.shape[1]
    return pl.pallas_call(
        _mm_kernel,
        grid=(m // bm, n // bn),
        in_specs=[pl.BlockSpec((bm, k), lambda i, j: (i, 0)),
                  pl.BlockSpec((k, bn), lambda i, j: (0, j))],
        out_specs=pl.BlockSpec((bm, bn), lambda i, j: (i, j)),
        out_shape=jax.ShapeDtypeStruct((m, n), F32),
        compiler_params=_cparams(("arbitrary", "arbitrary")),
        name="mem_kv_proj",
    )(x2, w)


def _conv_kernel(*refs, tile, nb, has_state):
    if has_state:
        (a_ref, b_ref, g_ref, st_ref, w_ref, p_ref, o_ref, so_ref, xp_ref) = refs
    else:
        (a_ref, b_ref, g_ref, w_ref, p_ref, o_ref, so_ref, xp_ref) = refs
        st_ref = None
    t = pl.program_id(1)
    last = pl.num_programs(1) - 1
    lo = CONV_HALO - (CONV_K - 1)
    n = tile + CONV_HALO
    rb = min(CONV_ROW_BLOCK, tile)
    w = w_ref[...]
    conv_b = p_ref[0:1, :]
    ln_g = p_ref[1:2, :]
    ln_b = p_ref[2:3, :]
    for i in range(nb):
        @pl.when(t == 0)
        def _():
            xp_ref[i, 0, 0:CONV_HALO, :] = jnp.zeros((CONV_HALO, C_CONV), F32)
            if has_state:
                xp_ref[i, 0, lo:CONV_HALO, :] = st_ref[i]

        xp_ref[i, 0, CONV_HALO:n, :] = a_ref[i] * _sigmoid(b_ref[i])
        xp = xp_ref[i, 0]
        for s in range(1, SUBLANES):
            xp_ref[i, s] = pltpu.roll(xp, n - s, axis=0)

        for r in range(tile // rb):
            acc = None
            for j in range(CONV_K):
                s = (lo + j) % SUBLANES
                start = lo + j - s + r * rb
                term = xp_ref[i, s, start:start + rb, :] * w[j:j + 1, :]
                acc = term if acc is None else acc + term
            hc = acc + conv_b
            mu = jnp.mean(hc, axis=-1, keepdims=True)
            xc = hc - mu
            var = jnp.mean(xc * xc, axis=-1, keepdims=True)
            hn = xc * lax.rsqrt(var + LN_EPS) * ln_g + ln_b
            rows = slice(r * rb, (r + 1) * rb)
            o_ref[i, rows, :] = (_silu(hn) * _silu(g_ref[i, rows, :])).astype(o_ref.dtype)

        @pl.when(t == last)
        def _():
            so_ref[i] = xp_ref[i, 0, tile + lo:n, :]

        xp_ref[i, 0, 0:CONV_HALO, :] = xp_ref[i, 0, tile:n, :]


def _conv_group(proj3, state, conv_w, conv_p, tile, nb):
    b, l, _ = proj3.shape
    has_state = state is not None
    cb = COL_BLOCK
    in_specs = [
        pl.BlockSpec((nb, tile, cb), lambda i, t: (i, t, GLUA_BLK)),
        pl.BlockSpec((nb, tile, cb), lambda i, t: (i, t, GLUB_BLK)),
        pl.BlockSpec((nb, tile, cb), lambda i, t: (i, t, CGATE_BLK)),
    ]
    args = [proj3, proj3, proj3]
    if has_state:
        in_specs.append(pl.BlockSpec((nb, CONV_K - 1, C_CONV), lambda i, t: (i, 0, 0)))
        args.append(state)
    in_specs += [pl.BlockSpec((CONV_K, C_CONV), lambda i, t: (0, 0)),
                 pl.BlockSpec((8, C_CONV), lambda i, t: (0, 0))]
    args += [conv_w, conv_p]
    return pl.pallas_call(
        functools.partial(_conv_kernel, tile=tile, nb=nb, has_state=has_state),
        grid=(b // nb, l // tile),
        in_specs=in_specs,
        out_specs=[pl.BlockSpec((nb, tile, C_CONV), lambda i, t: (i, t, 0)),
                   pl.BlockSpec((nb, CONV_K - 1, C_CONV), lambda i, t: (i, 0, 0))],
        out_shape=[jax.ShapeDtypeStruct((b, l, C_CONV), BF16),
                   jax.ShapeDtypeStruct((b, CONV_K - 1, C_CONV), F32)],
        scratch_shapes=[pltpu.VMEM((nb, SUBLANES, tile + CONV_HALO, C_CONV), F32)],
        compiler_params=_cparams(("arbitrary", "arbitrary")),
        name="conv_group",
    )(*args)


def _bdot(a, b, dims=None):
    a = a.astype(BF16)
    b = b.astype(BF16)
    if dims is None:
        return jnp.dot(a, b, preferred_element_type=F32)
    return lax.dot_general(a, b, (dims, ((), ())), preferred_element_type=F32)


_NT = ((1,), (1,))
_TN = ((0,), (0,))


def _split_bf16(a):
    hi = a.astype(BF16)
    return hi, (a - hi.astype(F32)).astype(BF16)


def _unit_lower_inverses(ms, c):
    row = lax.broadcasted_iota(jnp.int32, (c, c), 0)
    col = lax.broadcasted_iota(jnp.int32, (c, c), 1)
    eye = (row == col).astype(F32)
    ps = [-m for m in ms]
    ts = [eye + p for p in ps]
    span = 2
    while span < c:
        ps = [_bdot(p, p) for p in ps]
        ts = [t + _bdot(t, p) for t, p in zip(ts, ps)]
        span *= 2
    msp = [_split_bf16(m) for m in ms]
    tsp = [_split_bf16(t) for t in ts]
    mts = [_bdot(mh, th) + _bdot(mh, tl) + _bdot(ml, th) for (mh, ml), (th, tl) in zip(msp, tsp)]
    rs = [eye - t - mt for t, mt in zip(ts, mts)]
    return [t + _bdot(th, r) for t, (th, _), r in zip(ts, tsp, rs)]


def _dn_kernel(*refs, tile, chunk, nb, has_state):
    if has_state:
        (qkv_ref, gate_ref, bdc_ref, bdr_ref, prev_ref, s0_ref, w_ref, pr_ref, pc_ref, nw_ref,
         o_ref, st_ref, s_ref, xp_ref, xs_ref, osc_ref) = refs
    else:
        (qkv_ref, gate_ref, bdc_ref, bdr_ref, w_ref, pr_ref, pc_ref, nw_ref,
         o_ref, st_ref, s_ref, xp_ref, xs_ref, osc_ref) = refs
        prev_ref = s0_ref = None
    c = chunk
    nchunk = tile // c
    hd = DN_HEAD_DIM
    t = pl.program_id(1)
    last = pl.num_programs(1) - 1
    lo = QKV_HALO - (DN_CONV_K - 1)
    hi = lax.Precision.HIGHEST
    w = w_ref[...]

    for i in range(nb):
        @pl.when(t == 0)
        def _():
            xp_ref[i, 0:QKV_HALO, :] = jnp.zeros((QKV_HALO, 3 * DN_WIDTH), F32)
            if has_state:
                xp_ref[i, lo:QKV_HALO, :] = prev_ref[i]
                s_ref[i] = s0_ref[i]
            else:
                s_ref[i] = jnp.zeros((DN_HEADS, hd, hd), F32)

        xp_ref[i, QKV_HALO:QKV_HALO + tile, :] = qkv_ref[i]
        xp = xp_ref[i]
        for j in range(DN_CONV_K - 1):
            xs_ref[i, j] = pltpu.roll(xp, tile + QKV_HALO - (lo + j), axis=0)

        @pl.when(t == last)
        def _():
            st_ref[i] = xp_ref[i, tile + lo:tile + QKV_HALO, :]

    neg_a_row = -jnp.exp(pr_ref[0:1, :])
    dtb_row = pr_ref[1:2, :]
    neg_a_col = -jnp.exp(pc_ref[:, 0:1])
    dtb_col = pc_ref[:, 1:2]
    norm_w = nw_ref[...]

    row = lax.broadcasted_iota(jnp.int32, (c, c), 0)
    col = lax.broadcasted_iota(jnp.int32, (c, c), 1)
    incl = row >= col
    strict = row > col
    tri_l = incl.astype(F32)
    tri_u = (row <= col).astype(F32)

    seq_chunks = [(i, n) for i in range(nb) for n in range(nchunk)]
    probs = [(i, n, h) for i, n in seq_chunks for h in range(DN_HEADS)]
    beta, gcum_col, gcum_row, qkvc = {}, {}, {}, {}
    for i, n in seq_chunks:
        rows = slice(n * c, (n + 1) * c)
        bd = bdc_ref[i, rows, :]
        beta[i, n] = _sigmoid(bd[:, 0:DN_HEADS])
        g_col = neg_a_row * _softplus(bd[:, DN_HEADS:2 * DN_HEADS] + dtb_row)
        gcum_col[i, n] = jnp.dot(tri_l, g_col, precision=hi, preferred_element_type=F32)
        br = bdr_ref[i, n]
        g_row = neg_a_col * _softplus(br[DN_HEADS:2 * DN_HEADS, :] + dtb_col)
        gcum_row[i, n] = jnp.dot(g_row, tri_u, precision=hi, preferred_element_type=F32)
        acc = xp_ref[i, QKV_HALO + n * c:QKV_HALO + (n + 1) * c, :] * w[DN_CONV_K - 1:DN_CONV_K, :]
        for j in range(DN_CONV_K - 1):
            acc = acc + xs_ref[i, j, rows, :] * w[j:j + 1, :]
        qkvc[i, n] = _silu(acc)

    def head_lanes(i, n, h, part):
        return qkvc[i, n][:, part * DN_WIDTH + h * hd:part * DN_WIDTH + (h + 1) * hd]

    qs = [head_lanes(i, n, h, 0) for i, n, h in probs]
    ks = [head_lanes(i, n, h, 1) for i, n, h in probs]
    vs = [head_lanes(i, n, h, 2) for i, n, h in probs]
    qs = [q * (lax.rsqrt(jnp.sum(q * q, axis=-1, keepdims=True) + NORM_EPS) * (hd ** -0.5)) for q in qs]
    ks = [k * lax.rsqrt(jnp.sum(k * k, axis=-1, keepdims=True) + NORM_EPS) for k in ks]
    gcs = [gcum_col[i, n][:, h:h + 1] for i, n, h in probs]
    grs = [gcum_row[i, n][h:h + 1, :] for i, n, h in probs]
    bts = [beta[i, n][:, h:h + 1] for i, n, h in probs]
    gls = [gcum_col[i, n][c - 1:c, h:h + 1] for i, n, h in probs]
    decays = [jnp.exp(jnp.where(incl, gc - gr, -jnp.inf)) for gc, gr in zip(gcs, grs)]
    egcs = [jnp.exp(gc) for gc in gcs]
    kbs = [k * bt for k, bt in zip(ks, bts)]
    kqs = [_bdot(jnp.concatenate([kb, q], axis=0), k, _NT) for kb, q, k in zip(kbs, qs, ks)]
    aqks = [(kq[c:] * dc).astype(BF16) for kq, dc in zip(kqs, decays)]
    ms = [jnp.where(strict, kq[:c] * dc, 0.0) for kq, dc in zip(kqs, decays)]
    tinvs = _unit_lower_inverses(ms, c)
    uws = [_bdot(tinv, jnp.concatenate([v * bt, kb * e], axis=1))
           for tinv, v, bt, kb, e in zip(tinvs, vs, bts, kbs, egcs)]
    wqs = [jnp.concatenate([uw[:, hd:], q * e], axis=0).astype(BF16) for uw, q, e in zip(uws, qs, egcs)]
    kds = [(k * jnp.exp(gl - gc)).astype(BF16) for k, gl, gc in zip(ks, gls, gcs)]
    egls = [jnp.exp(gl) for gl in gls]
    local = dict(zip(probs, zip(uws, wqs, aqks, kds, egls)))

    heads = [(i, h) for i in range(nb) for h in range(DN_HEADS)]
    state = {(i, h): s_ref[i, h] for i, h in heads}
    for n in range(nchunk):
        rows = slice(n * c, (n + 1) * c)
        items = [(i, h) + local[i, n, h] for i, h in heads]
        s16s = [state[i, h].astype(BF16) for i, h in heads]
        wss = [_bdot(it[3], s16) for it, s16 in zip(items, s16s)]
        u16s = [(it[2][:, :hd] - ws[:c]).astype(BF16) for it, ws in zip(items, wss)]
        outs = [ws[c:] + _bdot(it[4], u16) for it, ws, u16 in zip(items, wss, u16s)]
        for it, u16 in zip(items, u16s):
            i, h = it[0], it[1]
            state[i, h] = state[i, h] * it[6] + _bdot(it[5], u16, _TN)
        for it, o in zip(items, outs):
            i, h = it[0], it[1]
            o = o * lax.rsqrt(jnp.mean(o * o, axis=-1, keepdims=True) + NORM_EPS) * norm_w
            osc_ref[i, rows, h * hd:(h + 1) * hd] = o
    for i, h in heads:
        s_ref[i, h] = state[i, h]

    for i in range(nb):
        xp_ref[i, 0:QKV_HALO, :] = xp_ref[i, tile:tile + QKV_HALO, :]
        o_ref[i] = (osc_ref[i] * _silu(gate_ref[i])).astype(o_ref.dtype)


def _dn_group(proj3, bdc3, bdr4, prev, s0, qkv_w, p_row, p_col, norm_w, tile, chunk, nb):
    b, l, _ = proj3.shape
    has_state = s0 is not None
    nchunk = tile // chunk
    hd = DN_HEAD_DIM
    in_specs = [
        pl.BlockSpec((nb, tile, 3 * DN_WIDTH), lambda i, t: (i, t, QKV_BLK)),
        pl.BlockSpec((nb, tile, DN_WIDTH), lambda i, t: (i, t, DNGATE_BLK)),
        pl.BlockSpec((nb, tile, SMALL_PAD), lambda i, t: (i, t, 0)),
        pl.BlockSpec((nb, nchunk, 2 * DN_HEADS, chunk), lambda i, t: (i, t, 0, 0)),
    ]
    args = [proj3, proj3, bdc3, bdr4]
    if has_state:
        in_specs += [pl.BlockSpec((nb, DN_CONV_K - 1, 3 * DN_WIDTH), lambda i, t: (i, 0, 0)),
                     pl.BlockSpec((nb, DN_HEADS, hd, hd), lambda i, t: (i, 0, 0, 0))]
        args += [prev, s0]
    in_specs += [pl.BlockSpec((DN_CONV_K, 3 * DN_WIDTH), lambda i, t: (0, 0)),
                 pl.BlockSpec((2, DN_HEADS), lambda i, t: (0, 0)),
                 pl.BlockSpec((DN_HEADS, 2), lambda i, t: (0, 0)),
                 pl.BlockSpec((1, hd), lambda i, t: (0, 0))]
    args += [qkv_w, p_row, p_col, norm_w]
    return pl.pallas_call(
        functools.partial(_dn_kernel, tile=tile, chunk=chunk, nb=nb, has_state=has_state),
        grid=(b // nb, l // tile),
        in_specs=in_specs,
        out_specs=[pl.BlockSpec((nb, tile, DN_WIDTH), lambda i, t: (i, t, 0)),
                   pl.BlockSpec((nb, DN_CONV_K - 1, 3 * DN_WIDTH), lambda i, t: (i, 0, 0)),
                   pl.BlockSpec((nb, DN_HEADS, hd, hd), lambda i, t: (i, 0, 0, 0))],
        out_shape=[jax.ShapeDtypeStruct((b, l, DN_WIDTH), BF16),
                   jax.ShapeDtypeStruct((b, DN_CONV_K - 1, 3 * DN_WIDTH), F32),
                   jax.ShapeDtypeStruct((b, DN_HEADS, hd, hd), F32)],
        scratch_shapes=[pltpu.VMEM((nb, tile + QKV_HALO, 3 * DN_WIDTH), F32),
                        pltpu.VMEM((nb, DN_CONV_K - 1, tile + QKV_HALO, 3 * DN_WIDTH), F32),
                        pltpu.VMEM((nb, tile, DN_WIDTH), F32)],
        compiler_params=_cparams(("arbitrary", "arbitrary")),
        name="deltanet_group",
    )(*args)


def _mem_kernel(q_ref, g_ref, k_ref, v_ref, o_ref, *, nb):
    hd = MEM_HEAD_DIM
    scale = hd ** -0.5
    probs = [(i, h) for i in range(nb) for h in range(MEM_HEADS)]
    lanes = [slice(h * hd, (h + 1) * hd) for h in range(MEM_HEADS)]
    qs = [q_ref[i, :, lanes[h]] for i, h in probs]
    ks = [k_ref[i, pl.ds(h, MEM_TOKENS, stride=MEM_HEADS), :] for i, h in probs]
    vs = [v_ref[i, pl.ds(h, MEM_TOKENS, stride=MEM_HEADS), :] for i, h in probs]
    ss = [_bdot(q, k, _NT) * scale for q, k in zip(qs, ks)]
    es = [jnp.exp(s - jnp.max(s, axis=-1, keepdims=True)) for s in ss]
    oms = [_bdot(e, v) / jnp.sum(e, axis=-1, keepdims=True) for e, v in zip(es, vs)]
    for (i, h), om in zip(probs, oms):
        o_ref[i, :, lanes[h]] = (om * _silu(g_ref[i, :, lanes[h]])).astype(o_ref.dtype)


def _mem_group(proj3, mem_k, mem_v, tile, nb):
    b, l, _ = proj3.shape
    cb = COL_BLOCK
    rows = MEM_TOKENS * MEM_HEADS
    return pl.pallas_call(
        functools.partial(_mem_kernel, nb=nb),
        grid=(b // nb, l // tile),
        in_specs=[pl.BlockSpec((nb, tile, cb), lambda i, t: (i, t, MEMQ_BLK)),
                  pl.BlockSpec((nb, tile, cb), lambda i, t: (i, t, MEMG_BLK)),
                  pl.BlockSpec((nb, rows, MEM_HEAD_DIM), lambda i, t: (i, 0, 0)),
                  pl.BlockSpec((nb, rows, MEM_HEAD_DIM), lambda i, t: (i, 0, 0))],
        out_specs=pl.BlockSpec((nb, tile, MEM_WIDTH), lambda i, t: (i, t, 0)),
        out_shape=jax.ShapeDtypeStruct((b, l, MEM_WIDTH), BF16),
        compiler_params=_cparams(("arbitrary", "arbitrary")),
        name="memory_group",
    )(proj3, proj3, mem_k, mem_v)


def _out_kernel(oc_ref, od_ref, om_ref, x_ref, w_ref, p_ref, y_ref):
    h = jnp.dot(oc_ref[...], w_ref[0:C_CONV, :], preferred_element_type=F32)
    h = h + jnp.dot(od_ref[...], w_ref[C_CONV:C_CONV + DN_WIDTH, :], preferred_element_type=F32)
    h = h + jnp.dot(om_ref[...], w_ref[C_CONV + DN_WIDTH:, :], preferred_element_type=F32)
    z = DEEPNORM_ALPHA * x_ref[...] + h
    mu = jnp.mean(z, axis=-1, keepdims=True)
    zc = z - mu
    var = jnp.mean(zc * zc, axis=-1, keepdims=True)
    y_ref[...] = zc * lax.rsqrt(var + LN_EPS) * p_ref[0:1, :] + p_ref[1:2, :]


def _out_proj(oc, od, om, x2, w_out, ln_p, bm):
    m = x2.shape[0]
    return pl.pallas_call(
        _out_kernel,
        grid=(m // bm,),
        in_specs=[pl.BlockSpec((bm, C_CONV), lambda i: (i, 0)),
                  pl.BlockSpec((bm, DN_WIDTH), lambda i: (i, 0)),
                  pl.BlockSpec((bm, MEM_WIDTH), lambda i: (i, 0)),
                  pl.BlockSpec((bm, D_MODEL), lambda i: (i, 0)),
                  pl.BlockSpec((D_MODEL, D_MODEL), lambda i: (0, 0)),
                  pl.BlockSpec((8, D_MODEL), lambda i: (0, 0))],
        out_specs=pl.BlockSpec((bm, D_MODEL), lambda i: (i, 0)),
        out_shape=jax.ShapeDtypeStruct((m, D_MODEL), F32),
        compiler_params=_cparams(("arbitrary",)),
        name="out_proj_ln",
    )(oc, od, om, x2, w_out, ln_p)


def _pad_rows(rows, n):
    a = jnp.stack(rows).astype(F32)
    return jnp.pad(a, ((0, n - a.shape[0]), (0, 0)))


def _layer(x, conv_prev, qkv_prev, s0, mem_k, mem_v, wts, *, bm, conv_tile, conv_nb, dn_tile, dn_chunk, dn_nb,
           mem_tile, mem_nb, out_bm):
    (w_in_t, layer, conv_w, conv_p, qkv_w, p_row, p_col, norm_w, w_out, ln_p) = wts
    b, l, _ = x.shape
    x2 = x.reshape(b * l, D_MODEL)
    proj, bdc, bdr = _in_proj(x2, w_in_t, layer, bm)
    proj3 = proj.reshape(b, l, N_MAIN)
    bdc3 = bdc.reshape(b, l, SMALL_PAD)
    bdr4 = bdr.reshape(2 * DN_HEADS, b, l // dn_chunk, dn_chunk).transpose(1, 2, 0, 3)

    oc, conv_state = _conv_group(proj3, conv_prev, conv_w, conv_p, conv_tile, conv_nb)
    od, qkv_state, s_final = _dn_group(proj3, bdc3, bdr4, qkv_prev, s0, qkv_w, p_row, p_col, norm_w,
                                       dn_tile, dn_chunk, dn_nb)
    om = _mem_group(proj3, mem_k, mem_v, mem_tile, mem_nb)
    y = _out_proj(oc.reshape(b * l, C_CONV), od.reshape(b * l, DN_WIDTH), om.reshape(b * l, MEM_WIDTH),
                  x2, w_out, ln_p, out_bm)
    return y.reshape(b, l, D_MODEL), conv_state, qkv_state, s_final


def kernel(x_prompt, x_sample, mem_prompt, state_conv, state_qkv_conv, state_delta, cache_mem_k, cache_mem_v,
           w_in, conv_w, conv_b, conv_ln_g, conv_ln_b, qkv_conv_w, a_log, dt_bias, delta_norm_w,
           w_mem_k, w_mem_v, w_out, ln_g, ln_b):
    bp = x_prompt.shape[0]
    bs = x_sample.shape[0]
    hp, hs = x_prompt, x_sample
    outs = [[] for _ in range(8)]
    w_in_t = jnp.swapaxes(w_in, 1, 2)
    for i in range(DEPTH):
        conv_p = _pad_rows([conv_b[i], conv_ln_g[i], conv_ln_b[i]], 8)
        p_row = jnp.stack([a_log[i], dt_bias[i]]).astype(F32)
        wts = (w_in_t, i, conv_w[i], conv_p, qkv_conv_w[i], p_row, p_row.T,
               delta_norm_w[i].reshape(1, DN_HEAD_DIM), w_out[i].astype(BF16),
               _pad_rows([ln_g[i], ln_b[i]], 8))

        w_kv = jnp.concatenate([w_mem_k[i], w_mem_v[i]], axis=1).astype(BF16)
        kv = _matmul(mem_prompt.reshape(bp * MEM_TOKENS, D_MODEL), w_kv, 512, 512)
        mk = kv[:, :MEM_WIDTH].reshape(bp, MEM_TOKENS, MEM_HEADS, MEM_HEAD_DIM)
        mv = kv[:, MEM_WIDTH:].reshape(bp, MEM_TOKENS, MEM_HEADS, MEM_HEAD_DIM)
        kv_rows = MEM_TOKENS * MEM_HEADS
        hp, c_st, q_st, s_st = _layer(
            hp, None, None, None, mk.reshape(bp, kv_rows, MEM_HEAD_DIM), mv.reshape(bp, kv_rows, MEM_HEAD_DIM),
            wts, bm=1024, conv_tile=512, conv_nb=1, dn_tile=256, dn_chunk=64, dn_nb=1, mem_tile=512, mem_nb=1,
            out_bm=512)
        outs[0].append(c_st); outs[1].append(q_st); outs[2].append(s_st)
        outs[3].append(mk)
        outs[4].append(mv)

        ls = hs.shape[1]
        hs, c_st, q_st, s_st = _layer(
            hs, state_conv[i], state_qkv_conv[i], state_delta[i],
            cache_mem_k[i].reshape(bs, kv_rows, MEM_HEAD_DIM), cache_mem_v[i].reshape(bs, kv_rows, MEM_HEAD_DIM),
            wts, bm=bs * ls, conv_tile=ls, conv_nb=8, dn_tile=ls, dn_chunk=ls, dn_nb=4, mem_tile=ls, mem_nb=8,
            out_bm=512)
        outs[5].append(c_st); outs[6].append(q_st); outs[7].append(s_st)
    return (hp, hs) + tuple(jnp.stack(o) for o in outs)
```

```python
import functools

import jax
import jax.numpy as jnp
from jax import lax
from jax.experimental import pallas as pl
from jax.experimental.pallas import tpu as pltpu

F32 = jnp.float32
BF16 = jnp.bfloat16

D_MODEL = 2048
C_CONV = 512
CONV_K = 31
DN_HEADS = 8
DN_HEAD_DIM = 128
DN_WIDTH = DN_HEADS * DN_HEAD_DIM
DN_CONV_K = 4
MEM_HEADS = 4
MEM_HEAD_DIM = 128
MEM_WIDTH = MEM_HEADS * MEM_HEAD_DIM
MEM_TOKENS = 256
DEPTH = 1
DEEPNORM_ALPHA = (2 * DEPTH) ** 0.25
LN_EPS = 1e-5
NORM_EPS = 1e-6

N_MAIN = 3 * DN_WIDTH + DN_WIDTH + 3 * C_CONV + 2 * MEM_WIDTH
COL_BLOCK = 512
QKV_BLK = 0
DNGATE_BLK = 3
GLUA_BLK, GLUB_BLK, CGATE_BLK, MEMQ_BLK, MEMG_BLK = 8, 9, 10, 11, 12
SMALL_PAD = 128
N_DIRECT_BLOCKS = (3 * C_CONV + 4 * DN_WIDTH) // COL_BLOCK
CONV_HALO = 32
QKV_HALO = 8
SUBLANES = 8
CONV_ROW_BLOCK = 64

VMEM_LIMIT = 56 * 1024 * 1024


def _sigmoid(x):
    return 1.0 / (1.0 + jnp.exp(-x))


def _silu(x):
    return x * _sigmoid(x)


def _softplus(x):
    return jnp.maximum(x, 0.0) + jnp.log(1.0 + jnp.exp(-jnp.abs(x)))


def _cparams(sem):
    return pltpu.CompilerParams(dimension_semantics=sem, vmem_limit_bytes=VMEM_LIMIT)


def _proj_kernel(x_ref, wa_ref, wb_ref, o_ref, bdc_ref, bdr_ref, xb_ref):
    j = pl.program_id(1)
    nt = (((1,), (1,)), ((), ()))
    n_bd = 2 * DN_HEADS

    @pl.when(j == 0)
    def _():
        xb_ref[...] = x_ref[...].astype(BF16)

    @pl.when(j < N_DIRECT_BLOCKS)
    def _():
        o_ref[...] = lax.dot_general(xb_ref[...], wa_ref[...].astype(BF16), nt, preferred_element_type=F32)

    @pl.when(j >= N_DIRECT_BLOCKS)
    def _():
        w = jnp.concatenate([wa_ref[n_bd:, :], wb_ref[...]], axis=0).astype(BF16)
        o_ref[...] = lax.dot_general(xb_ref[...], w, nt, preferred_element_type=F32)

    @pl.when(j == N_DIRECT_BLOCKS)
    def _():
        bd = lax.dot_general(xb_ref[...], wa_ref[0:SMALL_PAD, :].astype(BF16), nt, preferred_element_type=F32)
        bdc_ref[...] = bd
        bdr_ref[...] = bd.T[0:n_bd, :]


def _proj_out_block(j):
    n_conv = 3 * C_CONV // COL_BLOCK
    return jnp.where(j < n_conv, j + GLUA_BLK, jnp.where(j < N_DIRECT_BLOCKS, j - n_conv, j))


def _in_proj(x2, w_in_t, layer, bm):
    m = x2.shape[0]
    bn = COL_BLOCK
    n_blocks = N_MAIN // bn
    n_bd = 2 * DN_HEADS
    x_mode = pl.Buffered(1) if m > bm else None
    return pl.pallas_call(
        _proj_kernel,
        grid=(m // bm, n_blocks),
        in_specs=[
            pl.BlockSpec((bm, D_MODEL), lambda i, j: (i, 0), pipeline_mode=x_mode),
            pl.BlockSpec((None, bn, D_MODEL), lambda i, j: (layer, j, 0)),
            pl.BlockSpec((None, n_bd, D_MODEL),
                         lambda i, j: (layer, (jnp.maximum(j + 1, n_blocks - 1)) * (bn // n_bd), 0)),
        ],
        out_specs=[
            pl.BlockSpec((bm, bn), lambda i, j: (i, _proj_out_block(j))),
            pl.BlockSpec((bm, SMALL_PAD), lambda i, j: (i, 0)),
            pl.BlockSpec((2 * DN_HEADS, bm), lambda i, j: (0, i)),
        ],
        out_shape=[
            jax.ShapeDtypeStruct((m, N_MAIN), F32),
            jax.ShapeDtypeStruct((m, SMALL_PAD), F32),
            jax.ShapeDtypeStruct((2 * DN_HEADS, m), F32),
        ],
        scratch_shapes=[pltpu.VMEM((bm, D_MODEL), BF16)],
        compiler_params=_cparams(("arbitrary", "arbitrary")),
        name="in_proj",
    )(x2, w_in_t, w_in_t)


def _mm_kernel(x_ref, w_ref, o_ref):
    o_ref[...] = jnp.dot(x_ref[...].astype(BF16), w_ref[...], preferred_element_type=F32)


def _matmul(x2, w, bm, bn):
    m, k = x2.shape
    n = w.shape[1]
    return pl.pallas_call(
        _mm_kernel,
        grid=(m // bm, n // bn),
        in_specs=[pl.BlockSpec((bm, k), lambda i, j: (i, 0)),
                  pl.BlockSpec((k, bn), lambda i, j: (0, j))],
        out_specs=pl.BlockSpec((bm, bn), lambda i, j: (i, j)),
        out_shape=jax.ShapeDtypeStruct((m, n), F32),
        compiler_params=_cparams(("arbitrary", "arbitrary")),
        name="mem_kv_proj",
    )(x2, w)


def _conv_kernel(*refs, tile, nb, has_state):
    if has_state:
        (a_ref, b_ref, g_ref, st_ref, w_ref, p_ref, o_ref, so_ref, xp_ref) = refs
    else:
        (a_ref, b_ref, g_ref, w_ref, p_ref, o_ref, so_ref, xp_ref) = refs
        st_ref = None
    t = pl.program_id(1)
    last = pl.num_programs(1) - 1
    lo = CONV_HALO - (CONV_K - 1)
    n = tile + CONV_HALO
    rb = min(CONV_ROW_BLOCK, tile)
    w = w_ref[...]
    conv_b = p_ref[0:1, :]
    ln_g = p_ref[1:2, :]
    ln_b = p_ref[2:3, :]
    for i in range(nb):
        @pl.when(t == 0)
        def _():
            xp_ref[i, 0, 0:CONV_HALO, :] = jnp.zeros((CONV_HALO, C_CONV), F32)
            if has_state:
                xp_ref[i, 0, lo:CONV_HALO, :] = st_ref[i]

        xp_ref[i, 0, CONV_HALO:n, :] = a_ref[i] * _sigmoid(b_ref[i])
        xp = xp_ref[i, 0]
        for s in range(1, SUBLANES):
            xp_ref[i, s] = pltpu.roll(xp, n - s, axis=0)

        for r in range(tile // rb):
            acc = None
            for j in range(CONV_K):
                s = (lo + j) % SUBLANES
                start = lo + j - s + r * rb
                term = xp_ref[i, s, start:start + rb, :] * w[j:j + 1, :]
                acc = term if acc is None else acc + term
            hc = acc + conv_b
            mu = jnp.mean(hc, axis=-1, keepdims=True)
            xc = hc - mu
            var = jnp.mean(xc * xc, axis=-1, keepdims=True)
            hn = xc * lax.rsqrt(var + LN_EPS) * ln_g + ln_b
            rows = slice(r * rb, (r + 1) * rb)
            o_ref[i, rows, :] = (_silu(hn) * _silu(g_ref[i, rows, :])).astype(o_ref.dtype)

        @pl.when(t == last)
        def _():
            so_ref[i] = xp_ref[i, 0, tile + lo:n, :]

        xp_ref[i, 0, 0:CONV_HALO, :] = xp_ref[i, 0, tile:n, :]


def _conv_group(proj3, state, conv_w, conv_p, tile, nb):
    b, l, _ = proj3.shape
    has_state = state is not None
    cb = COL_BLOCK
    in_specs = [
        pl.BlockSpec((nb, tile, cb), lambda i, t: (i, t, GLUA_BLK)),
        pl.BlockSpec((nb, tile, cb), lambda i, t: (i, t, GLUB_BLK)),
        pl.BlockSpec((nb, tile, cb), lambda i, t: (i, t, CGATE_BLK)),
    ]
    args = [proj3, proj3, proj3]
    if has_state:
        in_specs.append(pl.BlockSpec((nb, CONV_K - 1, C_CONV), lambda i, t: (i, 0, 0)))
        args.append(state)
    in_specs += [pl.BlockSpec((CONV_K, C_CONV), lambda i, t: (0, 0)),
                 pl.BlockSpec((8, C_CONV), lambda i, t: (0, 0))]
    args += [conv_w, conv_p]
    return pl.pallas_call(
        functools.partial(_conv_kernel, tile=tile, nb=nb, has_state=has_state),
        grid=(b // nb, l // tile),
        in_specs=in_specs,
        out_specs=[pl.BlockSpec((nb, tile, C_CONV), lambda i, t: (i, t, 0)),
                   pl.BlockSpec((nb, CONV_K - 1, C_CONV), lambda i, t: (i, 0, 0))],
        out_shape=[jax.ShapeDtypeStruct((b, l, C_CONV), BF16),
                   jax.ShapeDtypeStruct((b, CONV_K - 1, C_CONV), F32)],
        scratch_shapes=[pltpu.VMEM((nb, SUBLANES, tile + CONV_HALO, C_CONV), F32)],
        compiler_params=_cparams(("arbitrary", "arbitrary")),
        name="conv_group",
    )(*args)


def _bdot(a, b, dims=None):
    a = a.astype(BF16)
    b = b.astype(BF16)
    if dims is None:
        return jnp.dot(a, b, preferred_element_type=F32)
    return lax.dot_general(a, b, (dims, ((), ())), preferred_element_type=F32)


_NT = ((1,), (1,))
_TN = ((0,), (0,))


def _split_bf16(a):
    hi = a.astype(BF16)
    return hi, (a - hi.astype(F32)).astype(BF16)


def _unit_lower_inverses(ms, c):
    row = lax.broadcasted_iota(jnp.int32, (c, c), 0)
    col = lax.broadcasted_iota(jnp.int32, (c, c), 1)
    eye = (row == col).astype(F32)
    ps = [-m for m in ms]
    ts = [eye + p for p in ps]
    span = 2
    while span < c:
        ps = [_bdot(p, p) for p in ps]
        ts = [t + _bdot(t, p) for t, p in zip(ts, ps)]
        span *= 2
    msp = [_split_bf16(m) for m in ms]
    tsp = [_split_bf16(t) for t in ts]
    mts = [_bdot(mh, th) + _bdot(mh, tl) + _bdot(ml, th) for (mh, ml), (th, tl) in zip(msp, tsp)]
    rs = [eye - t - mt for t, mt in zip(ts, mts)]
    return [t + _bdot(th, r) for t, (th, _), r in zip(ts, tsp, rs)]


def _dn_kernel(*refs, tile, chunk, nb, has_state):
    if has_state:
        (qkv_ref, gate_ref, bdc_ref, bdr_ref, prev_ref, s0_ref, w_ref, pr_ref, pc_ref, nw_ref,
         o_ref, st_ref, s_ref, xp_ref, xs_ref, osc_ref) = refs
    else:
        (qkv_ref, gate_ref, bdc_ref, bdr_ref, w_ref, pr_ref, pc_ref, nw_ref,
         o_ref, st_ref, s_ref, xp_ref, xs_ref, osc_ref) = refs
        prev_ref = s0_ref = None
    c = chunk
    nchunk = tile // c
    hd = DN_HEAD_DIM
    t = pl.program_id(1)
    last = pl.num_programs(1) - 1
    lo = QKV_HALO - (DN_CONV_K - 1)
    hi = lax.Precision.HIGHEST
    w = w_ref[...]

    for i in range(nb):
        @pl.when(t == 0)
        def _():
            xp_ref[i, 0:QKV_HALO, :] = jnp.zeros((QKV_HALO, 3 * DN_WIDTH), F32)
            if has_state:
                xp_ref[i, lo:QKV_HALO, :] = prev_ref[i]
                s_ref[i] = s0_ref[i]
            else:
                s_ref[i] = jnp.zeros((DN_HEADS, hd, hd), F32)

        xp_ref[i, QKV_HALO:QKV_HALO + tile, :] = qkv_ref[i]
        xp = xp_ref[i]
        for j in range(DN_CONV_K - 1):
            xs_ref[i, j] = pltpu.roll(xp, tile + QKV_HALO - (lo + j), axis=0)

        @pl.when(t == last)
        def _():
            st_ref[i] = xp_ref[i, tile + lo:tile + QKV_HALO, :]

    neg_a_row = -jnp.exp(pr_ref[0:1, :])
    dtb_row = pr_ref[1:2, :]
    neg_a_col = -jnp.exp(pc_ref[:, 0:1])
    dtb_col = pc_ref[:, 1:2]
    norm_w = nw_ref[...]

    row = lax.broadcasted_iota(jnp.int32, (c, c), 0)
    col = lax.broadcasted_iota(jnp.int32, (c, c), 1)
    incl = row >= col
    strict = row > col
    tri_l = incl.astype(F32)
    tri_u = (row <= col).astype(F32)

    seq_chunks = [(i, n) for i in range(nb) for n in range(nchunk)]
    probs = [(i, n, h) for i, n in seq_chunks for h in range(DN_HEADS)]
    beta, gcum_col, gcum_row, prep = {}, {}, {}, {}
    for i, n in seq_chunks:
        rows = slice(n * c, (n + 1) * c)
        bd = bdc_ref[i, rows, :]
        beta[i, n] = _sigmoid(bd[:, 0:DN_HEADS])
        g_col = neg_a_row * _softplus(bd[:, DN_HEADS:2 * DN_HEADS] + dtb_row)
        gcum_col[i, n] = jnp.dot(tri_l, g_col, precision=hi, preferred_element_type=F32)
        br = bdr_ref[i, n]
        g_row = neg_a_col * _softplus(br[DN_HEADS:2 * DN_HEADS, :] + dtb_col)
        gcum_row[i, n] = jnp.dot(g_row, tri_u, precision=hi, preferred_element_type=F32)
        acc = xp_ref[i, QKV_HALO + n * c:QKV_HALO + (n + 1) * c, :] * w[DN_CONV_K - 1:DN_CONV_K, :]
        for j in range(DN_CONV_K - 1):
            acc = acc + xs_ref[i, j, rows, :] * w[j:j + 1, :]
        qkvc = _silu(acc)
        for h in range(DN_HEADS):
            q, k, v = (qkvc[:, part * DN_WIDTH + h * hd:part * DN_WIDTH + (h + 1) * hd] for part in range(3))
            q = q * (lax.rsqrt(jnp.sum(q * q, axis=-1, keepdims=True) + NORM_EPS) * (hd ** -0.5))
            k = k * lax.rsqrt(jnp.sum(k * k, axis=-1, keepdims=True) + NORM_EPS)
            gc = gcum_col[i, n][:, h:h + 1]
            gr = gcum_row[i, n][h:h + 1, :]
            bt = beta[i, n][:, h:h + 1]
            gl = gcum_col[i, n][c - 1:c, h:h + 1]
            e = jnp.exp(gc)
            kb = k * bt
            prep[i, n, h] = (
                jnp.concatenate([kb, q], axis=0).astype(BF16),
                k.astype(BF16),
                jnp.exp(jnp.where(incl, gc - gr, -jnp.inf)),
                jnp.concatenate([v * bt, kb * e], axis=1).astype(BF16),
                q * e,
                (k * jnp.exp(gl - gc)).astype(BF16),
                jnp.exp(gl))

    preps = [prep[p] for p in probs]
    kqs = [_bdot(p[0], p[1], _NT) for p in preps]
    aqks = [(kq[c:] * p[2]).astype(BF16) for kq, p in zip(kqs, preps)]
    ms = [jnp.where(strict, kq[:c] * p[2], 0.0) for kq, p in zip(kqs, preps)]
    tinvs = _unit_lower_inverses(ms, c)
    uws = [_bdot(tinv, p[3]) for tinv, p in zip(tinvs, preps)]
    wqs = [jnp.concatenate([uw[:, hd:], p[4]], axis=0).astype(BF16) for uw, p in zip(uws, preps)]
    local = dict(zip(probs, zip(uws, wqs, aqks, [p[5] for p in preps], [p[6] for p in preps])))

    heads = [(i, h) for i in range(nb) for h in range(DN_HEADS)]
    state = {(i, h): s_ref[i, h] for i, h in heads}
    for n in range(nchunk):
        rows = slice(n * c, (n + 1) * c)
        items = [(i, h) + local[i, n, h] for i, h in heads]
        s16s = [state[i, h].astype(BF16) for i, h in heads]
        wss = [_bdot(it[3], s16) for it, s16 in zip(items, s16s)]
        u16s = [(it[2][:, :hd] - ws[:c]).astype(BF16) for it, ws in zip(items, wss)]
        outs = [ws[c:] + _bdot(it[4], u16) for it, ws, u16 in zip(items, wss, u16s)]
        for it, u16 in zip(items, u16s):
            i, h = it[0], it[1]
            state[i, h] = state[i, h] * it[6] + _bdot(it[5], u16, _TN)
        for it, o in zip(items, outs):
            i, h = it[0], it[1]
            o = o * lax.rsqrt(jnp.mean(o * o, axis=-1, keepdims=True) + NORM_EPS) * norm_w
            osc_ref[i, rows, h * hd:(h + 1) * hd] = o
    for i, h in heads:
        s_ref[i, h] = state[i, h]

    for i in range(nb):
        xp_ref[i, 0:QKV_HALO, :] = xp_ref[i, tile:tile + QKV_HALO, :]
        o_ref[i] = (osc_ref[i] * _silu(gate_ref[i])).astype(o_ref.dtype)


def _dn_group(proj3, bdc3, bdr4, prev, s0, qkv_w, p_row, p_col, norm_w, tile, chunk, nb):
    b, l, _ = proj3.shape
    has_state = s0 is not None
    nchunk = tile // chunk
    hd = DN_HEAD_DIM
    in_specs = [
        pl.BlockSpec((nb, tile, 3 * DN_WIDTH), lambda i, t: (i, t, QKV_BLK)),
        pl.BlockSpec((nb, tile, DN_WIDTH), lambda i, t: (i, t, DNGATE_BLK)),
        pl.BlockSpec((nb, tile, SMALL_PAD), lambda i, t: (i, t, 0)),
        pl.BlockSpec((nb, nchunk, 2 * DN_HEADS, chunk), lambda i, t: (i, t, 0, 0)),
    ]
    args = [proj3, proj3, bdc3, bdr4]
    if has_state:
        in_specs += [pl.BlockSpec((nb, DN_CONV_K - 1, 3 * DN_WIDTH), lambda i, t: (i, 0, 0)),
                     pl.BlockSpec((nb, DN_HEADS, hd, hd), lambda i, t: (i, 0, 0, 0))]
        args += [prev, s0]
    in_specs += [pl.BlockSpec((DN_CONV_K, 3 * DN_WIDTH), lambda i, t: (0, 0)),
                 pl.BlockSpec((2, DN_HEADS), lambda i, t: (0, 0)),
                 pl.BlockSpec((DN_HEADS, 2), lambda i, t: (0, 0)),
                 pl.BlockSpec((1, hd), lambda i, t: (0, 0))]
    args += [qkv_w, p_row, p_col, norm_w]
    return pl.pallas_call(
        functools.partial(_dn_kernel, tile=tile, chunk=chunk, nb=nb, has_state=has_state),
        grid=(b // nb, l // tile),
        in_specs=in_specs,
        out_specs=[pl.BlockSpec((nb, tile, DN_WIDTH), lambda i, t: (i, t, 0)),
                   pl.BlockSpec((nb, DN_CONV_K - 1, 3 * DN_WIDTH), lambda i, t: (i, 0, 0)),
                   pl.BlockSpec((nb, DN_HEADS, hd, hd), lambda i, t: (i, 0, 0, 0))],
        out_shape=[jax.ShapeDtypeStruct((b, l, DN_WIDTH), BF16),
                   jax.ShapeDtypeStruct((b, DN_CONV_K - 1, 3 * DN_WIDTH), F32),
                   jax.ShapeDtypeStruct((b, DN_HEADS, hd, hd), F32)],
        scratch_shapes=[pltpu.VMEM((nb, tile + QKV_HALO, 3 * DN_WIDTH), F32),
                        pltpu.VMEM((nb, DN_CONV_K - 1, tile + QKV_HALO, 3 * DN_WIDTH), F32),
                        pltpu.VMEM((nb, tile, DN_WIDTH), F32)],
        compiler_params=_cparams(("arbitrary", "arbitrary")),
        name="deltanet_group",
    )(*args)


def _mem_kernel(q_ref, g_ref, k_ref, v_ref, o_ref, *, nb):
    hd = MEM_HEAD_DIM
    scale = hd ** -0.5
    probs = [(i, h) for i in range(nb) for h in range(MEM_HEADS)]
    lanes = [slice(h * hd, (h + 1) * hd) for h in range(MEM_HEADS)]
    qs = [q_ref[i, :, lanes[h]] for i, h in probs]
    ks = [k_ref[i, pl.ds(h, MEM_TOKENS, stride=MEM_HEADS), :] for i, h in probs]
    vs = [v_ref[i, pl.ds(h, MEM_TOKENS, stride=MEM_HEADS), :] for i, h in probs]
    ss = [_bdot(q, k, _NT) * scale for q, k in zip(qs, ks)]
    es = [jnp.exp(s - jnp.max(s, axis=-1, keepdims=True)) for s in ss]
    oms = [_bdot(e, v) / jnp.sum(e, axis=-1, keepdims=True) for e, v in zip(es, vs)]
    for (i, h), om in zip(probs, oms):
        o_ref[i, :, lanes[h]] = (om * _silu(g_ref[i, :, lanes[h]])).astype(o_ref.dtype)


def _mem_group(proj3, mem_k, mem_v, tile, nb):
    b, l, _ = proj3.shape
    cb = COL_BLOCK
    rows = MEM_TOKENS * MEM_HEADS
    return pl.pallas_call(
        functools.partial(_mem_kernel, nb=nb),
        grid=(b // nb, l // tile),
        in_specs=[pl.BlockSpec((nb, tile, cb), lambda i, t: (i, t, MEMQ_BLK)),
                  pl.BlockSpec((nb, tile, cb), lambda i, t: (i, t, MEMG_BLK)),
                  pl.BlockSpec((nb, rows, MEM_HEAD_DIM), lambda i, t: (i, 0, 0)),
                  pl.BlockSpec((nb, rows, MEM_HEAD_DIM), lambda i, t: (i, 0, 0))],
        out_specs=pl.BlockSpec((nb, tile, MEM_WIDTH), lambda i, t: (i, t, 0)),
        out_shape=jax.ShapeDtypeStruct((b, l, MEM_WIDTH), BF16),
        compiler_params=_cparams(("arbitrary", "arbitrary")),
        name="memory_group",
    )(proj3, proj3, mem_k, mem_v)


def _out_kernel(oc_ref, od_ref, om_ref, x_ref, w_ref, p_ref, y_ref):
    half = x_ref.shape[0] // 2
    for r in range(2):
        rows = slice(r * half, (r + 1) * half)
        h = jnp.dot(oc_ref[rows, :], w_ref[0:C_CONV, :], preferred_element_type=F32)
        h = h + jnp.dot(od_ref[rows, :], w_ref[C_CONV:C_CONV + DN_WIDTH, :], preferred_element_type=F32)
        h = h + jnp.dot(om_ref[rows, :], w_ref[C_CONV + DN_WIDTH:, :], preferred_element_type=F32)
        z = DEEPNORM_ALPHA * x_ref[rows, :] + h
        mu = jnp.mean(z, axis=-1, keepdims=True)
        zc = z - mu
        var = jnp.mean(zc * zc, axis=-1, keepdims=True)
        y_ref[rows, :] = zc * lax.rsqrt(var + LN_EPS) * p_ref[0:1, :] + p_ref[1:2, :]


def _out_proj(oc, od, om, x2, w_out, ln_p, bm):
    m = x2.shape[0]
    return pl.pallas_call(
        _out_kernel,
        grid=(m // bm,),
        in_specs=[pl.BlockSpec((bm, C_CONV), lambda i: (i, 0)),
                  pl.BlockSpec((bm, DN_WIDTH), lambda i: (i, 0)),
                  pl.BlockSpec((bm, MEM_WIDTH), lambda i: (i, 0)),
                  pl.BlockSpec((bm, D_MODEL), lambda i: (i, 0)),
                  pl.BlockSpec((D_MODEL, D_MODEL), lambda i: (0, 0)),
                  pl.BlockSpec((8, D_MODEL), lambda i: (0, 0))],
        out_specs=pl.BlockSpec((bm, D_MODEL), lambda i: (i, 0)),
        out_shape=jax.ShapeDtypeStruct((m, D_MODEL), F32),
        compiler_params=_cparams(("arbitrary",)),
        name="out_proj_ln",
    )(oc, od, om, x2, w_out, ln_p)


def _pad_rows(rows, n):
    a = jnp.stack(rows).astype(F32)
    return jnp.pad(a, ((0, n - a.shape[0]), (0, 0)))


def _layer(x, conv_prev, qkv_prev, s0, mem_k, mem_v, wts, *, bm, conv_tile, conv_nb, dn_tile, dn_chunk, dn_nb,
           mem_tile, mem_nb, out_bm):
    (w_in_t, layer, conv_w, conv_p, qkv_w, p_row, p_col, norm_w, w_out, ln_p) = wts
    b, l, _ = x.shape
    x2 = x.reshape(b * l, D_MODEL)
    proj, bdc, bdr = _in_proj(x2, w_in_t, layer, bm)
    proj3 = proj.reshape(b, l, N_MAIN)
    bdc3 = bdc.reshape(b, l, SMALL_PAD)
    bdr4 = bdr.reshape(2 * DN_HEADS, b, l // dn_chunk, dn_chunk).transpose(1, 2, 0, 3)

    oc, conv_state = _conv_group(proj3, conv_prev, conv_w, conv_p, conv_tile, conv_nb)
    od, qkv_state, s_final = _dn_group(proj3, bdc3, bdr4, qkv_prev, s0, qkv_w, p_row, p_col, norm_w,
                                       dn_tile, dn_chunk, dn_nb)
    om = _mem_group(proj3, mem_k, mem_v, mem_tile, mem_nb)
    y = _out_proj(oc.reshape(b * l, C_CONV), od.reshape(b * l, DN_WIDTH), om.reshape(b * l, MEM_WIDTH),
                  x2, w_out, ln_p, out_bm)
    return y.reshape(b, l, D_MODEL), conv_state, qkv_state, s_final


def kernel(x_prompt, x_sample, mem_prompt, state_conv, state_qkv_conv, state_delta, cache_mem_k, cache_mem_v,
           w_in, conv_w, conv_b, conv_ln_g, conv_ln_b, qkv_conv_w, a_log, dt_bias, delta_norm_w,
           w_mem_k, w_mem_v, w_out, ln_g, ln_b):
    bp = x_prompt.shape[0]
    bs = x_sample.shape[0]
    hp, hs = x_prompt, x_sample
    outs = [[] for _ in range(8)]
    w_in_t = jnp.swapaxes(w_in, 1, 2)
    for i in range(DEPTH):
        conv_p = _pad_rows([conv_b[i], conv_ln_g[i], conv_ln_b[i]], 8)
        p_row = jnp.stack([a_log[i], dt_bias[i]]).astype(F32)
        wts = (w_in_t, i, conv_w[i], conv_p, qkv_conv_w[i], p_row, p_row.T,
               delta_norm_w[i].reshape(1, DN_HEAD_DIM), w_out[i].astype(BF16),
               _pad_rows([ln_g[i], ln_b[i]], 8))

        w_kv = jnp.concatenate([w_mem_k[i], w_mem_v[i]], axis=1).astype(BF16)
        kv = _matmul(mem_prompt.reshape(bp * MEM_TOKENS, D_MODEL), w_kv, 512, 512)
        mk = kv[:, :MEM_WIDTH].reshape(bp, MEM_TOKENS, MEM_HEADS, MEM_HEAD_DIM)
        mv = kv[:, MEM_WIDTH:].reshape(bp, MEM_TOKENS, MEM_HEADS, MEM_HEAD_DIM)
        kv_rows = MEM_TOKENS * MEM_HEADS
        hp, c_st, q_st, s_st = _layer(
            hp, None, None, None, mk.reshape(bp, kv_rows, MEM_HEAD_DIM), mv.reshape(bp, kv_rows, MEM_HEAD_DIM),
            wts, bm=2048, conv_tile=512, conv_nb=1, dn_tile=256, dn_chunk=64, dn_nb=1, mem_tile=512, mem_nb=1,
            out_bm=512)
        outs[0].append(c_st); outs[1].append(q_st); outs[2].append(s_st)
        outs[3].append(mk)
        outs[4].append(mv)

        ls = hs.shape[1]
        hs, c_st, q_st, s_st = _layer(
            hs, state_conv[i], state_qkv_conv[i], state_delta[i],
            cache_mem_k[i].reshape(bs, kv_rows, MEM_HEAD_DIM), cache_mem_v[i].reshape(bs, kv_rows, MEM_HEAD_DIM),
            wts, bm=bs * ls, conv_tile=ls, conv_nb=8, dn_tile=ls, dn_chunk=ls, dn_nb=8, mem_tile=ls, mem_nb=8,
            out_bm=512)
        outs[5].append(c_st); outs[6].append(q_st); outs[7].append(s_st)
    return (hp, hs) + tuple(jnp.stack(o) for o in outs)
```

```python
import functools

import jax
import jax.numpy as jnp
from jax import lax
from jax.experimental import pallas as pl
from jax.experimental.pallas import tpu as pltpu

F32 = jnp.float32
BF16 = jnp.bfloat16

D_MODEL = 2048
C_CONV = 512
CONV_K = 31
DN_HEADS = 8
DN_HEAD_DIM = 128
DN_WIDTH = DN_HEADS * DN_HEAD_DIM
DN_CONV_K = 4
MEM_HEADS = 4
MEM_HEAD_DIM = 128
MEM_WIDTH = MEM_HEADS * MEM_HEAD_DIM
MEM_TOKENS = 256
DEPTH = 1
DEEPNORM_ALPHA = (2 * DEPTH) ** 0.25
LN_EPS = 1e-5
NORM_EPS = 1e-6

N_MAIN = 3 * DN_WIDTH + DN_WIDTH + 3 * C_CONV + 2 * MEM_WIDTH
COL_BLOCK = 512
QKV_BLK = 0
DNGATE_BLK = 3
GLUA_BLK, GLUB_BLK, CGATE_BLK, MEMQ_BLK, MEMG_BLK = 8, 9, 10, 11, 12
SMALL_PAD = 128
N_DIRECT_BLOCKS = (3 * C_CONV + 4 * DN_WIDTH) // COL_BLOCK
CONV_HALO = 32
QKV_HALO = 8
SUBLANES = 8
CONV_ROW_BLOCK = 64

VMEM_LIMIT = 56 * 1024 * 1024


def _sigmoid(x):
    return 1.0 / (1.0 + jnp.exp(-x))


def _silu(x):
    return x * _sigmoid(x)


def _softplus(x):
    return jnp.maximum(x, 0.0) + jnp.log(1.0 + jnp.exp(-jnp.abs(x)))


def _cparams(sem):
    return pltpu.CompilerParams(dimension_semantics=sem, vmem_limit_bytes=VMEM_LIMIT)


def _proj_kernel(x_ref, wa_ref, wb_ref, o_ref, bdc_ref, bdr_ref, xb_ref):
    j = pl.program_id(1)
    nt = (((1,), (1,)), ((), ()))
    n_bd = 2 * DN_HEADS

    @pl.when(j == 0)
    def _():
        xb_ref[...] = x_ref[...].astype(BF16)

    @pl.when(j < N_DIRECT_BLOCKS)
    def _():
        o_ref[...] = lax.dot_general(xb_ref[...], wa_ref[...].astype(BF16), nt, preferred_element_type=F32)

    @pl.when(j >= N_DIRECT_BLOCKS)
    def _():
        w = jnp.concatenate([wa_ref[n_bd:, :], wb_ref[...]], axis=0).astype(BF16)
        o_ref[...] = lax.dot_general(xb_ref[...], w, nt, preferred_element_type=F32)

    @pl.when(j == N_DIRECT_BLOCKS)
    def _():
        bd = lax.dot_general(xb_ref[...], wa_ref[0:SMALL_PAD, :].astype(BF16), nt, preferred_element_type=F32)
        bdc_ref[...] = bd
        bdr_ref[...] = bd.T[0:n_bd, :]


def _proj_out_block(j):
    n_conv = 3 * C_CONV // COL_BLOCK
    return jnp.where(j < n_conv, j + GLUA_BLK, jnp.where(j < N_DIRECT_BLOCKS, j - n_conv, j))


def _in_proj(x2, w_in_t, layer, bm):
    m = x2.shape[0]
    bn = COL_BLOCK
    n_blocks = N_MAIN // bn
    n_bd = 2 * DN_HEADS
    x_mode = pl.Buffered(1) if m > bm else None
    return pl.pallas_call(
        _proj_kernel,
        grid=(m // bm, n_blocks),
        in_specs=[
            pl.BlockSpec((bm, D_MODEL), lambda i, j: (i, 0), pipeline_mode=x_mode),
            pl.BlockSpec((None, bn, D_MODEL), lambda i, j: (layer, j, 0)),
            pl.BlockSpec((None, n_bd, D_MODEL),
                         lambda i, j: (layer, (jnp.maximum(j + 1, n_blocks - 1)) * (bn // n_bd), 0)),
        ],
        out_specs=[
            pl.BlockSpec((bm, bn), lambda i, j: (i, _proj_out_block(j))),
            pl.BlockSpec((bm, SMALL_PAD), lambda i, j: (i, 0)),
            pl.BlockSpec((2 * DN_HEADS, bm), lambda i, j: (0, i)),
        ],
        out_shape=[
            jax.ShapeDtypeStruct((m, N_MAIN), F32),
            jax.ShapeDtypeStruct((m, SMALL_PAD), F32),
            jax.ShapeDtypeStruct((2 * DN_HEADS, m), F32),
        ],
        scratch_shapes=[pltpu.VMEM((bm, D_MODEL), BF16)],
        compiler_params=_cparams(("arbitrary", "arbitrary")),
        name="in_proj",
    )(x2, w_in_t, w_in_t)


def _mm_kernel(x_ref, w_ref, o_ref):
    o_ref[...] = jnp.dot(x_ref[...].astype(BF16), w_ref[...], preferred_element_type=F32)


def _matmul(x2, w, bm, bn):
    m, k = x2.shape
    n = w.shape[1]
    return pl.pallas_call(
        _mm_kernel,
        grid=(m // bm, n // bn),
        in_specs=[pl.BlockSpec((bm, k), lambda i, j: (i, 0)),
                  pl.BlockSpec((k, bn), lambda i, j: (0, j))],
        out_specs=pl.BlockSpec((bm, bn), lambda i, j: (i, j)),
        out_shape=jax.ShapeDtypeStruct((m, n), F32),
        compiler_params=_cparams(("arbitrary", "arbitrary")),
        name="mem_kv_proj",
    )(x2, w)


def _conv_kernel(*refs, tile, nb, has_state):
    if has_state:
        (a_ref, b_ref, g_ref, st_ref, w_ref, p_ref, o_ref, so_ref, xp_ref) = refs
    else:
        (a_ref, b_ref, g_ref, w_ref, p_ref, o_ref, so_ref, xp_ref) = refs
        st_ref = None
    t = pl.program_id(1)
    last = pl.num_programs(1) - 1
    lo = CONV_HALO - (CONV_K - 1)
    n = tile + CONV_HALO
    rb = min(CONV_ROW_BLOCK, tile)
    w = w_ref[...]
    conv_b = p_ref[0:1, :]
    ln_g = p_ref[1:2, :]
    ln_b = p_ref[2:3, :]
    for i in range(nb):
        @pl.when(t == 0)
        def _():
            xp_ref[i, 0, 0:CONV_HALO, :] = jnp.zeros((CONV_HALO, C_CONV), F32)
            if has_state:
                xp_ref[i, 0, lo:CONV_HALO, :] = st_ref[i]

        xp_ref[i, 0, CONV_HALO:n, :] = a_ref[i] * _sigmoid(b_ref[i])
        xp = xp_ref[i, 0]
        for s in range(1, SUBLANES):
            xp_ref[i, s] = pltpu.roll(xp, n - s, axis=0)

        for r in range(tile // rb):
            acc = None
            for j in range(CONV_K):
                s = (lo + j) % SUBLANES
                start = lo + j - s + r * rb
                term = xp_ref[i, s, start:start + rb, :] * w[j:j + 1, :]
                acc = term if acc is None else acc + term
            hc = acc + conv_b
            mu = jnp.mean(hc, axis=-1, keepdims=True)
            xc = hc - mu
            var = jnp.mean(xc * xc, axis=-1, keepdims=True)
            hn = xc * lax.rsqrt(var + LN_EPS) * ln_g + ln_b
            rows = slice(r * rb, (r + 1) * rb)
            o_ref[i, rows, :] = (_silu(hn) * _silu(g_ref[i, rows, :])).astype(o_ref.dtype)

        @pl.when(t == last)
        def _():
            so_ref[i] = xp_ref[i, 0, tile + lo:n, :]

        xp_ref[i, 0, 0:CONV_HALO, :] = xp_ref[i, 0, tile:n, :]


def _conv_group(proj3, state, conv_w, conv_p, tile, nb):
    b, l, _ = proj3.shape
    has_state = state is not None
    cb = COL_BLOCK
    in_specs = [
        pl.BlockSpec((nb, tile, cb), lambda i, t: (i, t, GLUA_BLK)),
        pl.BlockSpec((nb, tile, cb), lambda i, t: (i, t, GLUB_BLK)),
        pl.BlockSpec((nb, tile, cb), lambda i, t: (i, t, CGATE_BLK)),
    ]
    args = [proj3, proj3, proj3]
    if has_state:
        in_specs.append(pl.BlockSpec((nb, CONV_K - 1, C_CONV), lambda i, t: (i, 0, 0)))
        args.append(state)
    in_specs += [pl.BlockSpec((CONV_K, C_CONV), lambda i, t: (0, 0)),
                 pl.BlockSpec((8, C_CONV), lambda i, t: (0, 0))]
    args += [conv_w, conv_p]
    return pl.pallas_call(
        functools.partial(_conv_kernel, tile=tile, nb=nb, has_state=has_state),
        grid=(b // nb, l // tile),
        in_specs=in_specs,
        out_specs=[pl.BlockSpec((nb, tile, C_CONV), lambda i, t: (i, t, 0)),
                   pl.BlockSpec((nb, CONV_K - 1, C_CONV), lambda i, t: (i, 0, 0))],
        out_shape=[jax.ShapeDtypeStruct((b, l, C_CONV), BF16),
                   jax.ShapeDtypeStruct((b, CONV_K - 1, C_CONV), F32)],
        scratch_shapes=[pltpu.VMEM((nb, SUBLANES, tile + CONV_HALO, C_CONV), F32)],
        compiler_params=_cparams(("arbitrary", "arbitrary")),
        name="conv_group",
    )(*args)


def _bdot(a, b, dims=None):
    a = a.astype(BF16)
    b = b.astype(BF16)
    if dims is None:
        return jnp.dot(a, b, preferred_element_type=F32)
    return lax.dot_general(a, b, (dims, ((), ())), preferred_element_type=F32)


_NT = ((1,), (1,))
_TN = ((0,), (0,))


def _split_bf16(a):
    hi = a.astype(BF16)
    return hi, (a - hi.astype(F32)).astype(BF16)


def _unit_lower_inverses(ms, c):
    row = lax.broadcasted_iota(jnp.int32, (c, c), 0)
    col = lax.broadcasted_iota(jnp.int32, (c, c), 1)
    eye = (row == col).astype(F32)
    ps = [-m for m in ms]
    ts = [eye + p for p in ps]
    span = 2
    while span < c:
        ps = [_bdot(p, p) for p in ps]
        ts = [t + _bdot(t, p) for t, p in zip(ts, ps)]
        span *= 2
    msp = [_split_bf16(m) for m in ms]
    tsp = [_split_bf16(t) for t in ts]
    mts = [_bdot(mh, th) + _bdot(mh, tl) + _bdot(ml, th) for (mh, ml), (th, tl) in zip(msp, tsp)]
    rs = [eye - t - mt for t, mt in zip(ts, mts)]
    return [t + _bdot(th, r) for t, (th, _), r in zip(ts, tsp, rs)]


def _dn_kernel(*refs, tile, chunk, nb, has_state):
    if has_state:
        (qkv_ref, gate_ref, bdc_ref, bdr_ref, prev_ref, s0_ref, w_ref, pr_ref, pc_ref, nw_ref,
         o_ref, st_ref, s_ref, xp_ref, xs_ref, osc_ref) = refs
    else:
        (qkv_ref, gate_ref, bdc_ref, bdr_ref, w_ref, pr_ref, pc_ref, nw_ref,
         o_ref, st_ref, s_ref, xp_ref, xs_ref, osc_ref) = refs
        prev_ref = s0_ref = None
    c = chunk
    nchunk = tile // c
    hd = DN_HEAD_DIM
    t = pl.program_id(1)
    last = pl.num_programs(1) - 1
    lo = QKV_HALO - (DN_CONV_K - 1)
    hi = lax.Precision.HIGHEST
    w = w_ref[...]

    for i in range(nb):
        @pl.when(t == 0)
        def _():
            xp_ref[i, 0:QKV_HALO, :] = jnp.zeros((QKV_HALO, 3 * DN_WIDTH), F32)
            if has_state:
                xp_ref[i, lo:QKV_HALO, :] = prev_ref[i]
                s_ref[i] = s0_ref[i]
            else:
                s_ref[i] = jnp.zeros((DN_HEADS, hd, hd), F32)

        xp_ref[i, QKV_HALO:QKV_HALO + tile, :] = qkv_ref[i]
        xp = xp_ref[i]
        for j in range(DN_CONV_K - 1):
            xs_ref[i, j] = pltpu.roll(xp, tile + QKV_HALO - (lo + j), axis=0)

        @pl.when(t == last)
        def _():
            st_ref[i] = xp_ref[i, tile + lo:tile + QKV_HALO, :]

    neg_a_row = -jnp.exp(pr_ref[0:1, :])
    dtb_row = pr_ref[1:2, :]
    neg_a_col = -jnp.exp(pc_ref[:, 0:1])
    dtb_col = pc_ref[:, 1:2]
    norm_w = nw_ref[...]

    row = lax.broadcasted_iota(jnp.int32, (c, c), 0)
    col = lax.broadcasted_iota(jnp.int32, (c, c), 1)
    incl = row >= col
    strict = row > col
    tri_l = incl.astype(F32)
    tri_u = (row <= col).astype(F32)

    seq_chunks = [(i, n) for i in range(nb) for n in range(nchunk)]
    probs = [(i, n, h) for i, n in seq_chunks for h in range(DN_HEADS)]
    beta, gcum_col, gcum_row, prep = {}, {}, {}, {}
    for i, n in seq_chunks:
        rows = slice(n * c, (n + 1) * c)
        bd = bdc_ref[i, rows, :]
        beta[i, n] = _sigmoid(bd[:, 0:DN_HEADS])
        g_col = neg_a_row * _softplus(bd[:, DN_HEADS:2 * DN_HEADS] + dtb_row)
        gcum_col[i, n] = jnp.dot(tri_l, g_col, precision=hi, preferred_element_type=F32)
        br = bdr_ref[i, n]
        g_row = neg_a_col * _softplus(br[DN_HEADS:2 * DN_HEADS, :] + dtb_col)
        gcum_row[i, n] = jnp.dot(g_row, tri_u, precision=hi, preferred_element_type=F32)
        acc = xp_ref[i, QKV_HALO + n * c:QKV_HALO + (n + 1) * c, :] * w[DN_CONV_K - 1:DN_CONV_K, :]
        for j in range(DN_CONV_K - 1):
            acc = acc + xs_ref[i, j, rows, :] * w[j:j + 1, :]
        qkvc = _silu(acc)
        for h in range(DN_HEADS):
            q, k, v = (qkvc[:, part * DN_WIDTH + h * hd:part * DN_WIDTH + (h + 1) * hd] for part in range(3))
            q = q * (lax.rsqrt(jnp.sum(q * q, axis=-1, keepdims=True) + NORM_EPS) * (hd ** -0.5))
            k = k * lax.rsqrt(jnp.sum(k * k, axis=-1, keepdims=True) + NORM_EPS)
            gc = gcum_col[i, n][:, h:h + 1]
            gr = gcum_row[i, n][h:h + 1, :]
            bt = beta[i, n][:, h:h + 1]
            gl = gcum_col[i, n][c - 1:c, h:h + 1]
            e = jnp.exp(gc)
            kb = k * bt
            prep[i, n, h] = (
                jnp.concatenate([kb, q], axis=0).astype(BF16),
                k.astype(BF16),
                jnp.exp(jnp.where(incl, gc - gr, -jnp.inf)),
                jnp.concatenate([v * bt, kb * e], axis=1).astype(BF16),
                q * e,
                (k * jnp.exp(gl - gc)).astype(BF16),
                jnp.exp(gl))

    preps = [prep[p] for p in probs]
    kqs = [_bdot(p[0], p[1], _NT) for p in preps]
    aqks = [(kq[c:] * p[2]).astype(BF16) for kq, p in zip(kqs, preps)]
    ms = [jnp.where(strict, kq[:c] * p[2], 0.0) for kq, p in zip(kqs, preps)]
    tinvs = _unit_lower_inverses(ms, c)
    uws = [_bdot(tinv, p[3]) for tinv, p in zip(tinvs, preps)]
    wqs = [jnp.concatenate([uw[:, hd:], p[4]], axis=0).astype(BF16) for uw, p in zip(uws, preps)]
    local = dict(zip(probs, zip(uws, wqs, aqks, [p[5] for p in preps], [p[6] for p in preps])))

    heads = [(i, h) for i in range(nb) for h in range(DN_HEADS)]
    state = {(i, h): s_ref[i, h] for i, h in heads}
    for n in range(nchunk):
        rows = slice(n * c, (n + 1) * c)
        items = [(i, h) + local[i, n, h] for i, h in heads]
        s16s = [state[i, h].astype(BF16) for i, h in heads]
        wss = [_bdot(it[3], s16) for it, s16 in zip(items, s16s)]
        u16s = [(it[2][:, :hd] - ws[:c]).astype(BF16) for it, ws in zip(items, wss)]
        outs = [ws[c:] + _bdot(it[4], u16) for it, ws, u16 in zip(items, wss, u16s)]
        for it, u16 in zip(items, u16s):
            i, h = it[0], it[1]
            state[i, h] = state[i, h] * it[6] + _bdot(it[5], u16, _TN)
        for it, o in zip(items, outs):
            i, h = it[0], it[1]
            o = o * lax.rsqrt(jnp.mean(o * o, axis=-1, keepdims=True) + NORM_EPS) * norm_w
            osc_ref[i, rows, h * hd:(h + 1) * hd] = o
    for i, h in heads:
        s_ref[i, h] = state[i, h]

    for i in range(nb):
        xp_ref[i, 0:QKV_HALO, :] = xp_ref[i, tile:tile + QKV_HALO, :]
        o_ref[i] = (osc_ref[i] * _silu(gate_ref[i])).astype(o_ref.dtype)


def _dn_group(proj3, bdc3, bdr4, prev, s0, qkv_w, p_row, p_col, norm_w, tile, chunk, nb):
    b, l, _ = proj3.shape
    has_state = s0 is not None
    nchunk = tile // chunk
    hd = DN_HEAD_DIM
    in_specs = [
        pl.BlockSpec((nb, tile, 3 * DN_WIDTH), lambda i, t: (i, t, QKV_BLK)),
        pl.BlockSpec((nb, tile, DN_WIDTH), lambda i, t: (i, t, DNGATE_BLK)),
        pl.BlockSpec((nb, tile, SMALL_PAD), lambda i, t: (i, t, 0)),
        pl.BlockSpec((nb, nchunk, 2 * DN_HEADS, chunk), lambda i, t: (i, t, 0, 0)),
    ]
    args = [proj3, proj3, bdc3, bdr4]
    if has_state:
        in_specs += [pl.BlockSpec((nb, DN_CONV_K - 1, 3 * DN_WIDTH), lambda i, t: (i, 0, 0)),
                     pl.BlockSpec((nb, DN_HEADS, hd, hd), lambda i, t: (i, 0, 0, 0))]
        args += [prev, s0]
    in_specs += [pl.BlockSpec((DN_CONV_K, 3 * DN_WIDTH), lambda i, t: (0, 0)),
                 pl.BlockSpec((2, DN_HEADS), lambda i, t: (0, 0)),
                 pl.BlockSpec((DN_HEADS, 2), lambda i, t: (0, 0)),
                 pl.BlockSpec((1, hd), lambda i, t: (0, 0))]
    args += [qkv_w, p_row, p_col, norm_w]
    return pl.pallas_call(
        functools.partial(_dn_kernel, tile=tile, chunk=chunk, nb=nb, has_state=has_state),
        grid=(b // nb, l // tile),
        in_specs=in_specs,
        out_specs=[pl.BlockSpec((nb, tile, DN_WIDTH), lambda i, t: (i, t, 0)),
                   pl.BlockSpec((nb, DN_CONV_K - 1, 3 * DN_WIDTH), lambda i, t: (i, 0, 0)),
                   pl.BlockSpec((nb, DN_HEADS, hd, hd), lambda i, t: (i, 0, 0, 0))],
        out_shape=[jax.ShapeDtypeStruct((b, l, DN_WIDTH), BF16),
                   jax.ShapeDtypeStruct((b, DN_CONV_K - 1, 3 * DN_WIDTH), F32),
                   jax.ShapeDtypeStruct((b, DN_HEADS, hd, hd), F32)],
        scratch_shapes=[pltpu.VMEM((nb, tile + QKV_HALO, 3 * DN_WIDTH), F32),
                        pltpu.VMEM((nb, DN_CONV_K - 1, tile + QKV_HALO, 3 * DN_WIDTH), F32),
                        pltpu.VMEM((nb, tile, DN_WIDTH), F32)],
        compiler_params=_cparams(("arbitrary", "arbitrary")),
        name="deltanet_group",
    )(*args)


def _dn_pipelined_kernel(q0_ref, qa_ref, qb_ref, c0_ref, ca_ref, cb_ref, r0_ref, ra_ref, rb_ref, gate_ref,
                         w_ref, pr_ref, pc_ref, nw_ref, o_ref, st_ref, s_ref,
                         xp_ref, kq_ref, k16_ref, uwr_ref, qe_ref, kd_ref, gcc_ref, gcr_ref, osc_ref,
                         *, tile, chunk, nt, n_tiles):
    c = chunk
    nchunk = tile // c
    hd = DN_HEAD_DIM
    lo = QKV_HALO - (DN_CONV_K - 1)
    hi = lax.Precision.HIGHEST
    k = pl.program_id(0)
    w = w_ref[...]
    neg_a_row = -jnp.exp(pr_ref[0:1, :])
    dtb_row = pr_ref[1:2, :]
    neg_a_col = -jnp.exp(pc_ref[:, 0:1])
    dtb_col = pc_ref[:, 1:2]
    norm_w = nw_ref[...]
    row = lax.broadcasted_iota(jnp.int32, (c, c), 0)
    col = lax.broadcasted_iota(jnp.int32, (c, c), 1)
    incl = row >= col
    strict = row > col
    tri_l = incl.astype(F32)
    tri_u = (row <= col).astype(F32)
    probs = [(n, h) for n in range(nchunk) for h in range(DN_HEADS)]

    def prepare(buf, q_ref, c_ref, r_ref, first, write_state):
        halo = xp_ref[0:QKV_HALO, :]
        xp_ref[0:QKV_HALO, :] = jnp.zeros_like(halo) if first is True else jnp.where(first, 0.0, halo)
        xp_ref[QKV_HALO:QKV_HALO + tile, :] = q_ref[0]
        if write_state:
            st_ref[0] = xp_ref[tile + lo:tile + QKV_HALO, :]
        for n in range(nchunk):
            rows = slice(n * c, (n + 1) * c)
            bd = c_ref[0, rows, :]
            beta = _sigmoid(bd[:, 0:DN_HEADS])
            g_col = neg_a_row * _softplus(bd[:, DN_HEADS:2 * DN_HEADS] + dtb_row)
            gcum_col = jnp.dot(tri_l, g_col, precision=hi, preferred_element_type=F32)
            br = r_ref[0, n]
            g_row = neg_a_col * _softplus(br[DN_HEADS:2 * DN_HEADS, :] + dtb_col)
            gcc_ref[buf, n] = gcum_col
            gcr_ref[buf, n] = jnp.dot(g_row, tri_u, precision=hi, preferred_element_type=F32)
            window = xp_ref[n * c:(n + 1) * c + QKV_HALO, :]
            acc = window[QKV_HALO:, :] * w[DN_CONV_K - 1:DN_CONV_K, :]
            for j in range(DN_CONV_K - 1):
                acc = acc + pltpu.roll(window, c + QKV_HALO - (lo + j), axis=0)[0:c, :] * w[j:j + 1, :]
            qkvc = _silu(acc)
            for h in range(DN_HEADS):
                p = n * DN_HEADS + h
                q, kk, v = (qkvc[:, part * DN_WIDTH + h * hd:part * DN_WIDTH + (h + 1) * hd] for part in range(3))
                q = q * (lax.rsqrt(jnp.sum(q * q, axis=-1, keepdims=True) + NORM_EPS) * (hd ** -0.5))
                kk = kk * lax.rsqrt(jnp.sum(kk * kk, axis=-1, keepdims=True) + NORM_EPS)
                gc = gcum_col[:, h:h + 1]
                bt = beta[:, h:h + 1]
                gl = gcum_col[c - 1:c, h:h + 1]
                e = jnp.exp(gc)
                kb = kk * bt
                kq_ref[buf, p] = jnp.concatenate([kb, q], axis=0).astype(BF16)
                k16_ref[buf, p] = kk.astype(BF16)
                uwr_ref[buf, p] = jnp.concatenate([v * bt, kb * e], axis=1).astype(BF16)
                qe_ref[buf, p] = q * e
                kd_ref[buf, p] = (kk * jnp.exp(gl - gc)).astype(BF16)
        xp_ref[0:QKV_HALO, :] = xp_ref[tile:tile + QKV_HALO, :]

    def finish(buf, half):
        idx = [n * DN_HEADS + h for n, h in probs]
        kqs = [_bdot(kq_ref[buf, p], k16_ref[buf, p], _NT) for p in idx]
        gcs = [gcc_ref[buf, n][:, h:h + 1] for n, h in probs]
        decays = [jnp.exp(jnp.where(incl, gc - gcr_ref[buf, n][h:h + 1, :], -jnp.inf))
                  for gc, (n, h) in zip(gcs, probs)]
        aqks = [(kq[c:] * dc).astype(BF16) for kq, dc in zip(kqs, decays)]
        ms = [jnp.where(strict, kq[:c] * dc, 0.0) for kq, dc in zip(kqs, decays)]
        tinvs = _unit_lower_inverses(ms, c)
        uws = [_bdot(tinv, uwr_ref[buf, p]) for tinv, p in zip(tinvs, idx)]
        wqs = [jnp.concatenate([uw[:, hd:], qe_ref[buf, p]], axis=0).astype(BF16) for uw, p in zip(uws, idx)]
        egls = [jnp.exp(gc[c - 1:c, :]) for gc in gcs]
        local = dict(zip(probs, zip(uws, wqs, aqks, egls)))
        state = [s_ref[0, h] for h in range(DN_HEADS)]
        for n in range(nchunk):
            rows = slice(n * c, (n + 1) * c)
            items = [local[n, h] for h in range(DN_HEADS)]
            s16s = [s.astype(BF16) for s in state]
            wss = [_bdot(it[1], s16) for it, s16 in zip(items, s16s)]
            u16s = [(it[0][:, :hd] - ws[:c]).astype(BF16) for it, ws in zip(items, wss)]
            outs = [ws[c:] + _bdot(it[2], u16) for it, ws, u16 in zip(items, wss, u16s)]
            state = [s * it[3] + _bdot(kd_ref[buf, n * DN_HEADS + h], u16, _TN)
                     for h, (s, it, u16) in enumerate(zip(state, items, u16s))]
            for h, o in enumerate(outs):
                o = o * lax.rsqrt(jnp.mean(o * o, axis=-1, keepdims=True) + NORM_EPS) * norm_w
                osc_ref[rows, h * hd:(h + 1) * hd] = o
        for h in range(DN_HEADS):
            s_ref[0, h] = state[h]
        rows_out = slice(half * tile, (half + 1) * tile)
        o_ref[0, rows_out, :] = (osc_ref[...] * _silu(gate_ref[0, rows_out, :])).astype(o_ref.dtype)

    @pl.when(k == 0)
    def _():
        prepare(0, q0_ref, c0_ref, r0_ref, True, False)

    @pl.when(lax.rem(2 * k, nt) == 0)
    def _():
        s_ref[0] = jnp.zeros((DN_HEADS, hd, hd), F32)

    prepare(1, qa_ref, ca_ref, ra_ref, False, True)
    finish(0, 0)
    prepare(0, qb_ref, cb_ref, rb_ref, lax.rem(jnp.minimum(2 * k + 2, n_tiles - 1), nt) == 0, False)
    finish(1, 1)


def _dn_group_pipelined(proj3, bdc3, bdr4, qkv_w, p_row, p_col, norm_w, tile, chunk):
    b, l, _ = proj3.shape
    nt = l // tile
    assert nt % 2 == 0
    n_tiles = b * nt
    nchunk = tile // chunk
    hd = DN_HEAD_DIM
    n_prob = nchunk * DN_HEADS
    proj_t = proj3.reshape(n_tiles, tile, N_MAIN)
    proj_p = proj3.reshape(n_tiles // 2, 2 * tile, N_MAIN)
    bdc_t = bdc3.reshape(n_tiles, tile, SMALL_PAD)
    bdr_t = bdr4.reshape(n_tiles, nchunk, 2 * DN_HEADS, chunk)
    odd = lambda k: 2 * k + 1
    nxt = lambda k: jnp.minimum(2 * k + 2, n_tiles - 1)
    once = pl.Buffered(1)

    def tile_specs(index):
        mode = once if index is None else None
        at = (lambda k: 0) if index is None else index
        return [pl.BlockSpec((1, tile, 3 * DN_WIDTH), lambda k: (at(k), 0, QKV_BLK), pipeline_mode=mode),
                pl.BlockSpec((1, tile, SMALL_PAD), lambda k: (at(k), 0, 0), pipeline_mode=mode),
                pl.BlockSpec((1, nchunk, 2 * DN_HEADS, chunk), lambda k: (at(k), 0, 0, 0), pipeline_mode=mode)]

    (q0, c0, r0), (qa, ca, ra), (qb, cb, rb) = tile_specs(None), tile_specs(odd), tile_specs(nxt)
    const2 = lambda k: (0, 0)
    od, qkv_state, s_final = pl.pallas_call(
        functools.partial(_dn_pipelined_kernel, tile=tile, chunk=chunk, nt=nt, n_tiles=n_tiles),
        grid=(n_tiles // 2,),
        in_specs=[q0, qa, qb, c0, ca, cb, r0, ra, rb,
                  pl.BlockSpec((1, 2 * tile, DN_WIDTH), lambda k: (k, 0, DNGATE_BLK)),
                  pl.BlockSpec((DN_CONV_K, 3 * DN_WIDTH), const2),
                  pl.BlockSpec((2, DN_HEADS), const2),
                  pl.BlockSpec((DN_HEADS, 2), const2),
                  pl.BlockSpec((1, hd), const2)],
        out_specs=[pl.BlockSpec((1, 2 * tile, DN_WIDTH), lambda k: (k, 0, 0)),
                   pl.BlockSpec((1, DN_CONV_K - 1, 3 * DN_WIDTH), lambda k: (2 * k // nt, 0, 0)),
                   pl.BlockSpec((1, DN_HEADS, hd, hd), lambda k: (2 * k // nt, 0, 0, 0))],
        out_shape=[jax.ShapeDtypeStruct((n_tiles // 2, 2 * tile, DN_WIDTH), BF16),
                   jax.ShapeDtypeStruct((b, DN_CONV_K - 1, 3 * DN_WIDTH), F32),
                   jax.ShapeDtypeStruct((b, DN_HEADS, hd, hd), F32)],
        scratch_shapes=[pltpu.VMEM((tile + QKV_HALO, 3 * DN_WIDTH), F32),
                        pltpu.VMEM((2, n_prob, 2 * chunk, hd), BF16),
                        pltpu.VMEM((2, n_prob, chunk, hd), BF16),
                        pltpu.VMEM((2, n_prob, chunk, 2 * hd), BF16),
                        pltpu.VMEM((2, n_prob, chunk, hd), F32),
                        pltpu.VMEM((2, n_prob, chunk, hd), BF16),
                        pltpu.VMEM((2, nchunk, chunk, DN_HEADS), F32),
                        pltpu.VMEM((2, nchunk, DN_HEADS, chunk), F32),
                        pltpu.VMEM((tile, DN_WIDTH), F32)],
        compiler_params=_cparams(("arbitrary",)),
        name="deltanet_pipelined",
    )(proj_t, proj_t, proj_t, bdc_t, bdc_t, bdc_t, bdr_t, bdr_t, bdr_t, proj_p, qkv_w, p_row, p_col, norm_w)
    return od.reshape(b, l, DN_WIDTH), qkv_state, s_final


def _mem_kernel(q_ref, g_ref, k_ref, v_ref, o_ref, *, nb):
    hd = MEM_HEAD_DIM
    scale = hd ** -0.5
    probs = [(i, h) for i in range(nb) for h in range(MEM_HEADS)]
    lanes = [slice(h * hd, (h + 1) * hd) for h in range(MEM_HEADS)]
    qs = [q_ref[i, :, lanes[h]] for i, h in probs]
    ks = [k_ref[i, pl.ds(h, MEM_TOKENS, stride=MEM_HEADS), :] for i, h in probs]
    vs = [v_ref[i, pl.ds(h, MEM_TOKENS, stride=MEM_HEADS), :] for i, h in probs]
    ss = [_bdot(q, k, _NT) * scale for q, k in zip(qs, ks)]
    es = [jnp.exp(s - jnp.max(s, axis=-1, keepdims=True)) for s in ss]
    oms = [_bdot(e, v) / jnp.sum(e, axis=-1, keepdims=True) for e, v in zip(es, vs)]
    for (i, h), om in zip(probs, oms):
        o_ref[i, :, lanes[h]] = (om * _silu(g_ref[i, :, lanes[h]])).astype(o_ref.dtype)


def _mem_group(proj3, mem_k, mem_v, tile, nb):
    b, l, _ = proj3.shape
    cb = COL_BLOCK
    rows = MEM_TOKENS * MEM_HEADS
    return pl.pallas_call(
        functools.partial(_mem_kernel, nb=nb),
        grid=(b // nb, l // tile),
        in_specs=[pl.BlockSpec((nb, tile, cb), lambda i, t: (i, t, MEMQ_BLK)),
                  pl.BlockSpec((nb, tile, cb), lambda i, t: (i, t, MEMG_BLK)),
                  pl.BlockSpec((nb, rows, MEM_HEAD_DIM), lambda i, t: (i, 0, 0)),
                  pl.BlockSpec((nb, rows, MEM_HEAD_DIM), lambda i, t: (i, 0, 0))],
        out_specs=pl.BlockSpec((nb, tile, MEM_WIDTH), lambda i, t: (i, t, 0)),
        out_shape=jax.ShapeDtypeStruct((b, l, MEM_WIDTH), BF16),
        compiler_params=_cparams(("arbitrary", "arbitrary")),
        name="memory_group",
    )(proj3, proj3, mem_k, mem_v)


def _out_kernel(oc_ref, od_ref, om_ref, x_ref, w_ref, p_ref, y_ref):
    half = x_ref.shape[0] // 2
    for r in range(2):
        rows = slice(r * half, (r + 1) * half)
        h = jnp.dot(oc_ref[rows, :], w_ref[0:C_CONV, :], preferred_element_type=F32)
        h = h + jnp.dot(od_ref[rows, :], w_ref[C_CONV:C_CONV + DN_WIDTH, :], preferred_element_type=F32)
        h = h + jnp.dot(om_ref[rows, :], w_ref[C_CONV + DN_WIDTH:, :], preferred_element_type=F32)
        z = DEEPNORM_ALPHA * x_ref[rows, :] + h
        mu = jnp.mean(z, axis=-1, keepdims=True)
        zc = z - mu
        var = jnp.mean(zc * zc, axis=-1, keepdims=True)
        y_ref[rows, :] = zc * lax.rsqrt(var + LN_EPS) * p_ref[0:1, :] + p_ref[1:2, :]


def _out_proj(oc, od, om, x2, w_out, ln_p, bm):
    m = x2.shape[0]
    return pl.pallas_call(
        _out_kernel,
        grid=(m // bm,),
        in_specs=[pl.BlockSpec((bm, C_CONV), lambda i: (i, 0)),
                  pl.BlockSpec((bm, DN_WIDTH), lambda i: (i, 0)),
                  pl.BlockSpec((bm, MEM_WIDTH), lambda i: (i, 0)),
                  pl.BlockSpec((bm, D_MODEL), lambda i: (i, 0)),
                  pl.BlockSpec((D_MODEL, D_MODEL), lambda i: (0, 0)),
                  pl.BlockSpec((8, D_MODEL), lambda i: (0, 0))],
        out_specs=pl.BlockSpec((bm, D_MODEL), lambda i: (i, 0)),
        out_shape=jax.ShapeDtypeStruct((m, D_MODEL), F32),
        compiler_params=_cparams(("arbitrary",)),
        name="out_proj_ln",
    )(oc, od, om, x2, w_out, ln_p)


def _pad_rows(rows, n):
    a = jnp.stack(rows).astype(F32)
    return jnp.pad(a, ((0, n - a.shape[0]), (0, 0)))


def _layer(x, conv_prev, qkv_prev, s0, mem_k, mem_v, wts, *, bm, conv_tile, conv_nb, dn_tile, dn_chunk, dn_nb,
           mem_tile, mem_nb, out_bm):
    (w_in_t, layer, conv_w, conv_p, qkv_w, p_row, p_col, norm_w, w_out, ln_p) = wts
    b, l, _ = x.shape
    x2 = x.reshape(b * l, D_MODEL)
    proj, bdc, bdr = _in_proj(x2, w_in_t, layer, bm)
    proj3 = proj.reshape(b, l, N_MAIN)
    bdc3 = bdc.reshape(b, l, SMALL_PAD)
    bdr4 = bdr.reshape(2 * DN_HEADS, b, l // dn_chunk, dn_chunk).transpose(1, 2, 0, 3)

    oc, conv_state = _conv_group(proj3, conv_prev, conv_w, conv_p, conv_tile, conv_nb)
    if s0 is None:
        od, qkv_state, s_final = _dn_group_pipelined(proj3, bdc3, bdr4, qkv_w, p_row, p_col, norm_w,
                                                     dn_tile, dn_chunk)
    else:
        od, qkv_state, s_final = _dn_group(proj3, bdc3, bdr4, qkv_prev, s0, qkv_w, p_row, p_col, norm_w,
                                           dn_tile, dn_chunk, dn_nb)
    om = _mem_group(proj3, mem_k, mem_v, mem_tile, mem_nb)
    y = _out_proj(oc.reshape(b * l, C_CONV), od.reshape(b * l, DN_WIDTH), om.reshape(b * l, MEM_WIDTH),
                  x2, w_out, ln_p, out_bm)
    return y.reshape(b, l, D_MODEL), conv_state, qkv_state, s_final


def kernel(x_prompt, x_sample, mem_prompt, state_conv, state_qkv_conv, state_delta, cache_mem_k, cache_mem_v,
           w_in, conv_w, conv_b, conv_ln_g, conv_ln_b, qkv_conv_w, a_log, dt_bias, delta_norm_w,
           w_mem_k, w_mem_v, w_out, ln_g, ln_b):
    bp = x_prompt.shape[0]
    bs = x_sample.shape[0]
    hp, hs = x_prompt, x_sample
    outs = [[] for _ in range(8)]
    w_in_t = jnp.swapaxes(w_in, 1, 2)
    for i in range(DEPTH):
        conv_p = _pad_rows([conv_b[i], conv_ln_g[i], conv_ln_b[i]], 8)
        p_row = jnp.stack([a_log[i], dt_bias[i]]).astype(F32)
        wts = (w_in_t, i, conv_w[i], conv_p, qkv_conv_w[i], p_row, p_row.T,
               delta_norm_w[i].reshape(1, DN_HEAD_DIM), w_out[i].astype(BF16),
               _pad_rows([ln_g[i], ln_b[i]], 8))

        w_kv = jnp.concatenate([w_mem_k[i], w_mem_v[i]], axis=1).astype(BF16)
        kv = _matmul(mem_prompt.reshape(bp * MEM_TOKENS, D_MODEL), w_kv, 512, 512)
        mk = kv[:, :MEM_WIDTH].reshape(bp, MEM_TOKENS, MEM_HEADS, MEM_HEAD_DIM)
        mv = kv[:, MEM_WIDTH:].reshape(bp, MEM_TOKENS, MEM_HEADS, MEM_HEAD_DIM)
        kv_rows = MEM_TOKENS * MEM_HEADS
        hp, c_st, q_st, s_st = _layer(
            hp, None, None, None, mk.reshape(bp, kv_rows, MEM_HEAD_DIM), mv.reshape(bp, kv_rows, MEM_HEAD_DIM),
            wts, bm=2048, conv_tile=512, conv_nb=1, dn_tile=256, dn_chunk=64, dn_nb=1, mem_tile=512, mem_nb=1,
            out_bm=512)
        outs[0].append(c_st); outs[1].append(q_st); outs[2].append(s_st)
        outs[3].append(mk)
        outs[4].append(mv)

        ls = hs.shape[1]
        hs, c_st, q_st, s_st = _layer(
            hs, state_conv[i], state_qkv_conv[i], state_delta[i],
            cache_mem_k[i].reshape(bs, kv_rows, MEM_HEAD_DIM), cache_mem_v[i].reshape(bs, kv_rows, MEM_HEAD_DIM),
            wts, bm=bs * ls, conv_tile=ls, conv_nb=8, dn_tile=ls, dn_chunk=ls, dn_nb=8, mem_tile=ls, mem_nb=8,
            out_bm=512)
        outs[5].append(c_st); outs[6].append(q_st); outs[7].append(s_st)
    return (hp, hs) + tuple(jnp.stack(o) for o in outs)
```

```python
import functools

import jax
import jax.numpy as jnp
from jax import lax
from jax.experimental import pallas as pl
from jax.experimental.pallas import tpu as pltpu

F32 = jnp.float32
BF16 = jnp.bfloat16

D_MODEL = 2048
C_CONV = 512
CONV_K = 31
DN_HEADS = 8
DN_HEAD_DIM = 128
DN_WIDTH = DN_HEADS * DN_HEAD_DIM
DN_CONV_K = 4
MEM_HEADS = 4
MEM_HEAD_DIM = 128
MEM_WIDTH = MEM_HEADS * MEM_HEAD_DIM
MEM_TOKENS = 256
DEPTH = 1
DEEPNORM_ALPHA = (2 * DEPTH) ** 0.25
LN_EPS = 1e-5
NORM_EPS = 1e-6

N_MAIN = 3 * DN_WIDTH + DN_WIDTH + 3 * C_CONV + 2 * MEM_WIDTH
COL_BLOCK = 512
QKV_BLK = 0
DNGATE_BLK = 3
GLUA_BLK, GLUB_BLK, CGATE_BLK, MEMQ_BLK, MEMG_BLK = 8, 9, 10, 11, 12
SMALL_PAD = 128
N_DIRECT_BLOCKS = (3 * C_CONV + 4 * DN_WIDTH) // COL_BLOCK
CONV_HALO = 32
QKV_HALO = 8
SUBLANES = 8
CONV_ROW_BLOCK = 64

VMEM_LIMIT = 56 * 1024 * 1024


def _sigmoid(x):
    return 1.0 / (1.0 + jnp.exp(-x))


def _silu(x):
    return x * _sigmoid(x)


def _softplus(x):
    return jnp.maximum(x, 0.0) + jnp.log(1.0 + jnp.exp(-jnp.abs(x)))


def _cparams(sem):
    return pltpu.CompilerParams(dimension_semantics=sem, vmem_limit_bytes=VMEM_LIMIT)


def _proj_kernel(x_ref, wa_ref, wb_ref, o_ref, bdc_ref, bdr_ref, xb_ref):
    j = pl.program_id(1)
    nt = (((1,), (1,)), ((), ()))
    n_bd = 2 * DN_HEADS

    @pl.when(j == 0)
    def _():
        xb_ref[...] = x_ref[...].astype(BF16)

    @pl.when(j < N_DIRECT_BLOCKS)
    def _():
        o_ref[...] = lax.dot_general(xb_ref[...], wa_ref[...].astype(BF16), nt, preferred_element_type=F32)

    @pl.when(j >= N_DIRECT_BLOCKS)
    def _():
        w = jnp.concatenate([wa_ref[n_bd:, :], wb_ref[...]], axis=0).astype(BF16)
        o_ref[...] = lax.dot_general(xb_ref[...], w, nt, preferred_element_type=F32)

    @pl.when(j == N_DIRECT_BLOCKS)
    def _():
        bd = lax.dot_general(xb_ref[...], wa_ref[0:SMALL_PAD, :].astype(BF16), nt, preferred_element_type=F32)
        bdc_ref[...] = bd
        bdr_ref[...] = bd.T[0:n_bd, :]


def _proj_out_block(j):
    n_conv = 3 * C_CONV // COL_BLOCK
    return jnp.where(j < n_conv, j + GLUA_BLK, jnp.where(j < N_DIRECT_BLOCKS, j - n_conv, j))


def _in_proj(x2, w_in_t, layer, bm):
    m = x2.shape[0]
    bn = COL_BLOCK
    n_blocks = N_MAIN // bn
    n_bd = 2 * DN_HEADS
    x_mode = pl.Buffered(1) if m > bm else None
    return pl.pallas_call(
        _proj_kernel,
        grid=(m // bm, n_blocks),
        in_specs=[
            pl.BlockSpec((bm, D_MODEL), lambda i, j: (i, 0), pipeline_mode=x_mode),
            pl.BlockSpec((None, bn, D_MODEL), lambda i, j: (layer, j, 0)),
            pl.BlockSpec((None, n_bd, D_MODEL),
                         lambda i, j: (layer, (jnp.maximum(j + 1, n_blocks - 1)) * (bn // n_bd), 0)),
        ],
        out_specs=[
            pl.BlockSpec((bm, bn), lambda i, j: (i, _proj_out_block(j))),
            pl.BlockSpec((bm, SMALL_PAD), lambda i, j: (i, 0)),
            pl.BlockSpec((2 * DN_HEADS, bm), lambda i, j: (0, i)),
        ],
        out_shape=[
            jax.ShapeDtypeStruct((m, N_MAIN), F32),
            jax.ShapeDtypeStruct((m, SMALL_PAD), F32),
            jax.ShapeDtypeStruct((2 * DN_HEADS, m), F32),
        ],
        scratch_shapes=[pltpu.VMEM((bm, D_MODEL), BF16)],
        compiler_params=_cparams(("arbitrary", "arbitrary")),
        name="in_proj",
    )(x2, w_in_t, w_in_t)


def _mm_kernel(x_ref, w_ref, o_ref):
    o_ref[...] = jnp.dot(x_ref[...].astype(BF16), w_ref[...], preferred_element_type=F32)


def _matmul(x2, w, bm, bn):
    m, k = x2.shape
    n = w.shape[1]
    return pl.pallas_call(
        _mm_kernel,
        grid=(m // bm, n // bn),
        in_specs=[pl.BlockSpec((bm, k), lambda i, j: (i, 0)),
                  pl.BlockSpec((k, bn), lambda i, j: (0, j))],
        out_specs=pl.BlockSpec((bm, bn), lambda i, j: (i, j)),
        out_shape=jax.ShapeDtypeStruct((m, n), F32),
        compiler_params=_cparams(("arbitrary", "arbitrary")),
        name="mem_kv_proj",
    )(x2, w)


def _conv_kernel(*refs, tile, nb, has_state):
    if has_state:
        (a_ref, b_ref, g_ref, st_ref, w_ref, p_ref, o_ref, so_ref, xp_ref) = refs
    else:
        (a_ref, b_ref, g_ref, w_ref, p_ref, o_ref, so_ref, xp_ref) = refs
        st_ref = None
    t = pl.program_id(1)
    last = pl.num_programs(1) - 1
    lo = CONV_HALO - (CONV_K - 1)
    n = tile + CONV_HALO
    rb = min(CONV_ROW_BLOCK, tile)
    w = w_ref[...]
    conv_b = p_ref[0:1, :]
    ln_g = p_ref[1:2, :]
    ln_b = p_ref[2:3, :]
    for i in range(nb):
        @pl.when(t == 0)
        def _():
            xp_ref[i, 0, 0:CONV_HALO, :] = jnp.zeros((CONV_HALO, C_CONV), F32)
            if has_state:
                xp_ref[i, 0, lo:CONV_HALO, :] = st_ref[:, i, :]

        xp_ref[i, 0, CONV_HALO:n, :] = a_ref[i] * _sigmoid(b_ref[i])
        xp = xp_ref[i, 0]
        for s in range(1, SUBLANES):
            xp_ref[i, s] = pltpu.roll(xp, n - s, axis=0)

        for r in range(tile // rb):
            acc = None
            for j in range(CONV_K):
                s = (lo + j) % SUBLANES
                start = lo + j - s + r * rb
                term = xp_ref[i, s, start:start + rb, :] * w[j:j + 1, :]
                acc = term if acc is None else acc + term
            hc = acc + conv_b
            mu = jnp.mean(hc, axis=-1, keepdims=True)
            xc = hc - mu
            var = jnp.mean(xc * xc, axis=-1, keepdims=True)
            hn = xc * lax.rsqrt(var + LN_EPS) * ln_g + ln_b
            rows = slice(r * rb, (r + 1) * rb)
            o_ref[i, rows, :] = (_silu(hn) * _silu(g_ref[i, rows, :])).astype(o_ref.dtype)

        @pl.when(t == last)
        def _():
            so_ref[i] = xp_ref[i, 0, tile + lo:n, :]

        xp_ref[i, 0, 0:CONV_HALO, :] = xp_ref[i, 0, tile:n, :]


def _conv_group(proj3, state, layer, conv_w, conv_p, tile, nb):
    b, l, _ = proj3.shape
    has_state = state is not None
    cb = COL_BLOCK
    nrow = CONV_K - 1
    in_specs = [
        pl.BlockSpec((nb, tile, cb), lambda i, t: (i, t, GLUA_BLK)),
        pl.BlockSpec((nb, tile, cb), lambda i, t: (i, t, GLUB_BLK)),
        pl.BlockSpec((nb, tile, cb), lambda i, t: (i, t, CGATE_BLK)),
    ]
    args = [proj3, proj3, proj3]
    if has_state:
        in_specs.append(pl.BlockSpec((None, nrow, nb, C_CONV), lambda i, t: (layer, 0, i, 0)))
        args.append(jnp.swapaxes(state, 1, 2))
    in_specs += [pl.BlockSpec((CONV_K, C_CONV), lambda i, t: (0, 0)),
                 pl.BlockSpec((8, C_CONV), lambda i, t: (0, 0))]
    args += [conv_w, conv_p]
    return pl.pallas_call(
        functools.partial(_conv_kernel, tile=tile, nb=nb, has_state=has_state),
        grid=(b // nb, l // tile),
        in_specs=in_specs,
        out_specs=[pl.BlockSpec((nb, tile, C_CONV), lambda i, t: (i, t, 0)),
                   pl.BlockSpec((nb, nrow, C_CONV), lambda i, t: (i, 0, 0))],
        out_shape=[jax.ShapeDtypeStruct((b, l, C_CONV), BF16),
                   jax.ShapeDtypeStruct((b, nrow, C_CONV), F32)],
        scratch_shapes=[pltpu.VMEM((nb, SUBLANES, tile + CONV_HALO, C_CONV), F32)],
        compiler_params=_cparams(("arbitrary", "arbitrary")),
        name="conv_group",
    )(*args)


def _bdot(a, b, dims=None):
    a = a.astype(BF16)
    b = b.astype(BF16)
    if dims is None:
        return jnp.dot(a, b, preferred_element_type=F32)
    return lax.dot_general(a, b, (dims, ((), ())), preferred_element_type=F32)


_NT = ((1,), (1,))
_TN = ((0,), (0,))


def _split_bf16(a):
    hi = a.astype(BF16)
    return hi, (a - hi.astype(F32)).astype(BF16)


def _unit_lower_inverses(ms, c):
    row = lax.broadcasted_iota(jnp.int32, (c, c), 0)
    col = lax.broadcasted_iota(jnp.int32, (c, c), 1)
    eye = (row == col).astype(F32)
    ps = [-m for m in ms]
    ts = [eye + p for p in ps]
    span = 2
    while span < c:
        ps = [_bdot(p, p) for p in ps]
        ts = [t + _bdot(t, p) for t, p in zip(ts, ps)]
        span *= 2
    msp = [_split_bf16(m) for m in ms]
    tsp = [_split_bf16(t) for t in ts]
    mts = [_bdot(mh, th) + _bdot(mh, tl) + _bdot(ml, th) for (mh, ml), (th, tl) in zip(msp, tsp)]
    rs = [eye - t - mt for t, mt in zip(ts, mts)]
    return [t + _bdot(th, r) for t, (th, _), r in zip(ts, tsp, rs)]


def _dn_kernel(qkv_ref, gate_ref, bdc_ref, bdr_ref, prev_ref, s0_ref, w_ref, pr_ref, pc_ref, nw_ref,
               o_ref, st_ref, s_ref, xp_ref, xs_ref, osc_ref, *, tile, chunk, nb):
    c = chunk
    nchunk = tile // c
    hd = DN_HEAD_DIM
    t = pl.program_id(1)
    last = pl.num_programs(1) - 1
    lo = QKV_HALO - (DN_CONV_K - 1)
    hi = lax.Precision.HIGHEST
    w = w_ref[...]

    for i in range(nb):
        @pl.when(t == 0)
        def _():
            xp_ref[i, 0:QKV_HALO, :] = jnp.zeros((QKV_HALO, 3 * DN_WIDTH), F32)
            xp_ref[i, lo:QKV_HALO, :] = prev_ref[:, i, :]
            s_ref[i] = s0_ref[i]

        xp_ref[i, QKV_HALO:QKV_HALO + tile, :] = qkv_ref[i]
        xp = xp_ref[i]
        for j in range(DN_CONV_K - 1):
            xs_ref[i, j] = pltpu.roll(xp, tile + QKV_HALO - (lo + j), axis=0)

        @pl.when(t == last)
        def _():
            st_ref[i] = xp_ref[i, tile + lo:tile + QKV_HALO, :]

    neg_a_row = -jnp.exp(pr_ref[0:1, :])
    dtb_row = pr_ref[1:2, :]
    neg_a_col = -jnp.exp(pc_ref[:, 0:1])
    dtb_col = pc_ref[:, 1:2]
    norm_w = nw_ref[...]

    row = lax.broadcasted_iota(jnp.int32, (c, c), 0)
    col = lax.broadcasted_iota(jnp.int32, (c, c), 1)
    incl = row >= col
    strict = row > col
    tri_l = incl.astype(F32)
    tri_u = (row <= col).astype(F32)

    seq_chunks = [(i, n) for i in range(nb) for n in range(nchunk)]
    probs = [(i, n, h) for i, n in seq_chunks for h in range(DN_HEADS)]
    beta, gcum_col, gcum_row, prep = {}, {}, {}, {}
    for i, n in seq_chunks:
        rows = slice(n * c, (n + 1) * c)
        bd = bdc_ref[i, rows, :]
        beta[i, n] = _sigmoid(bd[:, 0:DN_HEADS])
        g_col = neg_a_row * _softplus(bd[:, DN_HEADS:2 * DN_HEADS] + dtb_row)
        gcum_col[i, n] = jnp.dot(tri_l, g_col, precision=hi, preferred_element_type=F32)
        br = bdr_ref[i, n]
        g_row = neg_a_col * _softplus(br[DN_HEADS:2 * DN_HEADS, :] + dtb_col)
        gcum_row[i, n] = jnp.dot(g_row, tri_u, precision=hi, preferred_element_type=F32)
        acc = xp_ref[i, QKV_HALO + n * c:QKV_HALO + (n + 1) * c, :] * w[DN_CONV_K - 1:DN_CONV_K, :]
        for j in range(DN_CONV_K - 1):
            acc = acc + xs_ref[i, j, rows, :] * w[j:j + 1, :]
        qkvc = _silu(acc)
        for h in range(DN_HEADS):
            q, k, v = (qkvc[:, part * DN_WIDTH + h * hd:part * DN_WIDTH + (h + 1) * hd] for part in range(3))
            q = q * (lax.rsqrt(jnp.sum(q * q, axis=-1, keepdims=True) + NORM_EPS) * (hd ** -0.5))
            k = k * lax.rsqrt(jnp.sum(k * k, axis=-1, keepdims=True) + NORM_EPS)
            gc = gcum_col[i, n][:, h:h + 1]
            gr = gcum_row[i, n][h:h + 1, :]
            bt = beta[i, n][:, h:h + 1]
            gl = gcum_col[i, n][c - 1:c, h:h + 1]
            e = jnp.exp(gc)
            kb = k * bt
            prep[i, n, h] = (
                jnp.concatenate([kb, q], axis=0).astype(BF16),
                k.astype(BF16),
                jnp.exp(jnp.where(incl, gc - gr, -jnp.inf)),
                jnp.concatenate([v * bt, kb * e], axis=1).astype(BF16),
                q * e,
                (k * jnp.exp(gl - gc)).astype(BF16),
                jnp.exp(gl))

    preps = [prep[p] for p in probs]
    kqs = [_bdot(p[0], p[1], _NT) for p in preps]
    aqks = [(kq[c:] * p[2]).astype(BF16) for kq, p in zip(kqs, preps)]
    ms = [jnp.where(strict, kq[:c] * p[2], 0.0) for kq, p in zip(kqs, preps)]
    tinvs = _unit_lower_inverses(ms, c)
    uws = [_bdot(tinv, p[3]) for tinv, p in zip(tinvs, preps)]
    wqs = [jnp.concatenate([uw[:, hd:], p[4]], axis=0).astype(BF16) for uw, p in zip(uws, preps)]
    local = dict(zip(probs, zip(uws, wqs, aqks, [p[5] for p in preps], [p[6] for p in preps])))

    heads = [(i, h) for i in range(nb) for h in range(DN_HEADS)]
    state = {(i, h): s_ref[i, h] for i, h in heads}
    for n in range(nchunk):
        rows = slice(n * c, (n + 1) * c)
        items = [(i, h) + local[i, n, h] for i, h in heads]
        s16s = [state[i, h].astype(BF16) for i, h in heads]
        wss = [_bdot(it[3], s16) for it, s16 in zip(items, s16s)]
        u16s = [(it[2][:, :hd] - ws[:c]).astype(BF16) for it, ws in zip(items, wss)]
        outs = [ws[c:] + _bdot(it[4], u16) for it, ws, u16 in zip(items, wss, u16s)]
        for it, u16 in zip(items, u16s):
            i, h = it[0], it[1]
            state[i, h] = state[i, h] * it[6] + _bdot(it[5], u16, _TN)
        for it, o in zip(items, outs):
            i, h = it[0], it[1]
            o = o * lax.rsqrt(jnp.mean(o * o, axis=-1, keepdims=True) + NORM_EPS) * norm_w
            osc_ref[i, rows, h * hd:(h + 1) * hd] = o
    for i, h in heads:
        s_ref[i, h] = state[i, h]

    for i in range(nb):
        xp_ref[i, 0:QKV_HALO, :] = xp_ref[i, tile:tile + QKV_HALO, :]
        o_ref[i] = (osc_ref[i] * _silu(gate_ref[i])).astype(o_ref.dtype)


def _dn_group(proj3, bdc3, bdr4, prev, s0, layer, qkv_w, p_row, p_col, norm_w, tile, chunk, nb):
    b, l, _ = proj3.shape
    nchunk = tile // chunk
    hd = DN_HEAD_DIM
    nrow = DN_CONV_K - 1
    const2 = lambda i, t: (0, 0)
    od, qkv_state, s_final = pl.pallas_call(
        functools.partial(_dn_kernel, tile=tile, chunk=chunk, nb=nb),
        grid=(b // nb, l // tile),
        in_specs=[
            pl.BlockSpec((nb, tile, 3 * DN_WIDTH), lambda i, t: (i, t, QKV_BLK)),
            pl.BlockSpec((nb, tile, DN_WIDTH), lambda i, t: (i, t, DNGATE_BLK)),
            pl.BlockSpec((nb, tile, SMALL_PAD), lambda i, t: (i, t, 0)),
            pl.BlockSpec((nb, nchunk, 2 * DN_HEADS, chunk), lambda i, t: (i, t, 0, 0)),
            pl.BlockSpec((None, nrow, nb, 3 * DN_WIDTH), lambda i, t: (layer, 0, i, 0)),
            pl.BlockSpec((None, nb, DN_HEADS, hd, hd), lambda i, t: (layer, i, 0, 0, 0)),
            pl.BlockSpec((DN_CONV_K, 3 * DN_WIDTH), const2),
            pl.BlockSpec((2, DN_HEADS), const2),
            pl.BlockSpec((DN_HEADS, 2), const2),
            pl.BlockSpec((1, hd), const2),
        ],
        out_specs=[pl.BlockSpec((nb, tile, DN_WIDTH), lambda i, t: (i, t, 0)),
                   pl.BlockSpec((nb, nrow, 3 * DN_WIDTH), lambda i, t: (i, 0, 0)),
                   pl.BlockSpec((nb, DN_HEADS, hd, hd), lambda i, t: (i, 0, 0, 0))],
        out_shape=[jax.ShapeDtypeStruct((b, l, DN_WIDTH), BF16),
                   jax.ShapeDtypeStruct((b, nrow, 3 * DN_WIDTH), F32),
                   jax.ShapeDtypeStruct((b, DN_HEADS, hd, hd), F32)],
        scratch_shapes=[pltpu.VMEM((nb, tile + QKV_HALO, 3 * DN_WIDTH), F32),
                        pltpu.VMEM((nb, DN_CONV_K - 1, tile + QKV_HALO, 3 * DN_WIDTH), F32),
                        pltpu.VMEM((nb, tile, DN_WIDTH), F32)],
        compiler_params=_cparams(("arbitrary", "arbitrary")),
        name="deltanet_group",
    )(proj3, proj3, bdc3, bdr4, jnp.swapaxes(prev, 1, 2), s0, qkv_w, p_row, p_col, norm_w)
    return od, qkv_state, s_final


def _dn_pipelined_kernel(q0_ref, qa_ref, qb_ref, c0_ref, ca_ref, cb_ref, r0_ref, ra_ref, rb_ref, gate_ref,
                         w_ref, pr_ref, pc_ref, nw_ref, o_ref, st_ref, s_ref,
                         xp_ref, kq_ref, k16_ref, uwr_ref, qe_ref, kd_ref, gcc_ref, gcr_ref, osc_ref,
                         *, tile, chunk, nt, n_tiles):
    c = chunk
    nchunk = tile // c
    hd = DN_HEAD_DIM
    lo = QKV_HALO - (DN_CONV_K - 1)
    hi = lax.Precision.HIGHEST
    k = pl.program_id(0)
    w = w_ref[...]
    neg_a_row = -jnp.exp(pr_ref[0:1, :])
    dtb_row = pr_ref[1:2, :]
    neg_a_col = -jnp.exp(pc_ref[:, 0:1])
    dtb_col = pc_ref[:, 1:2]
    norm_w = nw_ref[...]
    row = lax.broadcasted_iota(jnp.int32, (c, c), 0)
    col = lax.broadcasted_iota(jnp.int32, (c, c), 1)
    incl = row >= col
    strict = row > col
    tri_l = incl.astype(F32)
    tri_u = (row <= col).astype(F32)
    probs = [(n, h) for n in range(nchunk) for h in range(DN_HEADS)]

    def prepare(buf, q_ref, c_ref, r_ref, first, write_state):
        halo = xp_ref[0:QKV_HALO, :]
        xp_ref[0:QKV_HALO, :] = jnp.zeros_like(halo) if first is True else jnp.where(first, 0.0, halo)
        xp_ref[QKV_HALO:QKV_HALO + tile, :] = q_ref[0]
        if write_state:
            st_ref[0] = xp_ref[tile + lo:tile + QKV_HALO, :]
        for n in range(nchunk):
            rows = slice(n * c, (n + 1) * c)
            bd = c_ref[0, rows, :]
            beta = _sigmoid(bd[:, 0:DN_HEADS])
            g_col = neg_a_row * _softplus(bd[:, DN_HEADS:2 * DN_HEADS] + dtb_row)
            gcum_col = jnp.dot(tri_l, g_col, precision=hi, preferred_element_type=F32)
            br = r_ref[0, n]
            g_row = neg_a_col * _softplus(br[DN_HEADS:2 * DN_HEADS, :] + dtb_col)
            gcc_ref[buf, n] = gcum_col
            gcr_ref[buf, n] = jnp.dot(g_row, tri_u, precision=hi, preferred_element_type=F32)
            window = xp_ref[n * c:(n + 1) * c + QKV_HALO, :]
            acc = window[QKV_HALO:, :] * w[DN_CONV_K - 1:DN_CONV_K, :]
            for j in range(DN_CONV_K - 1):
                acc = acc + pltpu.roll(window, c + QKV_HALO - (lo + j), axis=0)[0:c, :] * w[j:j + 1, :]
            qkvc = _silu(acc)
            for h in range(DN_HEADS):
                p = n * DN_HEADS + h
                q, kk, v = (qkvc[:, part * DN_WIDTH + h * hd:part * DN_WIDTH + (h + 1) * hd] for part in range(3))
                q = q * (lax.rsqrt(jnp.sum(q * q, axis=-1, keepdims=True) + NORM_EPS) * (hd ** -0.5))
                kk = kk * lax.rsqrt(jnp.sum(kk * kk, axis=-1, keepdims=True) + NORM_EPS)
                gc = gcum_col[:, h:h + 1]
                bt = beta[:, h:h + 1]
                gl = gcum_col[c - 1:c, h:h + 1]
                e = jnp.exp(gc)
                kb = kk * bt
                kq_ref[buf, p] = jnp.concatenate([kb, q], axis=0).astype(BF16)
                k16_ref[buf, p] = kk.astype(BF16)
                uwr_ref[buf, p] = jnp.concatenate([v * bt, kb * e], axis=1).astype(BF16)
                qe_ref[buf, p] = q * e
                kd_ref[buf, p] = (kk * jnp.exp(gl - gc)).astype(BF16)
        xp_ref[0:QKV_HALO, :] = xp_ref[tile:tile + QKV_HALO, :]

    def finish(buf, half):
        idx = [n * DN_HEADS + h for n, h in probs]
        kqs = [_bdot(kq_ref[buf, p], k16_ref[buf, p], _NT) for p in idx]
        gcs = [gcc_ref[buf, n][:, h:h + 1] for n, h in probs]
        decays = [jnp.exp(jnp.where(incl, gc - gcr_ref[buf, n][h:h + 1, :], -jnp.inf))
                  for gc, (n, h) in zip(gcs, probs)]
        aqks = [(kq[c:] * dc).astype(BF16) for kq, dc in zip(kqs, decays)]
        ms = [jnp.where(strict, kq[:c] * dc, 0.0) for kq, dc in zip(kqs, decays)]
        tinvs = _unit_lower_inverses(ms, c)
        uws = [_bdot(tinv, uwr_ref[buf, p]) for tinv, p in zip(tinvs, idx)]
        wqs = [jnp.concatenate([uw[:, hd:], qe_ref[buf, p]], axis=0).astype(BF16) for uw, p in zip(uws, idx)]
        egls = [jnp.exp(gc[c - 1:c, :]) for gc in gcs]
        local = dict(zip(probs, zip(uws, wqs, aqks, egls)))
        state = [s_ref[0, h] for h in range(DN_HEADS)]
        for n in range(nchunk):
            rows = slice(n * c, (n + 1) * c)
            items = [local[n, h] for h in range(DN_HEADS)]
            s16s = [s.astype(BF16) for s in state]
            wss = [_bdot(it[1], s16) for it, s16 in zip(items, s16s)]
            u16s = [(it[0][:, :hd] - ws[:c]).astype(BF16) for it, ws in zip(items, wss)]
            outs = [ws[c:] + _bdot(it[2], u16) for it, ws, u16 in zip(items, wss, u16s)]
            state = [s * it[3] + _bdot(kd_ref[buf, n * DN_HEADS + h], u16, _TN)
                     for h, (s, it, u16) in enumerate(zip(state, items, u16s))]
            for h, o in enumerate(outs):
                o = o * lax.rsqrt(jnp.mean(o * o, axis=-1, keepdims=True) + NORM_EPS) * norm_w
                osc_ref[rows, h * hd:(h + 1) * hd] = o
        for h in range(DN_HEADS):
            s_ref[0, h] = state[h]
        rows_out = slice(half * tile, (half + 1) * tile)
        o_ref[0, rows_out, :] = (osc_ref[...] * _silu(gate_ref[0, rows_out, :])).astype(o_ref.dtype)

    @pl.when(k == 0)
    def _():
        prepare(0, q0_ref, c0_ref, r0_ref, True, False)

    @pl.when(lax.rem(2 * k, nt) == 0)
    def _():
        s_ref[0] = jnp.zeros((DN_HEADS, hd, hd), F32)

    prepare(1, qa_ref, ca_ref, ra_ref, False, True)
    finish(0, 0)
    prepare(0, qb_ref, cb_ref, rb_ref, lax.rem(jnp.minimum(2 * k + 2, n_tiles - 1), nt) == 0, False)
    finish(1, 1)


def _dn_group_pipelined(proj3, bdc3, bdr4, qkv_w, p_row, p_col, norm_w, tile, chunk):
    b, l, _ = proj3.shape
    nt = l // tile
    assert nt % 2 == 0
    n_tiles = b * nt
    nchunk = tile // chunk
    hd = DN_HEAD_DIM
    n_prob = nchunk * DN_HEADS
    proj_t = proj3.reshape(n_tiles, tile, N_MAIN)
    proj_p = proj3.reshape(n_tiles // 2, 2 * tile, N_MAIN)
    bdc_t = bdc3.reshape(n_tiles, tile, SMALL_PAD)
    bdr_t = bdr4.reshape(n_tiles, nchunk, 2 * DN_HEADS, chunk)
    odd = lambda k: 2 * k + 1
    nxt = lambda k: jnp.minimum(2 * k + 2, n_tiles - 1)
    once = pl.Buffered(1)

    def tile_specs(index):
        mode = once if index is None else None
        at = (lambda k: 0) if index is None else index
        return [pl.BlockSpec((1, tile, 3 * DN_WIDTH), lambda k: (at(k), 0, QKV_BLK), pipeline_mode=mode),
                pl.BlockSpec((1, tile, SMALL_PAD), lambda k: (at(k), 0, 0), pipeline_mode=mode),
                pl.BlockSpec((1, nchunk, 2 * DN_HEADS, chunk), lambda k: (at(k), 0, 0, 0), pipeline_mode=mode)]

    (q0, c0, r0), (qa, ca, ra), (qb, cb, rb) = tile_specs(None), tile_specs(odd), tile_specs(nxt)
    const2 = lambda k: (0, 0)
    od, qkv_state, s_final = pl.pallas_call(
        functools.partial(_dn_pipelined_kernel, tile=tile, chunk=chunk, nt=nt, n_tiles=n_tiles),
        grid=(n_tiles // 2,),
        in_specs=[q0, qa, qb, c0, ca, cb, r0, ra, rb,
                  pl.BlockSpec((1, 2 * tile, DN_WIDTH), lambda k: (k, 0, DNGATE_BLK)),
                  pl.BlockSpec((DN_CONV_K, 3 * DN_WIDTH), const2),
                  pl.BlockSpec((2, DN_HEADS), const2),
                  pl.BlockSpec((DN_HEADS, 2), const2),
                  pl.BlockSpec((1, hd), const2)],
        out_specs=[pl.BlockSpec((1, 2 * tile, DN_WIDTH), lambda k: (k, 0, 0)),
                   pl.BlockSpec((1, DN_CONV_K - 1, 3 * DN_WIDTH), lambda k: (2 * k // nt, 0, 0)),
                   pl.BlockSpec((1, DN_HEADS, hd, hd), lambda k: (2 * k // nt, 0, 0, 0))],
        out_shape=[jax.ShapeDtypeStruct((n_tiles // 2, 2 * tile, DN_WIDTH), BF16),
                   jax.ShapeDtypeStruct((b, DN_CONV_K - 1, 3 * DN_WIDTH), F32),
                   jax.ShapeDtypeStruct((b, DN_HEADS, hd, hd), F32)],
        scratch_shapes=[pltpu.VMEM((tile + QKV_HALO, 3 * DN_WIDTH), F32),
                        pltpu.VMEM((2, n_prob, 2 * chunk, hd), BF16),
                        pltpu.VMEM((2, n_prob, chunk, hd), BF16),
                        pltpu.VMEM((2, n_prob, chunk, 2 * hd), BF16),
                        pltpu.VMEM((2, n_prob, chunk, hd), F32),
                        pltpu.VMEM((2, n_prob, chunk, hd), BF16),
                        pltpu.VMEM((2, nchunk, chunk, DN_HEADS), F32),
                        pltpu.VMEM((2, nchunk, DN_HEADS, chunk), F32),
                        pltpu.VMEM((tile, DN_WIDTH), F32)],
        compiler_params=_cparams(("arbitrary",)),
        name="deltanet_pipelined",
    )(proj_t, proj_t, proj_t, bdc_t, bdc_t, bdc_t, bdr_t, bdr_t, bdr_t, proj_p, qkv_w, p_row, p_col, norm_w)
    return od.reshape(b, l, DN_WIDTH), qkv_state, s_final


def _mem_kernel(q_ref, g_ref, k_ref, v_ref, o_ref, *, nb):
    hd = MEM_HEAD_DIM
    scale = hd ** -0.5
    probs = [(i, h) for i in range(nb) for h in range(MEM_HEADS)]
    lanes = [slice(h * hd, (h + 1) * hd) for h in range(MEM_HEADS)]
    qs = [q_ref[i, :, lanes[h]] for i, h in probs]
    ks = [k_ref[i, pl.ds(h, MEM_TOKENS, stride=MEM_HEADS), :] for i, h in probs]
    vs = [v_ref[i, pl.ds(h, MEM_TOKENS, stride=MEM_HEADS), :] for i, h in probs]
    ss = [_bdot(q, k, _NT) * scale for q, k in zip(qs, ks)]
    es = [jnp.exp(s - jnp.max(s, axis=-1, keepdims=True)) for s in ss]
    oms = [_bdot(e, v) / jnp.sum(e, axis=-1, keepdims=True) for e, v in zip(es, vs)]
    for (i, h), om in zip(probs, oms):
        o_ref[i, :, lanes[h]] = (om * _silu(g_ref[i, :, lanes[h]])).astype(o_ref.dtype)


def _mem_group(proj3, mem_k, mem_v, tile, nb):
    b, l, _ = proj3.shape
    cb = COL_BLOCK
    rows = MEM_TOKENS * MEM_HEADS
    return pl.pallas_call(
        functools.partial(_mem_kernel, nb=nb),
        grid=(b // nb, l // tile),
        in_specs=[pl.BlockSpec((nb, tile, cb), lambda i, t: (i, t, MEMQ_BLK)),
                  pl.BlockSpec((nb, tile, cb), lambda i, t: (i, t, MEMG_BLK)),
                  pl.BlockSpec((nb, rows, MEM_HEAD_DIM), lambda i, t: (i, 0, 0)),
                  pl.BlockSpec((nb, rows, MEM_HEAD_DIM), lambda i, t: (i, 0, 0))],
        out_specs=pl.BlockSpec((nb, tile, MEM_WIDTH), lambda i, t: (i, t, 0)),
        out_shape=jax.ShapeDtypeStruct((b, l, MEM_WIDTH), BF16),
        compiler_params=_cparams(("arbitrary", "arbitrary")),
        name="memory_group",
    )(proj3, proj3, mem_k, mem_v)


def _out_kernel(oc_ref, od_ref, om_ref, x_ref, w_ref, p_ref, y_ref):
    half = x_ref.shape[0] // 2
    for r in range(2):
        rows = slice(r * half, (r + 1) * half)
        h = jnp.dot(oc_ref[rows, :], w_ref[0:C_CONV, :], preferred_element_type=F32)
        h = h + jnp.dot(od_ref[rows, :], w_ref[C_CONV:C_CONV + DN_WIDTH, :], preferred_element_type=F32)
        h = h + jnp.dot(om_ref[rows, :], w_ref[C_CONV + DN_WIDTH:, :], preferred_element_type=F32)
        z = DEEPNORM_ALPHA * x_ref[rows, :] + h
        mu = jnp.mean(z, axis=-1, keepdims=True)
        zc = z - mu
        var = jnp.mean(zc * zc, axis=-1, keepdims=True)
        y_ref[rows, :] = zc * lax.rsqrt(var + LN_EPS) * p_ref[0:1, :] + p_ref[1:2, :]


def _out_proj(oc, od, om, x2, w_out, ln_p, bm):
    m = x2.shape[0]
    return pl.pallas_call(
        _out_kernel,
        grid=(m // bm,),
        in_specs=[pl.BlockSpec((bm, C_CONV), lambda i: (i, 0)),
                  pl.BlockSpec((bm, DN_WIDTH), lambda i: (i, 0)),
                  pl.BlockSpec((bm, MEM_WIDTH), lambda i: (i, 0)),
                  pl.BlockSpec((bm, D_MODEL), lambda i: (i, 0)),
                  pl.BlockSpec((D_MODEL, D_MODEL), lambda i: (0, 0)),
                  pl.BlockSpec((8, D_MODEL), lambda i: (0, 0))],
        out_specs=pl.BlockSpec((bm, D_MODEL), lambda i: (i, 0)),
        out_shape=jax.ShapeDtypeStruct((m, D_MODEL), F32),
        compiler_params=_cparams(("arbitrary",)),
        name="out_proj_ln",
    )(oc, od, om, x2, w_out, ln_p)


def _pad_rows(rows, n):
    a = jnp.stack(rows).astype(F32)
    return jnp.pad(a, ((0, n - a.shape[0]), (0, 0)))


def _layer(x, conv_prev, qkv_prev, s0, mem_k, mem_v, wts, *, bm, conv_tile, conv_nb, dn_tile, dn_chunk, dn_nb,
           mem_tile, mem_nb, out_bm):
    (w_in_t, layer, conv_w, conv_p, qkv_w, p_row, p_col, norm_w, w_out, ln_p) = wts
    b, l, _ = x.shape
    x2 = x.reshape(b * l, D_MODEL)
    proj, bdc, bdr = _in_proj(x2, w_in_t, layer, bm)
    proj3 = proj.reshape(b, l, N_MAIN)
    bdc3 = bdc.reshape(b, l, SMALL_PAD)
    bdr4 = bdr.reshape(2 * DN_HEADS, b, l // dn_chunk, dn_chunk).transpose(1, 2, 0, 3)

    oc, conv_state = _conv_group(proj3, conv_prev, layer, conv_w, conv_p, conv_tile, conv_nb)
    if s0 is None:
        od, qkv_state, s_final = _dn_group_pipelined(proj3, bdc3, bdr4, qkv_w, p_row, p_col, norm_w,
                                                     dn_tile, dn_chunk)
    else:
        od, qkv_state, s_final = _dn_group(proj3, bdc3, bdr4, qkv_prev, s0, layer, qkv_w, p_row, p_col, norm_w,
                                           dn_tile, dn_chunk, dn_nb)
    om = _mem_group(proj3, mem_k, mem_v, mem_tile, mem_nb)
    y = _out_proj(oc.reshape(b * l, C_CONV), od.reshape(b * l, DN_WIDTH), om.reshape(b * l, MEM_WIDTH),
                  x2, w_out, ln_p, out_bm)
    return y.reshape(b, l, D_MODEL), conv_state, qkv_state, s_final


def kernel(x_prompt, x_sample, mem_prompt, state_conv, state_qkv_conv, state_delta, cache_mem_k, cache_mem_v,
           w_in, conv_w, conv_b, conv_ln_g, conv_ln_b, qkv_conv_w, a_log, dt_bias, delta_norm_w,
           w_mem_k, w_mem_v, w_out, ln_g, ln_b):
    bp = x_prompt.shape[0]
    bs = x_sample.shape[0]
    hp, hs = x_prompt, x_sample
    outs = [[] for _ in range(8)]
    w_in_t = jnp.swapaxes(w_in, 1, 2)
    for i in range(DEPTH):
        conv_p = _pad_rows([conv_b[i], conv_ln_g[i], conv_ln_b[i]], 8)
        p_row = jnp.stack([a_log[i], dt_bias[i]]).astype(F32)
        wts = (w_in_t, i, conv_w[i], conv_p, qkv_conv_w[i], p_row, p_row.T,
               delta_norm_w[i].reshape(1, DN_HEAD_DIM), w_out[i].astype(BF16),
               _pad_rows([ln_g[i], ln_b[i]], 8))

        w_kv = jnp.concatenate([w_mem_k[i], w_mem_v[i]], axis=1).astype(BF16)
        kv = _matmul(mem_prompt.reshape(bp * MEM_TOKENS, D_MODEL), w_kv, 512, 512)
        mk = kv[:, :MEM_WIDTH].reshape(bp, MEM_TOKENS, MEM_HEADS, MEM_HEAD_DIM)
        mv = kv[:, MEM_WIDTH:].reshape(bp, MEM_TOKENS, MEM_HEADS, MEM_HEAD_DIM)
        kv_rows = MEM_TOKENS * MEM_HEADS
        hp, c_st, q_st, s_st = _layer(
            hp, None, None, None, mk.reshape(bp, kv_rows, MEM_HEAD_DIM), mv.reshape(bp, kv_rows, MEM_HEAD_DIM),
            wts, bm=2048, conv_tile=512, conv_nb=1, dn_tile=256, dn_chunk=64, dn_nb=1, mem_tile=512, mem_nb=1,
            out_bm=512)
        outs[0].append(c_st); outs[1].append(q_st); outs[2].append(s_st)
        outs[3].append(mk)
        outs[4].append(mv)

        ls = hs.shape[1]
        hs, c_st, q_st, s_st = _layer(
            hs, state_conv, state_qkv_conv, state_delta,
            cache_mem_k[i].reshape(bs, kv_rows, MEM_HEAD_DIM), cache_mem_v[i].reshape(bs, kv_rows, MEM_HEAD_DIM),
            wts, bm=bs * ls, conv_tile=ls, conv_nb=8, dn_tile=ls, dn_chunk=ls, dn_nb=8, mem_tile=ls, mem_nb=8,
            out_bm=512)
        outs[5].append(c_st); outs[6].append(q_st); outs[7].append(s_st)
    return (hp, hs) + tuple(jnp.stack(o) for o in outs)
```

```python
import functools

import jax
import jax.numpy as jnp
from jax import lax
from jax.experimental import pallas as pl
from jax.experimental.pallas import tpu as pltpu

F32 = jnp.float32
BF16 = jnp.bfloat16

D_MODEL = 2048
C_CONV = 512
CONV_K = 31
DN_HEADS = 8
DN_HEAD_DIM = 128
DN_WIDTH = DN_HEADS * DN_HEAD_DIM
DN_CONV_K = 4
MEM_HEADS = 4
MEM_HEAD_DIM = 128
MEM_WIDTH = MEM_HEADS * MEM_HEAD_DIM
MEM_TOKENS = 256
DEPTH = 1
DEEPNORM_ALPHA = (2 * DEPTH) ** 0.25
LN_EPS = 1e-5
NORM_EPS = 1e-6

N_MAIN = 3 * DN_WIDTH + DN_WIDTH + 3 * C_CONV + 2 * MEM_WIDTH
COL_BLOCK = 512
QKV_BLK = 0
DNGATE_BLK = 3
GLUA_BLK, GLUB_BLK, CGATE_BLK, MEMQ_BLK, MEMG_BLK = 8, 9, 10, 11, 12
SMALL_PAD = 128
N_DIRECT_BLOCKS = (3 * C_CONV + 4 * DN_WIDTH) // COL_BLOCK
CONV_HALO = 32
QKV_HALO = 8
SUBLANES = 8
CONV_ROW_BLOCK = 64

VMEM_LIMIT = 56 * 1024 * 1024


def _sigmoid(x):
    return 1.0 / (1.0 + jnp.exp(-x))


def _silu(x):
    return x * _sigmoid(x)


def _softplus(x):
    return jnp.maximum(x, 0.0) + jnp.log(1.0 + jnp.exp(-jnp.abs(x)))


def _cparams(sem):
    return pltpu.CompilerParams(dimension_semantics=sem, vmem_limit_bytes=VMEM_LIMIT)


def _proj_kernel(x_hbm, wa_ref, wb_ref, o_ref, bdc_ref, bdr_ref, xf_ref, xb_ref, sem):
    i = pl.program_id(0)
    j = pl.program_id(1)
    bm = xf_ref.shape[0]
    nt = (((1,), (1,)), ((), ()))
    n_bd = 2 * DN_HEADS

    def x_copy(block):
        return pltpu.make_async_copy(x_hbm.at[pl.ds(block * bm, bm), :], xf_ref, sem)

    @pl.when(j == 0)
    def _():
        @pl.when(i == 0)
        def _():
            x_copy(0).start()

        x_copy(i).wait()
        xb_ref[...] = xf_ref[...].astype(BF16)

        @pl.when(i + 1 < pl.num_programs(0))
        def _():
            x_copy(i + 1).start()

    @pl.when(j < N_DIRECT_BLOCKS)
    def _():
        o_ref[...] = lax.dot_general(xb_ref[...], wa_ref[...].astype(BF16), nt, preferred_element_type=F32)

    @pl.when(j >= N_DIRECT_BLOCKS)
    def _():
        w = jnp.concatenate([wa_ref[n_bd:, :], wb_ref[...]], axis=0).astype(BF16)
        o_ref[...] = lax.dot_general(xb_ref[...], w, nt, preferred_element_type=F32)

    @pl.when(j == N_DIRECT_BLOCKS)
    def _():
        bd = lax.dot_general(xb_ref[...], wa_ref[0:SMALL_PAD, :].astype(BF16), nt, preferred_element_type=F32)
        bdc_ref[...] = bd
        bdr_ref[...] = bd.T[0:n_bd, :]


def _proj_out_block(j):
    n_conv = 3 * C_CONV // COL_BLOCK
    return jnp.where(j < n_conv, j + GLUA_BLK, jnp.where(j < N_DIRECT_BLOCKS, j - n_conv, j))


def _in_proj(x2, w_in_t, layer, bm):
    m = x2.shape[0]
    bn = COL_BLOCK
    n_blocks = N_MAIN // bn
    n_bd = 2 * DN_HEADS
    return pl.pallas_call(
        _proj_kernel,
        grid=(m // bm, n_blocks),
        in_specs=[
            pl.BlockSpec(memory_space=pl.ANY),
            pl.BlockSpec((None, bn, D_MODEL), lambda i, j: (layer, j, 0)),
            pl.BlockSpec((None, n_bd, D_MODEL),
                         lambda i, j: (layer, (jnp.maximum(j + 1, n_blocks - 1)) * (bn // n_bd), 0)),
        ],
        out_specs=[
            pl.BlockSpec((bm, bn), lambda i, j: (i, _proj_out_block(j))),
            pl.BlockSpec((bm, SMALL_PAD), lambda i, j: (i, 0)),
            pl.BlockSpec((2 * DN_HEADS, bm), lambda i, j: (0, i)),
        ],
        out_shape=[
            jax.ShapeDtypeStruct((m, N_MAIN), F32),
            jax.ShapeDtypeStruct((m, SMALL_PAD), F32),
            jax.ShapeDtypeStruct((2 * DN_HEADS, m), F32),
        ],
        scratch_shapes=[pltpu.VMEM((bm, D_MODEL), F32), pltpu.VMEM((bm, D_MODEL), BF16),
                        pltpu.SemaphoreType.DMA(())],
        compiler_params=_cparams(("arbitrary", "arbitrary")),
        name="in_proj",
    )(x2, w_in_t, w_in_t)


def _mm_kernel(x_ref, w_ref, o_ref):
    o_ref[...] = jnp.dot(x_ref[...].astype(BF16), w_ref[...], preferred_element_type=F32)


def _matmul(x2, w, bm, bn):
    m, k = x2.shape
    n = w.shape[1]
    return pl.pallas_call(
        _mm_kernel,
        grid=(m // bm, n // bn),
        in_specs=[pl.BlockSpec((bm, k), lambda i, j: (i, 0)),
                  pl.BlockSpec((k, bn), lambda i, j: (0, j))],
        out_specs=pl.BlockSpec((bm, bn), lambda i, j: (i, j)),
        out_shape=jax.ShapeDtypeStruct((m, n), F32),
        compiler_params=_cparams(("arbitrary", "arbitrary")),
        name="mem_kv_proj",
    )(x2, w)


def _conv_kernel(*refs, tile, nb, has_state):
    if has_state:
        (a_ref, b_ref, g_ref, st_ref, w_ref, p_ref, o_ref, so_ref, xp_ref) = refs
    else:
        (a_ref, b_ref, g_ref, w_ref, p_ref, o_ref, so_ref, xp_ref) = refs
        st_ref = None
    t = pl.program_id(1)
    last = pl.num_programs(1) - 1
    lo = CONV_HALO - (CONV_K - 1)
    n = tile + CONV_HALO
    rb = min(CONV_ROW_BLOCK, tile)
    w = w_ref[...]
    conv_b = p_ref[0:1, :]
    ln_g = p_ref[1:2, :]
    ln_b = p_ref[2:3, :]
    for i in range(nb):
        @pl.when(t == 0)
        def _():
            xp_ref[i, 0, 0:CONV_HALO, :] = jnp.zeros((CONV_HALO, C_CONV), F32)
            if has_state:
                xp_ref[i, 0, lo:CONV_HALO, :] = st_ref[:, i, :]

        xp_ref[i, 0, CONV_HALO:n, :] = a_ref[i] * _sigmoid(b_ref[i])
        xp = xp_ref[i, 0]
        for s in range(1, SUBLANES):
            xp_ref[i, s] = pltpu.roll(xp, n - s, axis=0)

        for r in range(tile // rb):
            acc = None
            for j in range(CONV_K):
                s = (lo + j) % SUBLANES
                start = lo + j - s + r * rb
                term = xp_ref[i, s, start:start + rb, :] * w[j:j + 1, :]
                acc = term if acc is None else acc + term
            hc = acc + conv_b
            mu = jnp.mean(hc, axis=-1, keepdims=True)
            xc = hc - mu
            var = jnp.mean(xc * xc, axis=-1, keepdims=True)
            hn = xc * lax.rsqrt(var + LN_EPS) * ln_g + ln_b
            rows = slice(r * rb, (r + 1) * rb)
            o_ref[i, rows, :] = (_silu(hn) * _silu(g_ref[i, rows, :])).astype(o_ref.dtype)

        @pl.when(t == last)
        def _():
            so_ref[i] = xp_ref[i, 0, tile + lo:n, :]

        xp_ref[i, 0, 0:CONV_HALO, :] = xp_ref[i, 0, tile:n, :]


def _conv_group(proj3, state, layer, conv_w, conv_p, tile, nb):
    b, l, _ = proj3.shape
    has_state = state is not None
    cb = COL_BLOCK
    nrow = CONV_K - 1
    in_specs = [
        pl.BlockSpec((nb, tile, cb), lambda i, t: (i, t, GLUA_BLK)),
        pl.BlockSpec((nb, tile, cb), lambda i, t: (i, t, GLUB_BLK)),
        pl.BlockSpec((nb, tile, cb), lambda i, t: (i, t, CGATE_BLK)),
    ]
    args = [proj3, proj3, proj3]
    if has_state:
        in_specs.append(pl.BlockSpec((None, nrow, nb, C_CONV), lambda i, t: (layer, 0, i, 0)))
        args.append(jnp.swapaxes(state, 1, 2))
    in_specs += [pl.BlockSpec((CONV_K, C_CONV), lambda i, t: (0, 0)),
                 pl.BlockSpec((8, C_CONV), lambda i, t: (0, 0))]
    args += [conv_w, conv_p]
    return pl.pallas_call(
        functools.partial(_conv_kernel, tile=tile, nb=nb, has_state=has_state),
        grid=(b // nb, l // tile),
        in_specs=in_specs,
        out_specs=[pl.BlockSpec((nb, tile, C_CONV), lambda i, t: (i, t, 0)),
                   pl.BlockSpec((nb, nrow, C_CONV), lambda i, t: (i, 0, 0))],
        out_shape=[jax.ShapeDtypeStruct((b, l, C_CONV), BF16),
                   jax.ShapeDtypeStruct((b, nrow, C_CONV), F32)],
        scratch_shapes=[pltpu.VMEM((nb, SUBLANES, tile + CONV_HALO, C_CONV), F32)],
        compiler_params=_cparams(("arbitrary", "arbitrary")),
        name="conv_group",
    )(*args)


def _bdot(a, b, dims=None):
    a = a.astype(BF16)
    b = b.astype(BF16)
    if dims is None:
        return jnp.dot(a, b, preferred_element_type=F32)
    return lax.dot_general(a, b, (dims, ((), ())), preferred_element_type=F32)


_NT = ((1,), (1,))
_TN = ((0,), (0,))


def _split_bf16(a):
    hi = a.astype(BF16)
    return hi, (a - hi.astype(F32)).astype(BF16)


def _unit_lower_inverses(ms, c):
    row = lax.broadcasted_iota(jnp.int32, (c, c), 0)
    col = lax.broadcasted_iota(jnp.int32, (c, c), 1)
    eye = (row == col).astype(F32)
    ps = [-m for m in ms]
    ts = [eye + p for p in ps]
    span = 2
    while span < c:
        ps = [_bdot(p, p) for p in ps]
        ts = [t + _bdot(t, p) for t, p in zip(ts, ps)]
        span *= 2
    msp = [_split_bf16(m) for m in ms]
    tsp = [_split_bf16(t) for t in ts]
    mts = [_bdot(mh, th) + _bdot(mh, tl) + _bdot(ml, th) for (mh, ml), (th, tl) in zip(msp, tsp)]
    rs = [eye - t - mt for t, mt in zip(ts, mts)]
    return [t + _bdot(th, r) for t, (th, _), r in zip(ts, tsp, rs)]


def _dn_kernel(qkv_ref, gate_ref, bdc_ref, bdr_ref, prev_ref, s0_ref, w_ref, pr_ref, pc_ref, nw_ref,
               o_ref, st_ref, s_ref, xp_ref, xs_ref, osc_ref, *, tile, chunk, nb):
    c = chunk
    nchunk = tile // c
    hd = DN_HEAD_DIM
    t = pl.program_id(1)
    last = pl.num_programs(1) - 1
    lo = QKV_HALO - (DN_CONV_K - 1)
    hi = lax.Precision.HIGHEST
    w = w_ref[...]

    for i in range(nb):
        @pl.when(t == 0)
        def _():
            xp_ref[i, 0:QKV_HALO, :] = jnp.zeros((QKV_HALO, 3 * DN_WIDTH), F32)
            xp_ref[i, lo:QKV_HALO, :] = prev_ref[:, i, :]
            s_ref[i] = s0_ref[i]

        xp_ref[i, QKV_HALO:QKV_HALO + tile, :] = qkv_ref[i]
        xp = xp_ref[i]
        for j in range(DN_CONV_K - 1):
            xs_ref[i, j] = pltpu.roll(xp, tile + QKV_HALO - (lo + j), axis=0)

        @pl.when(t == last)
        def _():
            st_ref[i] = xp_ref[i, tile + lo:tile + QKV_HALO, :]

    neg_a_row = -jnp.exp(pr_ref[0:1, :])
    dtb_row = pr_ref[1:2, :]
    neg_a_col = -jnp.exp(pc_ref[:, 0:1])
    dtb_col = pc_ref[:, 1:2]
    norm_w = nw_ref[...]

    row = lax.broadcasted_iota(jnp.int32, (c, c), 0)
    col = lax.broadcasted_iota(jnp.int32, (c, c), 1)
    incl = row >= col
    strict = row > col
    tri_l = incl.astype(F32)
    tri_u = (row <= col).astype(F32)

    seq_chunks = [(i, n) for i in range(nb) for n in range(nchunk)]
    probs = [(i, n, h) for i, n in seq_chunks for h in range(DN_HEADS)]
    beta, gcum_col, gcum_row, prep = {}, {}, {}, {}
    for i, n in seq_chunks:
        rows = slice(n * c, (n + 1) * c)
        bd = bdc_ref[i, rows, :]
        beta[i, n] = _sigmoid(bd[:, 0:DN_HEADS])
        g_col = neg_a_row * _softplus(bd[:, DN_HEADS:2 * DN_HEADS] + dtb_row)
        gcum_col[i, n] = jnp.dot(tri_l, g_col, precision=hi, preferred_element_type=F32)
        br = bdr_ref[i, n]
        g_row = neg_a_col * _softplus(br[DN_HEADS:2 * DN_HEADS, :] + dtb_col)
        gcum_row[i, n] = jnp.dot(g_row, tri_u, precision=hi, preferred_element_type=F32)
        acc = xp_ref[i, QKV_HALO + n * c:QKV_HALO + (n + 1) * c, :] * w[DN_CONV_K - 1:DN_CONV_K, :]
        for j in range(DN_CONV_K - 1):
            acc = acc + xs_ref[i, j, rows, :] * w[j:j + 1, :]
        qkvc = _silu(acc)
        for h in range(DN_HEADS):
            q, k, v = (qkvc[:, part * DN_WIDTH + h * hd:part * DN_WIDTH + (h + 1) * hd] for part in range(3))
            q = q * (lax.rsqrt(jnp.sum(q * q, axis=-1, keepdims=True) + NORM_EPS) * (hd ** -0.5))
            k = k * lax.rsqrt(jnp.sum(k * k, axis=-1, keepdims=True) + NORM_EPS)
            gc = gcum_col[i, n][:, h:h + 1]
            gr = gcum_row[i, n][h:h + 1, :]
            bt = beta[i, n][:, h:h + 1]
            gl = gcum_col[i, n][c - 1:c, h:h + 1]
            e = jnp.exp(gc)
            kb = k * bt
            prep[i, n, h] = (
                jnp.concatenate([kb, q], axis=0).astype(BF16),
                k.astype(BF16),
                jnp.exp(jnp.where(incl, gc - gr, -jnp.inf)),
                jnp.concatenate([v * bt, kb * e], axis=1).astype(BF16),
                q * e,
                (k * jnp.exp(gl - gc)).astype(BF16),
                jnp.exp(gl))

    preps = [prep[p] for p in probs]
    kqs = [_bdot(p[0], p[1], _NT) for p in preps]
    aqks = [(kq[c:] * p[2]).astype(BF16) for kq, p in zip(kqs, preps)]
    ms = [jnp.where(strict, kq[:c] * p[2], 0.0) for kq, p in zip(kqs, preps)]
    tinvs = _unit_lower_inverses(ms, c)
    uws = [_bdot(tinv, p[3]) for tinv, p in zip(tinvs, preps)]
    wqs = [jnp.concatenate([uw[:, hd:], p[4]], axis=0).astype(BF16) for uw, p in zip(uws, preps)]
    local = dict(zip(probs, zip(uws, wqs, aqks, [p[5] for p in preps], [p[6] for p in preps])))

    heads = [(i, h) for i in range(nb) for h in range(DN_HEADS)]
    state = {(i, h): s_ref[i, h] for i, h in heads}
    for n in range(nchunk):
        rows = slice(n * c, (n + 1) * c)
        items = [(i, h) + local[i, n, h] for i, h in heads]
        s16s = [state[i, h].astype(BF16) for i, h in heads]
        wss = [_bdot(it[3], s16) for it, s16 in zip(items, s16s)]
        u16s = [(it[2][:, :hd] - ws[:c]).astype(BF16) for it, ws in zip(items, wss)]
        outs = [ws[c:] + _bdot(it[4], u16) for it, ws, u16 in zip(items, wss, u16s)]
        for it, u16 in zip(items, u16s):
            i, h = it[0], it[1]
            state[i, h] = state[i, h] * it[6] + _bdot(it[5], u16, _TN)
        for it, o in zip(items, outs):
            i, h = it[0], it[1]
            o = o * lax.rsqrt(jnp.mean(o * o, axis=-1, keepdims=True) + NORM_EPS) * norm_w
            osc_ref[i, rows, h * hd:(h + 1) * hd] = o
    for i, h in heads:
        s_ref[i, h] = state[i, h]

    for i in range(nb):
        xp_ref[i, 0:QKV_HALO, :] = xp_ref[i, tile:tile + QKV_HALO, :]
        o_ref[i] = (osc_ref[i] * _silu(gate_ref[i])).astype(o_ref.dtype)


def _dn_group(proj3, bdc3, bdr4, prev, s0, layer, qkv_w, p_row, p_col, norm_w, tile, chunk, nb):
    b, l, _ = proj3.shape
    nchunk = tile // chunk
    hd = DN_HEAD_DIM
    nrow = DN_CONV_K - 1
    const2 = lambda i, t: (0, 0)
    od, qkv_state, s_final = pl.pallas_call(
        functools.partial(_dn_kernel, tile=tile, chunk=chunk, nb=nb),
        grid=(b // nb, l // tile),
        in_specs=[
            pl.BlockSpec((nb, tile, 3 * DN_WIDTH), lambda i, t: (i, t, QKV_BLK)),
            pl.BlockSpec((nb, tile, DN_WIDTH), lambda i, t: (i, t, DNGATE_BLK)),
            pl.BlockSpec((nb, tile, SMALL_PAD), lambda i, t: (i, t, 0)),
            pl.BlockSpec((nb, nchunk, 2 * DN_HEADS, chunk), lambda i, t: (i, t, 0, 0)),
            pl.BlockSpec((None, nrow, nb, 3 * DN_WIDTH), lambda i, t: (layer, 0, i, 0)),
            pl.BlockSpec((None, nb, DN_HEADS, hd, hd), lambda i, t: (layer, i, 0, 0, 0)),
            pl.BlockSpec((DN_CONV_K, 3 * DN_WIDTH), const2),
            pl.BlockSpec((2, DN_HEADS), const2),
            pl.BlockSpec((DN_HEADS, 2), const2),
            pl.BlockSpec((1, hd), const2),
        ],
        out_specs=[pl.BlockSpec((nb, tile, DN_WIDTH), lambda i, t: (i, t, 0)),
                   pl.BlockSpec((nb, nrow, 3 * DN_WIDTH), lambda i, t: (i, 0, 0)),
                   pl.BlockSpec((nb, DN_HEADS, hd, hd), lambda i, t: (i, 0, 0, 0))],
        out_shape=[jax.ShapeDtypeStruct((b, l, DN_WIDTH), BF16),
                   jax.ShapeDtypeStruct((b, nrow, 3 * DN_WIDTH), F32),
                   jax.ShapeDtypeStruct((b, DN_HEADS, hd, hd), F32)],
        scratch_shapes=[pltpu.VMEM((nb, tile + QKV_HALO, 3 * DN_WIDTH), F32),
                        pltpu.VMEM((nb, DN_CONV_K - 1, tile + QKV_HALO, 3 * DN_WIDTH), F32),
                        pltpu.VMEM((nb, tile, DN_WIDTH), F32)],
        compiler_params=_cparams(("arbitrary", "arbitrary")),
        name="deltanet_group",
    )(proj3, proj3, bdc3, bdr4, jnp.swapaxes(prev, 1, 2), s0, qkv_w, p_row, p_col, norm_w)
    return od, qkv_state, s_final


def _dn_pipelined_kernel(q0_ref, qa_ref, qb_ref, c0_ref, ca_ref, cb_ref, r0_ref, ra_ref, rb_ref, gate_ref,
                         w_ref, pr_ref, pc_ref, nw_ref, o_ref, st_ref, s_ref,
                         xp_ref, kq_ref, k16_ref, uwr_ref, qe_ref, kd_ref, gcc_ref, gcr_ref, osc_ref,
                         *, tile, chunk, nt, n_tiles):
    c = chunk
    nchunk = tile // c
    hd = DN_HEAD_DIM
    lo = QKV_HALO - (DN_CONV_K - 1)
    hi = lax.Precision.HIGHEST
    k = pl.program_id(0)
    w = w_ref[...]
    neg_a_row = -jnp.exp(pr_ref[0:1, :])
    dtb_row = pr_ref[1:2, :]
    neg_a_col = -jnp.exp(pc_ref[:, 0:1])
    dtb_col = pc_ref[:, 1:2]
    norm_w = nw_ref[...]
    row = lax.broadcasted_iota(jnp.int32, (c, c), 0)
    col = lax.broadcasted_iota(jnp.int32, (c, c), 1)
    incl = row >= col
    strict = row > col
    tri_l = incl.astype(F32)
    tri_u = (row <= col).astype(F32)
    probs = [(n, h) for n in range(nchunk) for h in range(DN_HEADS)]

    def prepare(buf, q_ref, c_ref, r_ref, first, write_state):
        halo = xp_ref[0:QKV_HALO, :]
        xp_ref[0:QKV_HALO, :] = jnp.zeros_like(halo) if first is True else jnp.where(first, 0.0, halo)
        xp_ref[QKV_HALO:QKV_HALO + tile, :] = q_ref[0]
        if write_state:
            st_ref[0] = xp_ref[tile + lo:tile + QKV_HALO, :]
        for n in range(nchunk):
            rows = slice(n * c, (n + 1) * c)
            bd = c_ref[0, rows, :]
            beta = _sigmoid(bd[:, 0:DN_HEADS])
            g_col = neg_a_row * _softplus(bd[:, DN_HEADS:2 * DN_HEADS] + dtb_row)
            gcum_col = jnp.dot(tri_l, g_col, precision=hi, preferred_element_type=F32)
            br = r_ref[0, n]
            g_row = neg_a_col * _softplus(br[DN_HEADS:2 * DN_HEADS, :] + dtb_col)
            gcc_ref[buf, n] = gcum_col
            gcr_ref[buf, n] = jnp.dot(g_row, tri_u, precision=hi, preferred_element_type=F32)
            window = xp_ref[n * c:(n + 1) * c + QKV_HALO, :]
            acc = window[QKV_HALO:, :] * w[DN_CONV_K - 1:DN_CONV_K, :]
            for j in range(DN_CONV_K - 1):
                acc = acc + pltpu.roll(window, c + QKV_HALO - (lo + j), axis=0)[0:c, :] * w[j:j + 1, :]
            qkvc = _silu(acc)
            for h in range(DN_HEADS):
                p = n * DN_HEADS + h
                q, kk, v = (qkvc[:, part * DN_WIDTH + h * hd:part * DN_WIDTH + (h + 1) * hd] for part in range(3))
                q = q * (lax.rsqrt(jnp.sum(q * q, axis=-1, keepdims=True) + NORM_EPS) * (hd ** -0.5))
                kk = kk * lax.rsqrt(jnp.sum(kk * kk, axis=-1, keepdims=True) + NORM_EPS)
                gc = gcum_col[:, h:h + 1]
                bt = beta[:, h:h + 1]
                gl = gcum_col[c - 1:c, h:h + 1]
                e = jnp.exp(gc)
                kb = kk * bt
                kq_ref[buf, p] = jnp.concatenate([kb, q], axis=0).astype(BF16)
                k16_ref[buf, p] = kk.astype(BF16)
                uwr_ref[buf, p] = jnp.concatenate([v * bt, kb * e], axis=1).astype(BF16)
                qe_ref[buf, p] = q * e
                kd_ref[buf, p] = (kk * jnp.exp(gl - gc)).astype(BF16)
        xp_ref[0:QKV_HALO, :] = xp_ref[tile:tile + QKV_HALO, :]

    def finish(buf, half):
        idx = [n * DN_HEADS + h for n, h in probs]
        kqs = [_bdot(kq_ref[buf, p], k16_ref[buf, p], _NT) for p in idx]
        gcs = [gcc_ref[buf, n][:, h:h + 1] for n, h in probs]
        decays = [jnp.exp(jnp.where(incl, gc - gcr_ref[buf, n][h:h + 1, :], -jnp.inf))
                  for gc, (n, h) in zip(gcs, probs)]
        aqks = [(kq[c:] * dc).astype(BF16) for kq, dc in zip(kqs, decays)]
        ms = [jnp.where(strict, kq[:c] * dc, 0.0) for kq, dc in zip(kqs, decays)]
        tinvs = _unit_lower_inverses(ms, c)
        uws = [_bdot(tinv, uwr_ref[buf, p]) for tinv, p in zip(tinvs, idx)]
        wqs = [jnp.concatenate([uw[:, hd:], qe_ref[buf, p]], axis=0).astype(BF16) for uw, p in zip(uws, idx)]
        egls = [jnp.exp(gc[c - 1:c, :]) for gc in gcs]
        local = dict(zip(probs, zip(uws, wqs, aqks, egls)))
        state = [s_ref[0, h] for h in range(DN_HEADS)]
        for n in range(nchunk):
            rows = slice(n * c, (n + 1) * c)
            items = [local[n, h] for h in range(DN_HEADS)]
            s16s = [s.astype(BF16) for s in state]
            wss = [_bdot(it[1], s16) for it, s16 in zip(items, s16s)]
            u16s = [(it[0][:, :hd] - ws[:c]).astype(BF16) for it, ws in zip(items, wss)]
            outs = [ws[c:] + _bdot(it[2], u16) for it, ws, u16 in zip(items, wss, u16s)]
            state = [s * it[3] + _bdot(kd_ref[buf, n * DN_HEADS + h], u16, _TN)
                     for h, (s, it, u16) in enumerate(zip(state, items, u16s))]
            for h, o in enumerate(outs):
                o = o * lax.rsqrt(jnp.mean(o * o, axis=-1, keepdims=True) + NORM_EPS) * norm_w
                osc_ref[rows, h * hd:(h + 1) * hd] = o
        for h in range(DN_HEADS):
            s_ref[0, h] = state[h]
        rows_out = slice(half * tile, (half + 1) * tile)
        o_ref[0, rows_out, :] = (osc_ref[...] * _silu(gate_ref[0, rows_out, :])).astype(o_ref.dtype)

    @pl.when(k == 0)
    def _():
        prepare(0, q0_ref, c0_ref, r0_ref, True, False)

    @pl.when(lax.rem(2 * k, nt) == 0)
    def _():
        s_ref[0] = jnp.zeros((DN_HEADS, hd, hd), F32)

    prepare(1, qa_ref, ca_ref, ra_ref, False, True)
    finish(0, 0)
    prepare(0, qb_ref, cb_ref, rb_ref, lax.rem(jnp.minimum(2 * k + 2, n_tiles - 1), nt) == 0, False)
    finish(1, 1)


def _dn_group_pipelined(proj3, bdc3, bdr4, qkv_w, p_row, p_col, norm_w, tile, chunk):
    b, l, _ = proj3.shape
    nt = l // tile
    assert nt % 2 == 0
    n_tiles = b * nt
    nchunk = tile // chunk
    hd = DN_HEAD_DIM
    n_prob = nchunk * DN_HEADS
    proj_t = proj3.reshape(n_tiles, tile, N_MAIN)
    proj_p = proj3.reshape(n_tiles // 2, 2 * tile, N_MAIN)
    bdc_t = bdc3.reshape(n_tiles, tile, SMALL_PAD)
    bdr_t = bdr4.reshape(n_tiles, nchunk, 2 * DN_HEADS, chunk)
    odd = lambda k: 2 * k + 1
    nxt = lambda k: jnp.minimum(2 * k + 2, n_tiles - 1)
    once = pl.Buffered(1)

    def tile_specs(index):
        mode = once if index is None else None
        at = (lambda k: 0) if index is None else index
        return [pl.BlockSpec((1, tile, 3 * DN_WIDTH), lambda k: (at(k), 0, QKV_BLK), pipeline_mode=mode),
                pl.BlockSpec((1, tile, SMALL_PAD), lambda k: (at(k), 0, 0), pipeline_mode=mode),
                pl.BlockSpec((1, nchunk, 2 * DN_HEADS, chunk), lambda k: (at(k), 0, 0, 0), pipeline_mode=mode)]

    (q0, c0, r0), (qa, ca, ra), (qb, cb, rb) = tile_specs(None), tile_specs(odd), tile_specs(nxt)
    const2 = lambda k: (0, 0)
    od, qkv_state, s_final = pl.pallas_call(
        functools.partial(_dn_pipelined_kernel, tile=tile, chunk=chunk, nt=nt, n_tiles=n_tiles),
        grid=(n_tiles // 2,),
        in_specs=[q0, qa, qb, c0, ca, cb, r0, ra, rb,
                  pl.BlockSpec((1, 2 * tile, DN_WIDTH), lambda k: (k, 0, DNGATE_BLK)),
                  pl.BlockSpec((DN_CONV_K, 3 * DN_WIDTH), const2),
                  pl.BlockSpec((2, DN_HEADS), const2),
                  pl.BlockSpec((DN_HEADS, 2), const2),
                  pl.BlockSpec((1, hd), const2)],
        out_specs=[pl.BlockSpec((1, 2 * tile, DN_WIDTH), lambda k: (k, 0, 0)),
                   pl.BlockSpec((1, DN_CONV_K - 1, 3 * DN_WIDTH), lambda k: (2 * k // nt, 0, 0)),
                   pl.BlockSpec((1, DN_HEADS, hd, hd), lambda k: (2 * k // nt, 0, 0, 0))],
        out_shape=[jax.ShapeDtypeStruct((n_tiles // 2, 2 * tile, DN_WIDTH), BF16),
                   jax.ShapeDtypeStruct((b, DN_CONV_K - 1, 3 * DN_WIDTH), F32),
                   jax.ShapeDtypeStruct((b, DN_HEADS, hd, hd), F32)],
        scratch_shapes=[pltpu.VMEM((tile + QKV_HALO, 3 * DN_WIDTH), F32),
                        pltpu.VMEM((2, n_prob, 2 * chunk, hd), BF16),
                        pltpu.VMEM((2, n_prob, chunk, hd), BF16),
                        pltpu.VMEM((2, n_prob, chunk, 2 * hd), BF16),
                        pltpu.VMEM((2, n_prob, chunk, hd), F32),
                        pltpu.VMEM((2, n_prob, chunk, hd), BF16),
                        pltpu.VMEM((2, nchunk, chunk, DN_HEADS), F32),
                        pltpu.VMEM((2, nchunk, DN_HEADS, chunk), F32),
                        pltpu.VMEM((tile, DN_WIDTH), F32)],
        compiler_params=_cparams(("arbitrary",)),
        name="deltanet_pipelined",
    )(proj_t, proj_t, proj_t, bdc_t, bdc_t, bdc_t, bdr_t, bdr_t, bdr_t, proj_p, qkv_w, p_row, p_col, norm_w)
    return od.reshape(b, l, DN_WIDTH), qkv_state, s_final


def _mem_kernel(q_ref, g_ref, k_ref, v_ref, o_ref, *, nb):
    hd = MEM_HEAD_DIM
    scale = hd ** -0.5
    probs = [(i, h) for i in range(nb) for h in range(MEM_HEADS)]
    lanes = [slice(h * hd, (h + 1) * hd) for h in range(MEM_HEADS)]
    qs = [q_ref[i, :, lanes[h]] for i, h in probs]
    ks = [k_ref[i, pl.ds(h, MEM_TOKENS, stride=MEM_HEADS), :] for i, h in probs]
    vs = [v_ref[i, pl.ds(h, MEM_TOKENS, stride=MEM_HEADS), :] for i, h in probs]
    ss = [_bdot(q, k, _NT) * scale for q, k in zip(qs, ks)]
    es = [jnp.exp(s - jnp.max(s, axis=-1, keepdims=True)) for s in ss]
    oms = [_bdot(e, v) / jnp.sum(e, axis=-1, keepdims=True) for e, v in zip(es, vs)]
    for (i, h), om in zip(probs, oms):
        o_ref[i, :, lanes[h]] = (om * _silu(g_ref[i, :, lanes[h]])).astype(o_ref.dtype)


def _mem_group(proj3, mem_k, mem_v, tile, nb):
    b, l, _ = proj3.shape
    cb = COL_BLOCK
    rows = MEM_TOKENS * MEM_HEADS
    return pl.pallas_call(
        functools.partial(_mem_kernel, nb=nb),
        grid=(b // nb, l // tile),
        in_specs=[pl.BlockSpec((nb, tile, cb), lambda i, t: (i, t, MEMQ_BLK)),
                  pl.BlockSpec((nb, tile, cb), lambda i, t: (i, t, MEMG_BLK)),
                  pl.BlockSpec((nb, rows, MEM_HEAD_DIM), lambda i, t: (i, 0, 0)),
                  pl.BlockSpec((nb, rows, MEM_HEAD_DIM), lambda i, t: (i, 0, 0))],
        out_specs=pl.BlockSpec((nb, tile, MEM_WIDTH), lambda i, t: (i, t, 0)),
        out_shape=jax.ShapeDtypeStruct((b, l, MEM_WIDTH), BF16),
        compiler_params=_cparams(("arbitrary", "arbitrary")),
        name="memory_group",
    )(proj3, proj3, mem_k, mem_v)


def _out_kernel(oc_ref, od_ref, om_ref, x_ref, w_ref, p_ref, y_ref):
    half = x_ref.shape[0] // 2
    for r in range(2):
        rows = slice(r * half, (r + 1) * half)
        h = jnp.dot(oc_ref[rows, :], w_ref[0:C_CONV, :], preferred_element_type=F32)
        h = h + jnp.dot(od_ref[rows, :], w_ref[C_CONV:C_CONV + DN_WIDTH, :], preferred_element_type=F32)
        h = h + jnp.dot(om_ref[rows, :], w_ref[C_CONV + DN_WIDTH:, :], preferred_element_type=F32)
        z = DEEPNORM_ALPHA * x_ref[rows, :] + h
        mu = jnp.mean(z, axis=-1, keepdims=True)
        zc = z - mu
        var = jnp.mean(zc * zc, axis=-1, keepdims=True)
        y_ref[rows, :] = zc * lax.rsqrt(var + LN_EPS) * p_ref[0:1, :] + p_ref[1:2, :]


def _out_proj(oc, od, om, x2, w_out, ln_p, bm):
    m = x2.shape[0]
    return pl.pallas_call(
        _out_kernel,
        grid=(m // bm,),
        in_specs=[pl.BlockSpec((bm, C_CONV), lambda i: (i, 0)),
                  pl.BlockSpec((bm, DN_WIDTH), lambda i: (i, 0)),
                  pl.BlockSpec((bm, MEM_WIDTH), lambda i: (i, 0)),
                  pl.BlockSpec((bm, D_MODEL), lambda i: (i, 0)),
                  pl.BlockSpec((D_MODEL, D_MODEL), lambda i: (0, 0)),
                  pl.BlockSpec((8, D_MODEL), lambda i: (0, 0))],
        out_specs=pl.BlockSpec((bm, D_MODEL), lambda i: (i, 0)),
        out_shape=jax.ShapeDtypeStruct((m, D_MODEL), F32),
        compiler_params=_cparams(("arbitrary",)),
        name="out_proj_ln",
    )(oc, od, om, x2, w_out, ln_p)


def _pad_rows(rows, n):
    a = jnp.stack(rows).astype(F32)
    return jnp.pad(a, ((0, n - a.shape[0]), (0, 0)))


def _layer(x, conv_prev, qkv_prev, s0, mem_k, mem_v, wts, *, bm, conv_tile, conv_nb, dn_tile, dn_chunk, dn_nb,
           mem_tile, mem_nb, out_bm):
    (w_in_t, layer, conv_w, conv_p, qkv_w, p_row, p_col, norm_w, w_out, ln_p) = wts
    b, l, _ = x.shape
    x2 = x.reshape(b * l, D_MODEL)
    proj, bdc, bdr = _in_proj(x2, w_in_t, layer, bm)
    proj3 = proj.reshape(b, l, N_MAIN)
    bdc3 = bdc.reshape(b, l, SMALL_PAD)
    bdr4 = bdr.reshape(2 * DN_HEADS, b, l // dn_chunk, dn_chunk).transpose(1, 2, 0, 3)

    oc, conv_state = _conv_group(proj3, conv_prev, layer, conv_w, conv_p, conv_tile, conv_nb)
    if s0 is None:
        od, qkv_state, s_final = _dn_group_pipelined(proj3, bdc3, bdr4, qkv_w, p_row, p_col, norm_w,
                                                     dn_tile, dn_chunk)
    else:
        od, qkv_state, s_final = _dn_group(proj3, bdc3, bdr4, qkv_prev, s0, layer, qkv_w, p_row, p_col, norm_w,
                                           dn_tile, dn_chunk, dn_nb)
    om = _mem_group(proj3, mem_k, mem_v, mem_tile, mem_nb)
    y = _out_proj(oc.reshape(b * l, C_CONV), od.reshape(b * l, DN_WIDTH), om.reshape(b * l, MEM_WIDTH),
                  x2, w_out, ln_p, out_bm)
    return y.reshape(b, l, D_MODEL), conv_state, qkv_state, s_final


def kernel(x_prompt, x_sample, mem_prompt, state_conv, state_qkv_conv, state_delta, cache_mem_k, cache_mem_v,
           w_in, conv_w, conv_b, conv_ln_g, conv_ln_b, qkv_conv_w, a_log, dt_bias, delta_norm_w,
           w_mem_k, w_mem_v, w_out, ln_g, ln_b):
    bp = x_prompt.shape[0]
    bs = x_sample.shape[0]
    hp, hs = x_prompt, x_sample
    outs = [[] for _ in range(8)]
    w_in_t = jnp.swapaxes(w_in, 1, 2)
    for i in range(DEPTH):
        conv_p = _pad_rows([conv_b[i], conv_ln_g[i], conv_ln_b[i]], 8)
        p_row = jnp.stack([a_log[i], dt_bias[i]]).astype(F32)
        wts = (w_in_t, i, conv_w[i], conv_p, qkv_conv_w[i], p_row, p_row.T,
               delta_norm_w[i].reshape(1, DN_HEAD_DIM), w_out[i].astype(BF16),
               _pad_rows([ln_g[i], ln_b[i]], 8))

        w_kv = jnp.concatenate([w_mem_k[i], w_mem_v[i]], axis=1).astype(BF16)
        kv = _matmul(mem_prompt.reshape(bp * MEM_TOKENS, D_MODEL), w_kv, 512, 512)
        mk = kv[:, :MEM_WIDTH].reshape(bp, MEM_TOKENS, MEM_HEADS, MEM_HEAD_DIM)
        mv = kv[:, MEM_WIDTH:].reshape(bp, MEM_TOKENS, MEM_HEADS, MEM_HEAD_DIM)
        kv_rows = MEM_TOKENS * MEM_HEADS
        hp, c_st, q_st, s_st = _layer(
            hp, None, None, None, mk.reshape(bp, kv_rows, MEM_HEAD_DIM), mv.reshape(bp, kv_rows, MEM_HEAD_DIM),
            wts, bm=2048, conv_tile=512, conv_nb=1, dn_tile=256, dn_chunk=64, dn_nb=1, mem_tile=512, mem_nb=1,
            out_bm=512)
        outs[0].append(c_st); outs[1].append(q_st); outs[2].append(s_st)
        outs[3].append(mk)
        outs[4].append(mv)

        ls = hs.shape[1]
        hs, c_st, q_st, s_st = _layer(
            hs, state_conv, state_qkv_conv, state_delta,
            cache_mem_k[i].reshape(bs, kv_rows, MEM_HEAD_DIM), cache_mem_v[i].reshape(bs, kv_rows, MEM_HEAD_DIM),
            wts, bm=bs * ls, conv_tile=ls, conv_nb=8, dn_tile=ls, dn_chunk=ls, dn_nb=8, mem_tile=ls, mem_nb=8,
            out_bm=512)
        outs[5].append(c_st); outs[6].append(q_st); outs[7].append(s_st)
    return (hp, hs) + tuple(jnp.stack(o) for o in outs)
```

```python
import functools

import jax
import jax.numpy as jnp
from jax import lax
from jax.experimental import pallas as pl
from jax.experimental.pallas import tpu as pltpu

F32 = jnp.float32
BF16 = jnp.bfloat16

D_MODEL = 2048
C_CONV = 512
CONV_K = 31
DN_HEADS = 8
DN_HEAD_DIM = 128
DN_WIDTH = DN_HEADS * DN_HEAD_DIM
DN_CONV_K = 4
MEM_HEADS = 4
MEM_HEAD_DIM = 128
MEM_WIDTH = MEM_HEADS * MEM_HEAD_DIM
MEM_TOKENS = 256
DEPTH = 1
DEEPNORM_ALPHA = (2 * DEPTH) ** 0.25
LN_EPS = 1e-5
NORM_EPS = 1e-6

N_MAIN = 3 * DN_WIDTH + DN_WIDTH + 3 * C_CONV + 2 * MEM_WIDTH
COL_BLOCK = 512
QKV_BLK = 0
DNGATE_BLK = 3
GLUA_BLK, GLUB_BLK, CGATE_BLK, MEMQ_BLK, MEMG_BLK = 8, 9, 10, 11, 12
SMALL_PAD = 128
N_DIRECT_BLOCKS = (3 * C_CONV + 4 * DN_WIDTH) // COL_BLOCK
CONV_HALO = 32
QKV_HALO = 8
SUBLANES = 8
CONV_ROW_BLOCK = 64

VMEM_LIMIT = 56 * 1024 * 1024


def _sigmoid(x):
    return 1.0 / (1.0 + jnp.exp(-x))


def _silu(x):
    return x * _sigmoid(x)


def _softplus(x):
    return jnp.maximum(x, 0.0) + jnp.log(1.0 + jnp.exp(-jnp.abs(x)))


def _cparams(sem):
    return pltpu.CompilerParams(dimension_semantics=sem, vmem_limit_bytes=VMEM_LIMIT)


def _proj_kernel(x_hbm, wa_ref, wb_ref, o_ref, bdc_ref, bdr_ref, xf_ref, xb_ref, sem):
    i = pl.program_id(0)
    j = pl.program_id(1)
    bm = xf_ref.shape[0]
    nt = (((1,), (1,)), ((), ()))
    n_bd = 2 * DN_HEADS

    def x_copy(block):
        return pltpu.make_async_copy(x_hbm.at[pl.ds(block * bm, bm), :], xf_ref, sem)

    @pl.when(j == 0)
    def _():
        @pl.when(i == 0)
        def _():
            x_copy(0).start()

        x_copy(i).wait()
        xb_ref[...] = xf_ref[...].astype(BF16)

        @pl.when(i + 1 < pl.num_programs(0))
        def _():
            x_copy(i + 1).start()

    @pl.when(j < N_DIRECT_BLOCKS)
    def _():
        o_ref[...] = lax.dot_general(xb_ref[...], wa_ref[...].astype(BF16), nt, preferred_element_type=F32)

    @pl.when(j >= N_DIRECT_BLOCKS)
    def _():
        w = jnp.concatenate([wa_ref[n_bd:, :], wb_ref[...]], axis=0).astype(BF16)
        o_ref[...] = lax.dot_general(xb_ref[...], w, nt, preferred_element_type=F32)

    @pl.when(j == N_DIRECT_BLOCKS)
    def _():
        bd = lax.dot_general(xb_ref[...], wa_ref[0:SMALL_PAD, :].astype(BF16), nt, preferred_element_type=F32)
        bdc_ref[...] = bd
        bdr_ref[...] = bd.T[0:n_bd, :]


def _proj_out_block(j):
    n_conv = 3 * C_CONV // COL_BLOCK
    return jnp.where(j < n_conv, j + GLUA_BLK, jnp.where(j < N_DIRECT_BLOCKS, j - n_conv, j))


def _in_proj(x2, w_in_t, layer, bm):
    m = x2.shape[0]
    bn = COL_BLOCK
    n_blocks = N_MAIN // bn
    n_bd = 2 * DN_HEADS
    return pl.pallas_call(
        _proj_kernel,
        grid=(m // bm, n_blocks),
        in_specs=[
            pl.BlockSpec(memory_space=pl.ANY),
            pl.BlockSpec((None, bn, D_MODEL), lambda i, j: (layer, j, 0)),
            pl.BlockSpec((None, n_bd, D_MODEL),
                         lambda i, j: (layer, (jnp.maximum(j + 1, n_blocks - 1)) * (bn // n_bd), 0)),
        ],
        out_specs=[
            pl.BlockSpec((bm, bn), lambda i, j: (i, _proj_out_block(j))),
            pl.BlockSpec((bm, SMALL_PAD), lambda i, j: (i, 0)),
            pl.BlockSpec((2 * DN_HEADS, bm), lambda i, j: (0, i)),
        ],
        out_shape=[
            jax.ShapeDtypeStruct((m, N_MAIN), F32),
            jax.ShapeDtypeStruct((m, SMALL_PAD), F32),
            jax.ShapeDtypeStruct((2 * DN_HEADS, m), F32),
        ],
        scratch_shapes=[pltpu.VMEM((bm, D_MODEL), F32), pltpu.VMEM((bm, D_MODEL), BF16),
                        pltpu.SemaphoreType.DMA(())],
        compiler_params=_cparams(("arbitrary", "arbitrary")),
        name="in_proj",
    )(x2, w_in_t, w_in_t)


def _mm_kernel(x_ref, w_ref, o_ref):
    o_ref[...] = jnp.dot(x_ref[...].astype(BF16), w_ref[...], preferred_element_type=F32)


def _matmul(x2, w, bm, bn):
    m, k = x2.shape
    n = w.shape[1]
    return pl.pallas_call(
        _mm_kernel,
        grid=(m // bm, n // bn),
        in_specs=[pl.BlockSpec((bm, k), lambda i, j: (i, 0)),
                  pl.BlockSpec((k, bn), lambda i, j: (0, j))],
        out_specs=pl.BlockSpec((bm, bn), lambda i, j: (i, j)),
        out_shape=jax.ShapeDtypeStruct((m, n), F32),
        compiler_params=_cparams(("arbitrary", "arbitrary")),
        name="mem_kv_proj",
    )(x2, w)


def _conv_kernel(*refs, tile, nb, has_state, guest_nb):
    refs = list(refs)
    a_ref, b_ref, g_ref = refs[:3]
    del refs[:3]
    st_ref = refs.pop(0) if has_state else None
    w_ref, p_ref = refs[:2]
    del refs[:2]
    if guest_nb:
        gq_ref, gg_ref, gk_ref, gv_ref = refs[:4]
        del refs[:4]
        o_ref, so_ref, go_ref, xp_ref = refs
        _mem_kernel(gq_ref, gg_ref, gk_ref, gv_ref, go_ref, nb=guest_nb)
    else:
        o_ref, so_ref, xp_ref = refs
    t = pl.program_id(1)
    last = pl.num_programs(1) - 1
    lo = CONV_HALO - (CONV_K - 1)
    n = tile + CONV_HALO
    rb = min(CONV_ROW_BLOCK, tile)
    w = w_ref[...]
    conv_b = p_ref[0:1, :]
    ln_g = p_ref[1:2, :]
    ln_b = p_ref[2:3, :]
    for i in range(nb):
        @pl.when(t == 0)
        def _():
            xp_ref[i, 0, 0:CONV_HALO, :] = jnp.zeros((CONV_HALO, C_CONV), F32)
            if has_state:
                xp_ref[i, 0, lo:CONV_HALO, :] = st_ref[:, i, :]

        xp_ref[i, 0, CONV_HALO:n, :] = a_ref[i] * _sigmoid(b_ref[i])
        xp = xp_ref[i, 0]
        for s in range(1, SUBLANES):
            xp_ref[i, s] = pltpu.roll(xp, n - s, axis=0)

        for r in range(tile // rb):
            acc = None
            for j in range(CONV_K):
                s = (lo + j) % SUBLANES
                start = lo + j - s + r * rb
                term = xp_ref[i, s, start:start + rb, :] * w[j:j + 1, :]
                acc = term if acc is None else acc + term
            hc = acc + conv_b
            mu = jnp.mean(hc, axis=-1, keepdims=True)
            xc = hc - mu
            var = jnp.mean(xc * xc, axis=-1, keepdims=True)
            hn = xc * lax.rsqrt(var + LN_EPS) * ln_g + ln_b
            rows = slice(r * rb, (r + 1) * rb)
            o_ref[i, rows, :] = (_silu(hn) * _silu(g_ref[i, rows, :])).astype(o_ref.dtype)

        @pl.when(t == last)
        def _():
            so_ref[i] = xp_ref[i, 0, tile + lo:n, :]

        xp_ref[i, 0, 0:CONV_HALO, :] = xp_ref[i, 0, tile:n, :]


def _conv_group(proj3, state, layer, conv_w, conv_p, tile, nb, guest=None):
    b, l, _ = proj3.shape
    has_state = state is not None
    cb = COL_BLOCK
    nrow = CONV_K - 1
    nt = l // tile
    in_specs = [
        pl.BlockSpec((nb, tile, cb), lambda i, t: (i, t, GLUA_BLK)),
        pl.BlockSpec((nb, tile, cb), lambda i, t: (i, t, GLUB_BLK)),
        pl.BlockSpec((nb, tile, cb), lambda i, t: (i, t, CGATE_BLK)),
    ]
    args = [proj3, proj3, proj3]
    if has_state:
        in_specs.append(pl.BlockSpec((None, nrow, nb, C_CONV), lambda i, t: (layer, 0, i, 0)))
        args.append(jnp.swapaxes(state, 1, 2))
    in_specs += [pl.BlockSpec((CONV_K, C_CONV), lambda i, t: (0, 0)),
                 pl.BlockSpec((8, C_CONV), lambda i, t: (0, 0))]
    args += [conv_w, conv_p]
    out_specs = [pl.BlockSpec((nb, tile, C_CONV), lambda i, t: (i, t, 0)),
                 pl.BlockSpec((nb, nrow, C_CONV), lambda i, t: (i, 0, 0))]
    out_shape = [jax.ShapeDtypeStruct((b, l, C_CONV), BF16),
                 jax.ShapeDtypeStruct((b, nrow, C_CONV), F32)]
    guest_nb = 0
    if guest is not None:
        g_proj3, g_k, g_v, guest_nb = guest
        gb, gl, _ = g_proj3.shape
        assert gb // guest_nb == (b // nb) * nt
        kv_rows = MEM_TOKENS * MEM_HEADS
        step = lambda i, t: i * nt + t
        in_specs += [pl.BlockSpec((guest_nb, gl, cb), lambda i, t: (step(i, t), 0, MEMQ_BLK)),
                     pl.BlockSpec((guest_nb, gl, cb), lambda i, t: (step(i, t), 0, MEMG_BLK)),
                     pl.BlockSpec((guest_nb, kv_rows, MEM_HEAD_DIM), lambda i, t: (step(i, t), 0, 0)),
                     pl.BlockSpec((guest_nb, kv_rows, MEM_HEAD_DIM), lambda i, t: (step(i, t), 0, 0))]
        args += [g_proj3, g_proj3, g_k, g_v]
        out_specs.append(pl.BlockSpec((guest_nb, gl, MEM_WIDTH), lambda i, t: (step(i, t), 0, 0)))
        out_shape.append(jax.ShapeDtypeStruct((gb, gl, MEM_WIDTH), BF16))
    return pl.pallas_call(
        functools.partial(_conv_kernel, tile=tile, nb=nb, has_state=has_state, guest_nb=guest_nb),
        grid=(b // nb, nt),
        in_specs=in_specs,
        out_specs=out_specs,
        out_shape=out_shape,
        scratch_shapes=[pltpu.VMEM((nb, SUBLANES, tile + CONV_HALO, C_CONV), F32)],
        compiler_params=_cparams(("arbitrary", "arbitrary")),
        name="conv_group",
    )(*args)


def _bdot(a, b, dims=None):
    a = a.astype(BF16)
    b = b.astype(BF16)
    if dims is None:
        return jnp.dot(a, b, preferred_element_type=F32)
    return lax.dot_general(a, b, (dims, ((), ())), preferred_element_type=F32)


_NT = ((1,), (1,))
_TN = ((0,), (0,))


def _split_bf16(a):
    hi = a.astype(BF16)
    return hi, (a - hi.astype(F32)).astype(BF16)


def _unit_lower_inverses(ms, c):
    row = lax.broadcasted_iota(jnp.int32, (c, c), 0)
    col = lax.broadcasted_iota(jnp.int32, (c, c), 1)
    eye = (row == col).astype(F32)
    ps = [-m for m in ms]
    ts = [eye + p for p in ps]
    span = 2
    while span < c:
        ps = [_bdot(p, p) for p in ps]
        ts = [t + _bdot(t, p) for t, p in zip(ts, ps)]
        span *= 2
    msp = [_split_bf16(m) for m in ms]
    tsp = [_split_bf16(t) for t in ts]
    mts = [_bdot(mh, th) + _bdot(mh, tl) + _bdot(ml, th) for (mh, ml), (th, tl) in zip(msp, tsp)]
    rs = [eye - t - mt for t, mt in zip(ts, mts)]
    return [t + _bdot(th, r) for t, (th, _), r in zip(ts, tsp, rs)]


def _dn_kernel(qkv_ref, gate_ref, bdc_ref, bdr_ref, prev_ref, s0_ref, w_ref, pr_ref, pc_ref, nw_ref,
               o_ref, st_ref, s_ref, xp_ref, xs_ref, osc_ref, *, tile, chunk, nb):
    c = chunk
    nchunk = tile // c
    hd = DN_HEAD_DIM
    t = pl.program_id(1)
    last = pl.num_programs(1) - 1
    lo = QKV_HALO - (DN_CONV_K - 1)
    hi = lax.Precision.HIGHEST
    w = w_ref[...]

    for i in range(nb):
        @pl.when(t == 0)
        def _():
            xp_ref[i, 0:QKV_HALO, :] = jnp.zeros((QKV_HALO, 3 * DN_WIDTH), F32)
            xp_ref[i, lo:QKV_HALO, :] = prev_ref[:, i, :]
            s_ref[i] = s0_ref[i]

        xp_ref[i, QKV_HALO:QKV_HALO + tile, :] = qkv_ref[i]
        xp = xp_ref[i]
        for j in range(DN_CONV_K - 1):
            xs_ref[i, j] = pltpu.roll(xp, tile + QKV_HALO - (lo + j), axis=0)

        @pl.when(t == last)
        def _():
            st_ref[i] = xp_ref[i, tile + lo:tile + QKV_HALO, :]

    neg_a_row = -jnp.exp(pr_ref[0:1, :])
    dtb_row = pr_ref[1:2, :]
    neg_a_col = -jnp.exp(pc_ref[:, 0:1])
    dtb_col = pc_ref[:, 1:2]
    norm_w = nw_ref[...]

    row = lax.broadcasted_iota(jnp.int32, (c, c), 0)
    col = lax.broadcasted_iota(jnp.int32, (c, c), 1)
    incl = row >= col
    strict = row > col
    tri_l = incl.astype(F32)
    tri_u = (row <= col).astype(F32)

    seq_chunks = [(i, n) for i in range(nb) for n in range(nchunk)]
    probs = [(i, n, h) for i, n in seq_chunks for h in range(DN_HEADS)]
    beta, gcum_col, gcum_row, prep = {}, {}, {}, {}
    for i, n in seq_chunks:
        rows = slice(n * c, (n + 1) * c)
        bd = bdc_ref[i, rows, :]
        beta[i, n] = _sigmoid(bd[:, 0:DN_HEADS])
        g_col = neg_a_row * _softplus(bd[:, DN_HEADS:2 * DN_HEADS] + dtb_row)
        gcum_col[i, n] = jnp.dot(tri_l, g_col, precision=hi, preferred_element_type=F32)
        br = bdr_ref[i, n]
        g_row = neg_a_col * _softplus(br[DN_HEADS:2 * DN_HEADS, :] + dtb_col)
        gcum_row[i, n] = jnp.dot(g_row, tri_u, precision=hi, preferred_element_type=F32)
        acc = xp_ref[i, QKV_HALO + n * c:QKV_HALO + (n + 1) * c, :] * w[DN_CONV_K - 1:DN_CONV_K, :]
        for j in range(DN_CONV_K - 1):
            acc = acc + xs_ref[i, j, rows, :] * w[j:j + 1, :]
        qkvc = _silu(acc)
        for h in range(DN_HEADS):
            q, k, v = (qkvc[:, part * DN_WIDTH + h * hd:part * DN_WIDTH + (h + 1) * hd] for part in range(3))
            q = q * (lax.rsqrt(jnp.sum(q * q, axis=-1, keepdims=True) + NORM_EPS) * (hd ** -0.5))
            k = k * lax.rsqrt(jnp.sum(k * k, axis=-1, keepdims=True) + NORM_EPS)
            gc = gcum_col[i, n][:, h:h + 1]
            gr = gcum_row[i, n][h:h + 1, :]
            bt = beta[i, n][:, h:h + 1]
            gl = gcum_col[i, n][c - 1:c, h:h + 1]
            e = jnp.exp(gc)
            kb = k * bt
            prep[i, n, h] = (
                jnp.concatenate([kb, q], axis=0).astype(BF16),
                k.astype(BF16),
                jnp.exp(jnp.where(incl, gc - gr, -jnp.inf)),
                jnp.concatenate([v * bt, kb * e], axis=1).astype(BF16),
                q * e,
                (k * jnp.exp(gl - gc)).astype(BF16),
                jnp.exp(gl))

    preps = [prep[p] for p in probs]
    kqs = [_bdot(p[0], p[1], _NT) for p in preps]
    aqks = [(kq[c:] * p[2]).astype(BF16) for kq, p in zip(kqs, preps)]
    ms = [jnp.where(strict, kq[:c] * p[2], 0.0) for kq, p in zip(kqs, preps)]
    tinvs = _unit_lower_inverses(ms, c)
    uws = [_bdot(tinv, p[3]) for tinv, p in zip(tinvs, preps)]
    wqs = [jnp.concatenate([uw[:, hd:], p[4]], axis=0).astype(BF16) for uw, p in zip(uws, preps)]
    local = dict(zip(probs, zip(uws, wqs, aqks, [p[5] for p in preps], [p[6] for p in preps])))

    heads = [(i, h) for i in range(nb) for h in range(DN_HEADS)]
    state = {(i, h): s_ref[i, h] for i, h in heads}
    for n in range(nchunk):
        rows = slice(n * c, (n + 1) * c)
        items = [(i, h) + local[i, n, h] for i, h in heads]
        s16s = [state[i, h].astype(BF16) for i, h in heads]
        wss = [_bdot(it[3], s16) for it, s16 in zip(items, s16s)]
        u16s = [(it[2][:, :hd] - ws[:c]).astype(BF16) for it, ws in zip(items, wss)]
        outs = [ws[c:] + _bdot(it[4], u16) for it, ws, u16 in zip(items, wss, u16s)]
        for it, u16 in zip(items, u16s):
            i, h = it[0], it[1]
            state[i, h] = state[i, h] * it[6] + _bdot(it[5], u16, _TN)
        for it, o in zip(items, outs):
            i, h = it[0], it[1]
            o = o * lax.rsqrt(jnp.mean(o * o, axis=-1, keepdims=True) + NORM_EPS) * norm_w
            osc_ref[i, rows, h * hd:(h + 1) * hd] = o
    for i, h in heads:
        s_ref[i, h] = state[i, h]

    for i in range(nb):
        xp_ref[i, 0:QKV_HALO, :] = xp_ref[i, tile:tile + QKV_HALO, :]
        o_ref[i] = (osc_ref[i] * _silu(gate_ref[i])).astype(o_ref.dtype)


def _dn_group(proj3, bdc3, bdr4, prev, s0, layer, qkv_w, p_row, p_col, norm_w, tile, chunk, nb):
    b, l, _ = proj3.shape
    nchunk = tile // chunk
    hd = DN_HEAD_DIM
    nrow = DN_CONV_K - 1
    const2 = lambda i, t: (0, 0)
    od, qkv_state, s_final = pl.pallas_call(
        functools.partial(_dn_kernel, tile=tile, chunk=chunk, nb=nb),
        grid=(b // nb, l // tile),
        in_specs=[
            pl.BlockSpec((nb, tile, 3 * DN_WIDTH), lambda i, t: (i, t, QKV_BLK)),
            pl.BlockSpec((nb, tile, DN_WIDTH), lambda i, t: (i, t, DNGATE_BLK)),
            pl.BlockSpec((nb, tile, SMALL_PAD), lambda i, t: (i, t, 0)),
            pl.BlockSpec((nb, nchunk, 2 * DN_HEADS, chunk), lambda i, t: (i, t, 0, 0)),
            pl.BlockSpec((None, nrow, nb, 3 * DN_WIDTH), lambda i, t: (layer, 0, i, 0)),
            pl.BlockSpec((None, nb, DN_HEADS, hd, hd), lambda i, t: (layer, i, 0, 0, 0)),
            pl.BlockSpec((DN_CONV_K, 3 * DN_WIDTH), const2),
            pl.BlockSpec((2, DN_HEADS), const2),
            pl.BlockSpec((DN_HEADS, 2), const2),
            pl.BlockSpec((1, hd), const2),
        ],
        out_specs=[pl.BlockSpec((nb, tile, DN_WIDTH), lambda i, t: (i, t, 0)),
                   pl.BlockSpec((nb, nrow, 3 * DN_WIDTH), lambda i, t: (i, 0, 0)),
                   pl.BlockSpec((nb, DN_HEADS, hd, hd), lambda i, t: (i, 0, 0, 0))],
        out_shape=[jax.ShapeDtypeStruct((b, l, DN_WIDTH), BF16),
                   jax.ShapeDtypeStruct((b, nrow, 3 * DN_WIDTH), F32),
                   jax.ShapeDtypeStruct((b, DN_HEADS, hd, hd), F32)],
        scratch_shapes=[pltpu.VMEM((nb, tile + QKV_HALO, 3 * DN_WIDTH), F32),
                        pltpu.VMEM((nb, DN_CONV_K - 1, tile + QKV_HALO, 3 * DN_WIDTH), F32),
                        pltpu.VMEM((nb, tile, DN_WIDTH), F32)],
        compiler_params=_cparams(("arbitrary", "arbitrary")),
        name="deltanet_group",
    )(proj3, proj3, bdc3, bdr4, jnp.swapaxes(prev, 1, 2), s0, qkv_w, p_row, p_col, norm_w)
    return od, qkv_state, s_final


def _dn_pipelined_kernel(q0_ref, qa_ref, qb_ref, c0_ref, ca_ref, cb_ref, r0_ref, ra_ref, rb_ref, gate_ref,
                         w_ref, pr_ref, pc_ref, nw_ref, o_ref, st_ref, s_ref,
                         xp_ref, kq_ref, k16_ref, uwr_ref, qe_ref, kd_ref, gcc_ref, gcr_ref, osc_ref,
                         *, tile, chunk, nt, n_tiles):
    c = chunk
    nchunk = tile // c
    hd = DN_HEAD_DIM
    lo = QKV_HALO - (DN_CONV_K - 1)
    hi = lax.Precision.HIGHEST
    k = pl.program_id(0)
    w = w_ref[...]
    neg_a_row = -jnp.exp(pr_ref[0:1, :])
    dtb_row = pr_ref[1:2, :]
    neg_a_col = -jnp.exp(pc_ref[:, 0:1])
    dtb_col = pc_ref[:, 1:2]
    norm_w = nw_ref[...]
    row = lax.broadcasted_iota(jnp.int32, (c, c), 0)
    col = lax.broadcasted_iota(jnp.int32, (c, c), 1)
    incl = row >= col
    strict = row > col
    tri_l = incl.astype(F32)
    tri_u = (row <= col).astype(F32)
    probs = [(n, h) for n in range(nchunk) for h in range(DN_HEADS)]

    def prepare(buf, q_ref, c_ref, r_ref, first, write_state):
        halo = xp_ref[0:QKV_HALO, :]
        xp_ref[0:QKV_HALO, :] = jnp.zeros_like(halo) if first is True else jnp.where(first, 0.0, halo)
        xp_ref[QKV_HALO:QKV_HALO + tile, :] = q_ref[0]
        if write_state:
            st_ref[0] = xp_ref[tile + lo:tile + QKV_HALO, :]
        for n in range(nchunk):
            rows = slice(n * c, (n + 1) * c)
            bd = c_ref[0, rows, :]
            beta = _sigmoid(bd[:, 0:DN_HEADS])
            g_col = neg_a_row * _softplus(bd[:, DN_HEADS:2 * DN_HEADS] + dtb_row)
            gcum_col = jnp.dot(tri_l, g_col, precision=hi, preferred_element_type=F32)
            br = r_ref[0, n]
            g_row = neg_a_col * _softplus(br[DN_HEADS:2 * DN_HEADS, :] + dtb_col)
            gcc_ref[buf, n] = gcum_col
            gcr_ref[buf, n] = jnp.dot(g_row, tri_u, precision=hi, preferred_element_type=F32)
            window = xp_ref[n * c:(n + 1) * c + QKV_HALO, :]
            acc = window[QKV_HALO:, :] * w[DN_CONV_K - 1:DN_CONV_K, :]
            for j in range(DN_CONV_K - 1):
                acc = acc + pltpu.roll(window, c + QKV_HALO - (lo + j), axis=0)[0:c, :] * w[j:j + 1, :]
            qkvc = _silu(acc)
            for h in range(DN_HEADS):
                p = n * DN_HEADS + h
                q, kk, v = (qkvc[:, part * DN_WIDTH + h * hd:part * DN_WIDTH + (h + 1) * hd] for part in range(3))
                q = q * (lax.rsqrt(jnp.sum(q * q, axis=-1, keepdims=True) + NORM_EPS) * (hd ** -0.5))
                kk = kk * lax.rsqrt(jnp.sum(kk * kk, axis=-1, keepdims=True) + NORM_EPS)
                gc = gcum_col[:, h:h + 1]
                bt = beta[:, h:h + 1]
                gl = gcum_col[c - 1:c, h:h + 1]
                e = jnp.exp(gc)
                kb = kk * bt
                kq_ref[buf, p] = jnp.concatenate([kb, q], axis=0).astype(BF16)
                k16_ref[buf, p] = kk.astype(BF16)
                uwr_ref[buf, p] = jnp.concatenate([v * bt, kb * e], axis=1).astype(BF16)
                qe_ref[buf, p] = q * e
                kd_ref[buf, p] = (kk * jnp.exp(gl - gc)).astype(BF16)
        xp_ref[0:QKV_HALO, :] = xp_ref[tile:tile + QKV_HALO, :]

    def finish(buf, half):
        idx = [n * DN_HEADS + h for n, h in probs]
        kqs = [_bdot(kq_ref[buf, p], k16_ref[buf, p], _NT) for p in idx]
        gcs = [gcc_ref[buf, n][:, h:h + 1] for n, h in probs]
        decays = [jnp.exp(jnp.where(incl, gc - gcr_ref[buf, n][h:h + 1, :], -jnp.inf))
                  for gc, (n, h) in zip(gcs, probs)]
        aqks = [(kq[c:] * dc).astype(BF16) for kq, dc in zip(kqs, decays)]
        ms = [jnp.where(strict, kq[:c] * dc, 0.0) for kq, dc in zip(kqs, decays)]
        tinvs = _unit_lower_inverses(ms, c)
        uws = [_bdot(tinv, uwr_ref[buf, p]) for tinv, p in zip(tinvs, idx)]
        wqs = [jnp.concatenate([uw[:, hd:], qe_ref[buf, p]], axis=0).astype(BF16) for uw, p in zip(uws, idx)]
        egls = [jnp.exp(gc[c - 1:c, :]) for gc in gcs]
        local = dict(zip(probs, zip(uws, wqs, aqks, egls)))
        state = [s_ref[0, h] for h in range(DN_HEADS)]
        for n in range(nchunk):
            rows = slice(n * c, (n + 1) * c)
            items = [local[n, h] for h in range(DN_HEADS)]
            s16s = [s.astype(BF16) for s in state]
            wss = [_bdot(it[1], s16) for it, s16 in zip(items, s16s)]
            u16s = [(it[0][:, :hd] - ws[:c]).astype(BF16) for it, ws in zip(items, wss)]
            outs = [ws[c:] + _bdot(it[2], u16) for it, ws, u16 in zip(items, wss, u16s)]
            state = [s * it[3] + _bdot(kd_ref[buf, n * DN_HEADS + h], u16, _TN)
                     for h, (s, it, u16) in enumerate(zip(state, items, u16s))]
            for h, o in enumerate(outs):
                o = o * lax.rsqrt(jnp.mean(o * o, axis=-1, keepdims=True) + NORM_EPS) * norm_w
                osc_ref[rows, h * hd:(h + 1) * hd] = o
        for h in range(DN_HEADS):
            s_ref[0, h] = state[h]
        rows_out = slice(half * tile, (half + 1) * tile)
        o_ref[0, rows_out, :] = (osc_ref[...] * _silu(gate_ref[0, rows_out, :])).astype(o_ref.dtype)

    @pl.when(k == 0)
    def _():
        prepare(0, q0_ref, c0_ref, r0_ref, True, False)

    @pl.when(lax.rem(2 * k, nt) == 0)
    def _():
        s_ref[0] = jnp.zeros((DN_HEADS, hd, hd), F32)

    prepare(1, qa_ref, ca_ref, ra_ref, False, True)
    finish(0, 0)
    prepare(0, qb_ref, cb_ref, rb_ref, lax.rem(jnp.minimum(2 * k + 2, n_tiles - 1), nt) == 0, False)
    finish(1, 1)


def _dn_group_pipelined(proj3, bdc3, bdr4, qkv_w, p_row, p_col, norm_w, tile, chunk):
    b, l, _ = proj3.shape
    nt = l // tile
    assert nt % 2 == 0
    n_tiles = b * nt
    nchunk = tile // chunk
    hd = DN_HEAD_DIM
    n_prob = nchunk * DN_HEADS
    proj_t = proj3.reshape(n_tiles, tile, N_MAIN)
    proj_p = proj3.reshape(n_tiles // 2, 2 * tile, N_MAIN)
    bdc_t = bdc3.reshape(n_tiles, tile, SMALL_PAD)
    bdr_t = bdr4.reshape(n_tiles, nchunk, 2 * DN_HEADS, chunk)
    odd = lambda k: 2 * k + 1
    nxt = lambda k: jnp.minimum(2 * k + 2, n_tiles - 1)
    once = pl.Buffered(1)

    def tile_specs(index):
        mode = once if index is None else None
        at = (lambda k: 0) if index is None else index
        return [pl.BlockSpec((1, tile, 3 * DN_WIDTH), lambda k: (at(k), 0, QKV_BLK), pipeline_mode=mode),
                pl.BlockSpec((1, tile, SMALL_PAD), lambda k: (at(k), 0, 0), pipeline_mode=mode),
                pl.BlockSpec((1, nchunk, 2 * DN_HEADS, chunk), lambda k: (at(k), 0, 0, 0), pipeline_mode=mode)]

    (q0, c0, r0), (qa, ca, ra), (qb, cb, rb) = tile_specs(None), tile_specs(odd), tile_specs(nxt)
    const2 = lambda k: (0, 0)
    od, qkv_state, s_final = pl.pallas_call(
        functools.partial(_dn_pipelined_kernel, tile=tile, chunk=chunk, nt=nt, n_tiles=n_tiles),
        grid=(n_tiles // 2,),
        in_specs=[q0, qa, qb, c0, ca, cb, r0, ra, rb,
                  pl.BlockSpec((1, 2 * tile, DN_WIDTH), lambda k: (k, 0, DNGATE_BLK)),
                  pl.BlockSpec((DN_CONV_K, 3 * DN_WIDTH), const2),
                  pl.BlockSpec((2, DN_HEADS), const2),
                  pl.BlockSpec((DN_HEADS, 2), const2),
                  pl.BlockSpec((1, hd), const2)],
        out_specs=[pl.BlockSpec((1, 2 * tile, DN_WIDTH), lambda k: (k, 0, 0)),
                   pl.BlockSpec((1, DN_CONV_K - 1, 3 * DN_WIDTH), lambda k: (2 * k // nt, 0, 0)),
                   pl.BlockSpec((1, DN_HEADS, hd, hd), lambda k: (2 * k // nt, 0, 0, 0))],
        out_shape=[jax.ShapeDtypeStruct((n_tiles // 2, 2 * tile, DN_WIDTH), BF16),
                   jax.ShapeDtypeStruct((b, DN_CONV_K - 1, 3 * DN_WIDTH), F32),
                   jax.ShapeDtypeStruct((b, DN_HEADS, hd, hd), F32)],
        scratch_shapes=[pltpu.VMEM((tile + QKV_HALO, 3 * DN_WIDTH), F32),
                        pltpu.VMEM((2, n_prob, 2 * chunk, hd), BF16),
                        pltpu.VMEM((2, n_prob, chunk, hd), BF16),
                        pltpu.VMEM((2, n_prob, chunk, 2 * hd), BF16),
                        pltpu.VMEM((2, n_prob, chunk, hd), F32),
                        pltpu.VMEM((2, n_prob, chunk, hd), BF16),
                        pltpu.VMEM((2, nchunk, chunk, DN_HEADS), F32),
                        pltpu.VMEM((2, nchunk, DN_HEADS, chunk), F32),
                        pltpu.VMEM((tile, DN_WIDTH), F32)],
        compiler_params=_cparams(("arbitrary",)),
        name="deltanet_pipelined",
    )(proj_t, proj_t, proj_t, bdc_t, bdc_t, bdc_t, bdr_t, bdr_t, bdr_t, proj_p, qkv_w, p_row, p_col, norm_w)
    return od.reshape(b, l, DN_WIDTH), qkv_state, s_final


def _mem_kernel(q_ref, g_ref, k_ref, v_ref, o_ref, *, nb):
    hd = MEM_HEAD_DIM
    scale = hd ** -0.5
    probs = [(i, h) for i in range(nb) for h in range(MEM_HEADS)]
    lanes = [slice(h * hd, (h + 1) * hd) for h in range(MEM_HEADS)]
    qs = [q_ref[i, :, lanes[h]] for i, h in probs]
    ks = [k_ref[i, pl.ds(h, MEM_TOKENS, stride=MEM_HEADS), :] for i, h in probs]
    vs = [v_ref[i, pl.ds(h, MEM_TOKENS, stride=MEM_HEADS), :] for i, h in probs]
    ss = [_bdot(q, k, _NT) * scale for q, k in zip(qs, ks)]
    es = [jnp.exp(s - jnp.max(s, axis=-1, keepdims=True)) for s in ss]
    oms = [_bdot(e, v) / jnp.sum(e, axis=-1, keepdims=True) for e, v in zip(es, vs)]
    for (i, h), om in zip(probs, oms):
        o_ref[i, :, lanes[h]] = (om * _silu(g_ref[i, :, lanes[h]])).astype(o_ref.dtype)


def _mem_group(proj3, mem_k, mem_v, tile, nb):
    b, l, _ = proj3.shape
    cb = COL_BLOCK
    rows = MEM_TOKENS * MEM_HEADS
    return pl.pallas_call(
        functools.partial(_mem_kernel, nb=nb),
        grid=(b // nb, l // tile),
        in_specs=[pl.BlockSpec((nb, tile, cb), lambda i, t: (i, t, MEMQ_BLK)),
                  pl.BlockSpec((nb, tile, cb), lambda i, t: (i, t, MEMG_BLK)),
                  pl.BlockSpec((nb, rows, MEM_HEAD_DIM), lambda i, t: (i, 0, 0)),
                  pl.BlockSpec((nb, rows, MEM_HEAD_DIM), lambda i, t: (i, 0, 0))],
        out_specs=pl.BlockSpec((nb, tile, MEM_WIDTH), lambda i, t: (i, t, 0)),
        out_shape=jax.ShapeDtypeStruct((b, l, MEM_WIDTH), BF16),
        compiler_params=_cparams(("arbitrary", "arbitrary")),
        name="memory_group",
    )(proj3, proj3, mem_k, mem_v)


def _out_kernel(oc_ref, od_ref, om_ref, x_ref, w_ref, p_ref, y_ref):
    half = x_ref.shape[0] // 2
    for r in range(2):
        rows = slice(r * half, (r + 1) * half)
        h = jnp.dot(oc_ref[rows, :], w_ref[0:C_CONV, :], preferred_element_type=F32)
        h = h + jnp.dot(od_ref[rows, :], w_ref[C_CONV:C_CONV + DN_WIDTH, :], preferred_element_type=F32)
        h = h + jnp.dot(om_ref[rows, :], w_ref[C_CONV + DN_WIDTH:, :], preferred_element_type=F32)
        z = DEEPNORM_ALPHA * x_ref[rows, :] + h
        mu = jnp.mean(z, axis=-1, keepdims=True)
        zc = z - mu
        var = jnp.mean(zc * zc, axis=-1, keepdims=True)
        y_ref[rows, :] = zc * lax.rsqrt(var + LN_EPS) * p_ref[0:1, :] + p_ref[1:2, :]


def _out_proj(oc, od, om, x2, w_out, ln_p, bm):
    m = x2.shape[0]
    return pl.pallas_call(
        _out_kernel,
        grid=(m // bm,),
        in_specs=[pl.BlockSpec((bm, C_CONV), lambda i: (i, 0)),
                  pl.BlockSpec((bm, DN_WIDTH), lambda i: (i, 0)),
                  pl.BlockSpec((bm, MEM_WIDTH), lambda i: (i, 0)),
                  pl.BlockSpec((bm, D_MODEL), lambda i: (i, 0)),
                  pl.BlockSpec((D_MODEL, D_MODEL), lambda i: (0, 0)),
                  pl.BlockSpec((8, D_MODEL), lambda i: (0, 0))],
        out_specs=pl.BlockSpec((bm, D_MODEL), lambda i: (i, 0)),
        out_shape=jax.ShapeDtypeStruct((m, D_MODEL), F32),
        compiler_params=_cparams(("arbitrary",)),
        name="out_proj_ln",
    )(oc, od, om, x2, w_out, ln_p)


def _pad_rows(rows, n):
    a = jnp.stack(rows).astype(F32)
    return jnp.pad(a, ((0, n - a.shape[0]), (0, 0)))


def _project(x, w_in_t, layer, bm, dn_chunk):
    b, l, _ = x.shape
    proj, bdc, bdr = _in_proj(x.reshape(b * l, D_MODEL), w_in_t, layer, bm)
    bdr4 = bdr.reshape(2 * DN_HEADS, b, l // dn_chunk, dn_chunk).transpose(1, 2, 0, 3)
    return proj.reshape(b, l, N_MAIN), bdc.reshape(b, l, SMALL_PAD), bdr4


def _mix(x, oc, od, om, w_out, ln_p, out_bm):
    b, l, _ = x.shape
    y = _out_proj(oc.reshape(b * l, C_CONV), od.reshape(b * l, DN_WIDTH), om.reshape(b * l, MEM_WIDTH),
                  x.reshape(b * l, D_MODEL), w_out, ln_p, out_bm)
    return y.reshape(b, l, D_MODEL)


def kernel(x_prompt, x_sample, mem_prompt, state_conv, state_qkv_conv, state_delta, cache_mem_k, cache_mem_v,
           w_in, conv_w, conv_b, conv_ln_g, conv_ln_b, qkv_conv_w, a_log, dt_bias, delta_norm_w,
           w_mem_k, w_mem_v, w_out, ln_g, ln_b):
    bp = x_prompt.shape[0]
    bs = x_sample.shape[0]
    hp, hs = x_prompt, x_sample
    outs = [[] for _ in range(8)]
    w_in_t = jnp.swapaxes(w_in, 1, 2)
    kv_rows = MEM_TOKENS * MEM_HEADS
    ls = x_sample.shape[1]
    prompt_chunk = 64
    for i in range(DEPTH):
        conv_p = _pad_rows([conv_b[i], conv_ln_g[i], conv_ln_b[i]], 8)
        p_row = jnp.stack([a_log[i], dt_bias[i]]).astype(F32)
        p_col = p_row.T
        norm_w = delta_norm_w[i].reshape(1, DN_HEAD_DIM)
        w_out_i = w_out[i].astype(BF16)
        ln_p = _pad_rows([ln_g[i], ln_b[i]], 8)

        w_kv = jnp.concatenate([w_mem_k[i], w_mem_v[i]], axis=1).astype(BF16)
        kv = _matmul(mem_prompt.reshape(bp * MEM_TOKENS, D_MODEL), w_kv, 512, 512)
        mk = kv[:, :MEM_WIDTH].reshape(bp, MEM_TOKENS, MEM_HEADS, MEM_HEAD_DIM)
        mv = kv[:, MEM_WIDTH:].reshape(bp, MEM_TOKENS, MEM_HEADS, MEM_HEAD_DIM)
        sk = cache_mem_k[i].reshape(bs, kv_rows, MEM_HEAD_DIM)
        sv = cache_mem_v[i].reshape(bs, kv_rows, MEM_HEAD_DIM)

        proj_p, bdc_p, bdr_p = _project(hp, w_in_t, i, 2048, prompt_chunk)
        proj_s, bdc_s, bdr_s = _project(hs, w_in_t, i, bs * ls, ls)

        oc_p, conv_st_p, om_s = _conv_group(proj_p, None, i, conv_w[i], conv_p, 512, 1,
                                            guest=(proj_s, sk, sv, 8))
        od_p, qkv_st_p, s_p = _dn_group_pipelined(proj_p, bdc_p, bdr_p, qkv_conv_w[i], p_row, p_col, norm_w,
                                                  256, prompt_chunk)
        om_p = _mem_group(proj_p, mk.reshape(bp, kv_rows, MEM_HEAD_DIM), mv.reshape(bp, kv_rows, MEM_HEAD_DIM),
                          512, 1)
        hp_next = _mix(hp, oc_p, od_p, om_p, w_out_i, ln_p, 512)
        outs[0].append(conv_st_p); outs[1].append(qkv_st_p); outs[2].append(s_p)
        outs[3].append(mk)
        outs[4].append(mv)

        oc_s, conv_st_s = _conv_group(proj_s, state_conv, i, conv_w[i], conv_p, ls, 8)
        od_s, qkv_st_s, s_s = _dn_group(proj_s, bdc_s, bdr_s, state_qkv_conv, state_delta, i, qkv_conv_w[i],
                                        p_row, p_col, norm_w, ls, ls, 8)
        hs_next = _mix(hs, oc_s, od_s, om_s, w_out_i, ln_p, 512)
        outs[5].append(conv_st_s); outs[6].append(qkv_st_s); outs[7].append(s_s)
        hp, hs = hp_next, hs_next
    return (hp, hs) + tuple(jnp.stack(o) for o in outs)
```

```python
import functools

import jax
import jax.numpy as jnp
from jax import lax
from jax.experimental import pallas as pl
from jax.experimental.pallas import tpu as pltpu

F32 = jnp.float32
BF16 = jnp.bfloat16

D_MODEL = 2048
C_CONV = 512
CONV_K = 31
DN_HEADS = 8
DN_HEAD_DIM = 128
DN_WIDTH = DN_HEADS * DN_HEAD_DIM
DN_CONV_K = 4
MEM_HEADS = 4
MEM_HEAD_DIM = 128
MEM_WIDTH = MEM_HEADS * MEM_HEAD_DIM
MEM_TOKENS = 256
DEPTH = 1
DEEPNORM_ALPHA = (2 * DEPTH) ** 0.25
LN_EPS = 1e-5
NORM_EPS = 1e-6

N_MAIN = 3 * DN_WIDTH + DN_WIDTH + 3 * C_CONV + 2 * MEM_WIDTH
COL_BLOCK = 512
QKV_BLK = 0
DNGATE_BLK = 3
GLUA_BLK, GLUB_BLK, CGATE_BLK, MEMQ_BLK, MEMG_BLK = 8, 9, 10, 11, 12
SMALL_PAD = 128
N_DIRECT_BLOCKS = (3 * C_CONV + 4 * DN_WIDTH) // COL_BLOCK
CONV_HALO = 32
QKV_HALO = 8
SUBLANES = 8
CONV_ROW_BLOCK = 64
PARAM_ROWS = SUBLANES

PROMPT_PROJ_ROWS = 2048
PROMPT_CONV_TILE = 512
PROMPT_DN_TILE = 256
PROMPT_DN_CHUNK = 64
PROMPT_MEM_TILE = 512
OUT_ROWS = 512
KV_PROJ_BLOCK = 512
SAMPLE_SEQS = 8

VMEM_LIMIT = 56 * 1024 * 1024


def _sigmoid(x):
    return 1.0 / (1.0 + jnp.exp(-x))


def _silu(x):
    return x * _sigmoid(x)


def _softplus(x):
    return jnp.maximum(x, 0.0) + jnp.log(1.0 + jnp.exp(-jnp.abs(x)))


def _cparams(sem):
    return pltpu.CompilerParams(dimension_semantics=sem, vmem_limit_bytes=VMEM_LIMIT)


def _proj_kernel(x_hbm, wa_ref, wb_ref, o_ref, bdc_ref, bdr_ref, xf_ref, xb_ref, sem):
    i = pl.program_id(0)
    j = pl.program_id(1)
    bm = xf_ref.shape[0]
    nt = (((1,), (1,)), ((), ()))
    n_bd = 2 * DN_HEADS

    def x_copy(block):
        return pltpu.make_async_copy(x_hbm.at[pl.ds(block * bm, bm), :], xf_ref, sem)

    @pl.when(j == 0)
    def _():
        @pl.when(i == 0)
        def _():
            x_copy(0).start()

        x_copy(i).wait()
        xb_ref[...] = xf_ref[...].astype(BF16)

        @pl.when(i + 1 < pl.num_programs(0))
        def _():
            x_copy(i + 1).start()

    @pl.when(j < N_DIRECT_BLOCKS)
    def _():
        o_ref[...] = lax.dot_general(xb_ref[...], wa_ref[...].astype(BF16), nt, preferred_element_type=F32)

    @pl.when(j >= N_DIRECT_BLOCKS)
    def _():
        w = jnp.concatenate([wa_ref[n_bd:, :], wb_ref[...]], axis=0).astype(BF16)
        o_ref[...] = lax.dot_general(xb_ref[...], w, nt, preferred_element_type=F32)

    @pl.when(j == N_DIRECT_BLOCKS)
    def _():
        bd = lax.dot_general(xb_ref[...], wa_ref[0:SMALL_PAD, :].astype(BF16), nt, preferred_element_type=F32)
        bdc_ref[...] = bd
        bdr_ref[...] = bd.T[0:n_bd, :]


def _proj_out_block(j):
    n_conv = 3 * C_CONV // COL_BLOCK
    return jnp.where(j < n_conv, j + GLUA_BLK, jnp.where(j < N_DIRECT_BLOCKS, j - n_conv, j))


def _in_proj(x2, w_in_t, layer, bm):
    m = x2.shape[0]
    bn = COL_BLOCK
    n_blocks = N_MAIN // bn
    n_bd = 2 * DN_HEADS
    return pl.pallas_call(
        _proj_kernel,
        grid=(m // bm, n_blocks),
        in_specs=[
            pl.BlockSpec(memory_space=pl.ANY),
            pl.BlockSpec((None, bn, D_MODEL), lambda i, j: (layer, j, 0)),
            pl.BlockSpec((None, n_bd, D_MODEL),
                         lambda i, j: (layer, (jnp.maximum(j + 1, n_blocks - 1)) * (bn // n_bd), 0)),
        ],
        out_specs=[
            pl.BlockSpec((bm, bn), lambda i, j: (i, _proj_out_block(j))),
            pl.BlockSpec((bm, SMALL_PAD), lambda i, j: (i, 0)),
            pl.BlockSpec((2 * DN_HEADS, bm), lambda i, j: (0, i)),
        ],
        out_shape=[
            jax.ShapeDtypeStruct((m, N_MAIN), F32),
            jax.ShapeDtypeStruct((m, SMALL_PAD), F32),
            jax.ShapeDtypeStruct((2 * DN_HEADS, m), F32),
        ],
        scratch_shapes=[pltpu.VMEM((bm, D_MODEL), F32), pltpu.VMEM((bm, D_MODEL), BF16),
                        pltpu.SemaphoreType.DMA(())],
        compiler_params=_cparams(("arbitrary", "arbitrary")),
        name="in_proj",
    )(x2, w_in_t, w_in_t)


def _mm_kernel(x_ref, w_ref, o_ref):
    o_ref[...] = jnp.dot(x_ref[...].astype(BF16), w_ref[...], preferred_element_type=F32)


def _matmul(x2, w, bm, bn):
    m, k = x2.shape
    n = w.shape[1]
    return pl.pallas_call(
        _mm_kernel,
        grid=(m // bm, n // bn),
        in_specs=[pl.BlockSpec((bm, k), lambda i, j: (i, 0)),
                  pl.BlockSpec((k, bn), lambda i, j: (0, j))],
        out_specs=pl.BlockSpec((bm, bn), lambda i, j: (i, j)),
        out_shape=jax.ShapeDtypeStruct((m, n), F32),
        compiler_params=_cparams(("arbitrary", "arbitrary")),
        name="mem_kv_proj",
    )(x2, w)


def _conv_kernel(*refs, tile, nb, has_state, guest_nb):
    refs = list(refs)
    a_ref, b_ref, g_ref = refs[:3]
    del refs[:3]
    st_ref = refs.pop(0) if has_state else None
    w_ref, p_ref = refs[:2]
    del refs[:2]
    if guest_nb:
        gq_ref, gg_ref, gk_ref, gv_ref = refs[:4]
        del refs[:4]
        o_ref, so_ref, go_ref, xp_ref = refs
        _mem_kernel(gq_ref, gg_ref, gk_ref, gv_ref, go_ref, nb=guest_nb)
    else:
        o_ref, so_ref, xp_ref = refs
    t = pl.program_id(1)
    last = pl.num_programs(1) - 1
    lo = CONV_HALO - (CONV_K - 1)
    n = tile + CONV_HALO
    rb = min(CONV_ROW_BLOCK, tile)
    w = w_ref[...]
    conv_b = p_ref[0:1, :]
    ln_g = p_ref[1:2, :]
    ln_b = p_ref[2:3, :]
    for i in range(nb):
        @pl.when(t == 0)
        def _():
            xp_ref[i, 0, 0:CONV_HALO, :] = jnp.zeros((CONV_HALO, C_CONV), F32)
            if has_state:
                xp_ref[i, 0, lo:CONV_HALO, :] = st_ref[:, i, :]

        xp_ref[i, 0, CONV_HALO:n, :] = a_ref[i] * _sigmoid(b_ref[i])
        xp = xp_ref[i, 0]
        for s in range(1, SUBLANES):
            xp_ref[i, s] = pltpu.roll(xp, n - s, axis=0)

        for r in range(tile // rb):
            acc = None
            for j in range(CONV_K):
                s = (lo + j) % SUBLANES
                start = lo + j - s + r * rb
                term = xp_ref[i, s, start:start + rb, :] * w[j:j + 1, :]
                acc = term if acc is None else acc + term
            hc = acc + conv_b
            mu = jnp.mean(hc, axis=-1, keepdims=True)
            xc = hc - mu
            var = jnp.mean(xc * xc, axis=-1, keepdims=True)
            hn = xc * lax.rsqrt(var + LN_EPS) * ln_g + ln_b
            rows = slice(r * rb, (r + 1) * rb)
            o_ref[i, rows, :] = (_silu(hn) * _silu(g_ref[i, rows, :])).astype(o_ref.dtype)

        @pl.when(t == last)
        def _():
            so_ref[i] = xp_ref[i, 0, tile + lo:n, :]

        xp_ref[i, 0, 0:CONV_HALO, :] = xp_ref[i, 0, tile:n, :]


def _conv_group(proj3, state, layer, conv_w, conv_p, tile, nb, guest=None):
    b, l, _ = proj3.shape
    has_state = state is not None
    cb = COL_BLOCK
    nrow = CONV_K - 1
    nt = l // tile
    in_specs = [
        pl.BlockSpec((nb, tile, cb), lambda i, t: (i, t, GLUA_BLK)),
        pl.BlockSpec((nb, tile, cb), lambda i, t: (i, t, GLUB_BLK)),
        pl.BlockSpec((nb, tile, cb), lambda i, t: (i, t, CGATE_BLK)),
    ]
    args = [proj3, proj3, proj3]
    if has_state:
        in_specs.append(pl.BlockSpec((None, nrow, nb, C_CONV), lambda i, t: (layer, 0, i, 0)))
        args.append(jnp.swapaxes(state, 1, 2))
    in_specs += [pl.BlockSpec((CONV_K, C_CONV), lambda i, t: (0, 0)),
                 pl.BlockSpec((PARAM_ROWS, C_CONV), lambda i, t: (0, 0))]
    args += [conv_w, conv_p]
    out_specs = [pl.BlockSpec((nb, tile, C_CONV), lambda i, t: (i, t, 0)),
                 pl.BlockSpec((nb, nrow, C_CONV), lambda i, t: (i, 0, 0))]
    out_shape = [jax.ShapeDtypeStruct((b, l, C_CONV), BF16),
                 jax.ShapeDtypeStruct((b, nrow, C_CONV), F32)]
    guest_nb = 0
    if guest is not None:
        g_proj3, g_k, g_v, guest_nb = guest
        gb, gl, _ = g_proj3.shape
        assert gb // guest_nb == (b // nb) * nt
        kv_rows = MEM_TOKENS * MEM_HEADS
        step = lambda i, t: i * nt + t
        in_specs += [pl.BlockSpec((guest_nb, gl, cb), lambda i, t: (step(i, t), 0, MEMQ_BLK)),
                     pl.BlockSpec((guest_nb, gl, cb), lambda i, t: (step(i, t), 0, MEMG_BLK)),
                     pl.BlockSpec((guest_nb, kv_rows, MEM_HEAD_DIM), lambda i, t: (step(i, t), 0, 0)),
                     pl.BlockSpec((guest_nb, kv_rows, MEM_HEAD_DIM), lambda i, t: (step(i, t), 0, 0))]
        args += [g_proj3, g_proj3, g_k, g_v]
        out_specs.append(pl.BlockSpec((guest_nb, gl, MEM_WIDTH), lambda i, t: (step(i, t), 0, 0)))
        out_shape.append(jax.ShapeDtypeStruct((gb, gl, MEM_WIDTH), BF16))
    return pl.pallas_call(
        functools.partial(_conv_kernel, tile=tile, nb=nb, has_state=has_state, guest_nb=guest_nb),
        grid=(b // nb, nt),
        in_specs=in_specs,
        out_specs=out_specs,
        out_shape=out_shape,
        scratch_shapes=[pltpu.VMEM((nb, SUBLANES, tile + CONV_HALO, C_CONV), F32)],
        compiler_params=_cparams(("arbitrary", "arbitrary")),
        name="conv_group",
    )(*args)


def _bdot(a, b, dims=None):
    a = a.astype(BF16)
    b = b.astype(BF16)
    if dims is None:
        return jnp.dot(a, b, preferred_element_type=F32)
    return lax.dot_general(a, b, (dims, ((), ())), preferred_element_type=F32)


_NT = ((1,), (1,))
_TN = ((0,), (0,))


def _split_bf16(a):
    hi = a.astype(BF16)
    return hi, (a - hi.astype(F32)).astype(BF16)


def _unit_lower_inverses(ms, c):
    row = lax.broadcasted_iota(jnp.int32, (c, c), 0)
    col = lax.broadcasted_iota(jnp.int32, (c, c), 1)
    eye = (row == col).astype(F32)
    ps = [-m for m in ms]
    ts = [eye + p for p in ps]
    span = 2
    while span < c:
        ps = [_bdot(p, p) for p in ps]
        ts = [t + _bdot(t, p) for t, p in zip(ts, ps)]
        span *= 2
    msp = [_split_bf16(m) for m in ms]
    tsp = [_split_bf16(t) for t in ts]
    mts = [_bdot(mh, th) + _bdot(mh, tl) + _bdot(ml, th) for (mh, ml), (th, tl) in zip(msp, tsp)]
    rs = [eye - t - mt for t, mt in zip(ts, mts)]
    return [t + _bdot(th, r) for t, (th, _), r in zip(ts, tsp, rs)]


def _dn_kernel(qkv_ref, gate_ref, bdc_ref, bdr_ref, prev_ref, s0_ref, w_ref, pr_ref, pc_ref, nw_ref,
               o_ref, st_ref, s_ref, xp_ref, xs_ref, osc_ref, *, tile, chunk, nb):
    c = chunk
    nchunk = tile // c
    hd = DN_HEAD_DIM
    t = pl.program_id(1)
    last = pl.num_programs(1) - 1
    lo = QKV_HALO - (DN_CONV_K - 1)
    hi = lax.Precision.HIGHEST
    w = w_ref[...]

    for i in range(nb):
        @pl.when(t == 0)
        def _():
            xp_ref[i, 0:QKV_HALO, :] = jnp.zeros((QKV_HALO, 3 * DN_WIDTH), F32)
            xp_ref[i, lo:QKV_HALO, :] = prev_ref[:, i, :]
            s_ref[i] = s0_ref[i]

        xp_ref[i, QKV_HALO:QKV_HALO + tile, :] = qkv_ref[i]
        xp = xp_ref[i]
        for j in range(DN_CONV_K - 1):
            xs_ref[i, j] = pltpu.roll(xp, tile + QKV_HALO - (lo + j), axis=0)

        @pl.when(t == last)
        def _():
            st_ref[i] = xp_ref[i, tile + lo:tile + QKV_HALO, :]

    neg_a_row = -jnp.exp(pr_ref[0:1, :])
    dtb_row = pr_ref[1:2, :]
    neg_a_col = -jnp.exp(pc_ref[:, 0:1])
    dtb_col = pc_ref[:, 1:2]
    norm_w = nw_ref[...]

    row = lax.broadcasted_iota(jnp.int32, (c, c), 0)
    col = lax.broadcasted_iota(jnp.int32, (c, c), 1)
    incl = row >= col
    strict = row > col
    tri_l = incl.astype(F32)
    tri_u = (row <= col).astype(F32)

    seq_chunks = [(i, n) for i in range(nb) for n in range(nchunk)]
    probs = [(i, n, h) for i, n in seq_chunks for h in range(DN_HEADS)]
    beta, gcum_col, gcum_row, prep = {}, {}, {}, {}
    for i, n in seq_chunks:
        rows = slice(n * c, (n + 1) * c)
        bd = bdc_ref[i, rows, :]
        beta[i, n] = _sigmoid(bd[:, 0:DN_HEADS])
        g_col = neg_a_row * _softplus(bd[:, DN_HEADS:2 * DN_HEADS] + dtb_row)
        gcum_col[i, n] = jnp.dot(tri_l, g_col, precision=hi, preferred_element_type=F32)
        br = bdr_ref[i, n]
        g_row = neg_a_col * _softplus(br[DN_HEADS:2 * DN_HEADS, :] + dtb_col)
        gcum_row[i, n] = jnp.dot(g_row, tri_u, precision=hi, preferred_element_type=F32)
        acc = xp_ref[i, QKV_HALO + n * c:QKV_HALO + (n + 1) * c, :] * w[DN_CONV_K - 1:DN_CONV_K, :]
        for j in range(DN_CONV_K - 1):
            acc = acc + xs_ref[i, j, rows, :] * w[j:j + 1, :]
        qkvc = _silu(acc)
        for h in range(DN_HEADS):
            q, k, v = (qkvc[:, part * DN_WIDTH + h * hd:part * DN_WIDTH + (h + 1) * hd] for part in range(3))
            q = q * (lax.rsqrt(jnp.sum(q * q, axis=-1, keepdims=True) + NORM_EPS) * (hd ** -0.5))
            k = k * lax.rsqrt(jnp.sum(k * k, axis=-1, keepdims=True) + NORM_EPS)
            gc = gcum_col[i, n][:, h:h + 1]
            gr = gcum_row[i, n][h:h + 1, :]
            bt = beta[i, n][:, h:h + 1]
            gl = gcum_col[i, n][c - 1:c, h:h + 1]
            e = jnp.exp(gc)
            kb = k * bt
            prep[i, n, h] = (
                jnp.concatenate([kb, q], axis=0).astype(BF16),
                k.astype(BF16),
                jnp.exp(jnp.where(incl, gc - gr, -jnp.inf)),
                jnp.concatenate([v * bt, kb * e], axis=1).astype(BF16),
                q * e,
                (k * jnp.exp(gl - gc)).astype(BF16),
                jnp.exp(gl))

    preps = [prep[p] for p in probs]
    kqs = [_bdot(p[0], p[1], _NT) for p in preps]
    aqks = [(kq[c:] * p[2]).astype(BF16) for kq, p in zip(kqs, preps)]
    ms = [jnp.where(strict, kq[:c] * p[2], 0.0) for kq, p in zip(kqs, preps)]
    tinvs = _unit_lower_inverses(ms, c)
    uws = [_bdot(tinv, p[3]) for tinv, p in zip(tinvs, preps)]
    wqs = [jnp.concatenate([uw[:, hd:], p[4]], axis=0).astype(BF16) for uw, p in zip(uws, preps)]
    local = dict(zip(probs, zip(uws, wqs, aqks, [p[5] for p in preps], [p[6] for p in preps])))

    heads = [(i, h) for i in range(nb) for h in range(DN_HEADS)]
    state = {(i, h): s_ref[i, h] for i, h in heads}
    for n in range(nchunk):
        rows = slice(n * c, (n + 1) * c)
        items = [(i, h) + local[i, n, h] for i, h in heads]
        s16s = [state[i, h].astype(BF16) for i, h in heads]
        wss = [_bdot(it[3], s16) for it, s16 in zip(items, s16s)]
        u16s = [(it[2][:, :hd] - ws[:c]).astype(BF16) for it, ws in zip(items, wss)]
        outs = [ws[c:] + _bdot(it[4], u16) for it, ws, u16 in zip(items, wss, u16s)]
        for it, u16 in zip(items, u16s):
            i, h = it[0], it[1]
            state[i, h] = state[i, h] * it[6] + _bdot(it[5], u16, _TN)
        for it, o in zip(items, outs):
            i, h = it[0], it[1]
            o = o * lax.rsqrt(jnp.mean(o * o, axis=-1, keepdims=True) + NORM_EPS) * norm_w
            osc_ref[i, rows, h * hd:(h + 1) * hd] = o
    for i, h in heads:
        s_ref[i, h] = state[i, h]

    for i in range(nb):
        xp_ref[i, 0:QKV_HALO, :] = xp_ref[i, tile:tile + QKV_HALO, :]
        o_ref[i] = (osc_ref[i] * _silu(gate_ref[i])).astype(o_ref.dtype)


def _dn_group(proj3, bdc3, bdr4, prev, s0, layer, qkv_w, p_row, p_col, norm_w, tile, chunk, nb):
    b, l, _ = proj3.shape
    nchunk = tile // chunk
    hd = DN_HEAD_DIM
    nrow = DN_CONV_K - 1
    const2 = lambda i, t: (0, 0)
    od, qkv_state, s_final = pl.pallas_call(
        functools.partial(_dn_kernel, tile=tile, chunk=chunk, nb=nb),
        grid=(b // nb, l // tile),
        in_specs=[
            pl.BlockSpec((nb, tile, 3 * DN_WIDTH), lambda i, t: (i, t, QKV_BLK)),
            pl.BlockSpec((nb, tile, DN_WIDTH), lambda i, t: (i, t, DNGATE_BLK)),
            pl.BlockSpec((nb, tile, SMALL_PAD), lambda i, t: (i, t, 0)),
            pl.BlockSpec((nb, nchunk, 2 * DN_HEADS, chunk), lambda i, t: (i, t, 0, 0)),
            pl.BlockSpec((None, nrow, nb, 3 * DN_WIDTH), lambda i, t: (layer, 0, i, 0)),
            pl.BlockSpec((None, nb, DN_HEADS, hd, hd), lambda i, t: (layer, i, 0, 0, 0)),
            pl.BlockSpec((DN_CONV_K, 3 * DN_WIDTH), const2),
            pl.BlockSpec((2, DN_HEADS), const2),
            pl.BlockSpec((DN_HEADS, 2), const2),
            pl.BlockSpec((1, hd), const2),
        ],
        out_specs=[pl.BlockSpec((nb, tile, DN_WIDTH), lambda i, t: (i, t, 0)),
                   pl.BlockSpec((nb, nrow, 3 * DN_WIDTH), lambda i, t: (i, 0, 0)),
                   pl.BlockSpec((nb, DN_HEADS, hd, hd), lambda i, t: (i, 0, 0, 0))],
        out_shape=[jax.ShapeDtypeStruct((b, l, DN_WIDTH), BF16),
                   jax.ShapeDtypeStruct((b, nrow, 3 * DN_WIDTH), F32),
                   jax.ShapeDtypeStruct((b, DN_HEADS, hd, hd), F32)],
        scratch_shapes=[pltpu.VMEM((nb, tile + QKV_HALO, 3 * DN_WIDTH), F32),
                        pltpu.VMEM((nb, DN_CONV_K - 1, tile + QKV_HALO, 3 * DN_WIDTH), F32),
                        pltpu.VMEM((nb, tile, DN_WIDTH), F32)],
        compiler_params=_cparams(("arbitrary", "arbitrary")),
        name="deltanet_group",
    )(proj3, proj3, bdc3, bdr4, jnp.swapaxes(prev, 1, 2), s0, qkv_w, p_row, p_col, norm_w)
    return od, qkv_state, s_final


def _dn_pipelined_kernel(q0_ref, qa_ref, qb_ref, c0_ref, ca_ref, cb_ref, r0_ref, ra_ref, rb_ref, gate_ref,
                         w_ref, pr_ref, pc_ref, nw_ref, o_ref, st_ref, s_ref,
                         xp_ref, kq_ref, k16_ref, uwr_ref, qe_ref, kd_ref, gcc_ref, gcr_ref, osc_ref,
                         *, tile, chunk, nt, n_tiles):
    c = chunk
    nchunk = tile // c
    hd = DN_HEAD_DIM
    lo = QKV_HALO - (DN_CONV_K - 1)
    hi = lax.Precision.HIGHEST
    k = pl.program_id(0)
    w = w_ref[...]
    neg_a_row = -jnp.exp(pr_ref[0:1, :])
    dtb_row = pr_ref[1:2, :]
    neg_a_col = -jnp.exp(pc_ref[:, 0:1])
    dtb_col = pc_ref[:, 1:2]
    norm_w = nw_ref[...]
    row = lax.broadcasted_iota(jnp.int32, (c, c), 0)
    col = lax.broadcasted_iota(jnp.int32, (c, c), 1)
    incl = row >= col
    strict = row > col
    tri_l = incl.astype(F32)
    tri_u = (row <= col).astype(F32)
    probs = [(n, h) for n in range(nchunk) for h in range(DN_HEADS)]

    def prepare(buf, q_ref, c_ref, r_ref, first, write_state):
        halo = xp_ref[0:QKV_HALO, :]
        xp_ref[0:QKV_HALO, :] = jnp.zeros_like(halo) if first is True else jnp.where(first, 0.0, halo)
        xp_ref[QKV_HALO:QKV_HALO + tile, :] = q_ref[0]
        if write_state:
            st_ref[0] = xp_ref[tile + lo:tile + QKV_HALO, :]
        for n in range(nchunk):
            rows = slice(n * c, (n + 1) * c)
            bd = c_ref[0, rows, :]
            beta = _sigmoid(bd[:, 0:DN_HEADS])
            g_col = neg_a_row * _softplus(bd[:, DN_HEADS:2 * DN_HEADS] + dtb_row)
            gcum_col = jnp.dot(tri_l, g_col, precision=hi, preferred_element_type=F32)
            br = r_ref[0, n]
            g_row = neg_a_col * _softplus(br[DN_HEADS:2 * DN_HEADS, :] + dtb_col)
            gcc_ref[buf, n] = gcum_col
            gcr_ref[buf, n] = jnp.dot(g_row, tri_u, precision=hi, preferred_element_type=F32)
            window = xp_ref[n * c:(n + 1) * c + QKV_HALO, :]
            acc = window[QKV_HALO:, :] * w[DN_CONV_K - 1:DN_CONV_K, :]
            for j in range(DN_CONV_K - 1):
                acc = acc + pltpu.roll(window, c + QKV_HALO - (lo + j), axis=0)[0:c, :] * w[j:j + 1, :]
            qkvc = _silu(acc)
            for h in range(DN_HEADS):
                p = n * DN_HEADS + h
                q, kk, v = (qkvc[:, part * DN_WIDTH + h * hd:part * DN_WIDTH + (h + 1) * hd] for part in range(3))
                q = q * (lax.rsqrt(jnp.sum(q * q, axis=-1, keepdims=True) + NORM_EPS) * (hd ** -0.5))
                kk = kk * lax.rsqrt(jnp.sum(kk * kk, axis=-1, keepdims=True) + NORM_EPS)
                gc = gcum_col[:, h:h + 1]
                bt = beta[:, h:h + 1]
                gl = gcum_col[c - 1:c, h:h + 1]
                e = jnp.exp(gc)
                kb = kk * bt
                kq_ref[buf, p] = jnp.concatenate([kb, q], axis=0).astype(BF16)
                k16_ref[buf, p] = kk.astype(BF16)
                uwr_ref[buf, p] = jnp.concatenate([v * bt, kb * e], axis=1).astype(BF16)
                qe_ref[buf, p] = q * e
                kd_ref[buf, p] = (kk * jnp.exp(gl - gc)).astype(BF16)
        xp_ref[0:QKV_HALO, :] = xp_ref[tile:tile + QKV_HALO, :]

    def finish(buf, half):
        idx = [n * DN_HEADS + h for n, h in probs]
        kqs = [_bdot(kq_ref[buf, p], k16_ref[buf, p], _NT) for p in idx]
        gcs = [gcc_ref[buf, n][:, h:h + 1] for n, h in probs]
        decays = [jnp.exp(jnp.where(incl, gc - gcr_ref[buf, n][h:h + 1, :], -jnp.inf))
                  for gc, (n, h) in zip(gcs, probs)]
        aqks = [(kq[c:] * dc).astype(BF16) for kq, dc in zip(kqs, decays)]
        ms = [jnp.where(strict, kq[:c] * dc, 0.0) for kq, dc in zip(kqs, decays)]
        tinvs = _unit_lower_inverses(ms, c)
        uws = [_bdot(tinv, uwr_ref[buf, p]) for tinv, p in zip(tinvs, idx)]
        wqs = [jnp.concatenate([uw[:, hd:], qe_ref[buf, p]], axis=0).astype(BF16) for uw, p in zip(uws, idx)]
        egls = [jnp.exp(gc[c - 1:c, :]) for gc in gcs]
        local = dict(zip(probs, zip(uws, wqs, aqks, egls)))
        state = [s_ref[0, h] for h in range(DN_HEADS)]
        for n in range(nchunk):
            rows = slice(n * c, (n + 1) * c)
            items = [local[n, h] for h in range(DN_HEADS)]
            s16s = [s.astype(BF16) for s in state]
            wss = [_bdot(it[1], s16) for it, s16 in zip(items, s16s)]
            u16s = [(it[0][:, :hd] - ws[:c]).astype(BF16) for it, ws in zip(items, wss)]
            outs = [ws[c:] + _bdot(it[2], u16) for it, ws, u16 in zip(items, wss, u16s)]
            state = [s * it[3] + _bdot(kd_ref[buf, n * DN_HEADS + h], u16, _TN)
                     for h, (s, it, u16) in enumerate(zip(state, items, u16s))]
            for h, o in enumerate(outs):
                o = o * lax.rsqrt(jnp.mean(o * o, axis=-1, keepdims=True) + NORM_EPS) * norm_w
                osc_ref[rows, h * hd:(h + 1) * hd] = o
        for h in range(DN_HEADS):
            s_ref[0, h] = state[h]
        rows_out = slice(half * tile, (half + 1) * tile)
        o_ref[0, rows_out, :] = (osc_ref[...] * _silu(gate_ref[0, rows_out, :])).astype(o_ref.dtype)

    @pl.when(k == 0)
    def _():
        prepare(0, q0_ref, c0_ref, r0_ref, True, False)

    @pl.when(lax.rem(2 * k, nt) == 0)
    def _():
        s_ref[0] = jnp.zeros((DN_HEADS, hd, hd), F32)

    prepare(1, qa_ref, ca_ref, ra_ref, False, True)
    finish(0, 0)
    prepare(0, qb_ref, cb_ref, rb_ref, lax.rem(jnp.minimum(2 * k + 2, n_tiles - 1), nt) == 0, False)
    finish(1, 1)


def _dn_group_pipelined(proj3, bdc3, bdr4, qkv_w, p_row, p_col, norm_w, tile, chunk):
    b, l, _ = proj3.shape
    nt = l // tile
    assert nt % 2 == 0
    n_tiles = b * nt
    nchunk = tile // chunk
    hd = DN_HEAD_DIM
    n_prob = nchunk * DN_HEADS
    proj_t = proj3.reshape(n_tiles, tile, N_MAIN)
    proj_p = proj3.reshape(n_tiles // 2, 2 * tile, N_MAIN)
    bdc_t = bdc3.reshape(n_tiles, tile, SMALL_PAD)
    bdr_t = bdr4.reshape(n_tiles, nchunk, 2 * DN_HEADS, chunk)
    odd = lambda k: 2 * k + 1
    nxt = lambda k: jnp.minimum(2 * k + 2, n_tiles - 1)
    once = pl.Buffered(1)

    def tile_specs(index):
        mode = once if index is None else None
        at = (lambda k: 0) if index is None else index
        return [pl.BlockSpec((1, tile, 3 * DN_WIDTH), lambda k: (at(k), 0, QKV_BLK), pipeline_mode=mode),
                pl.BlockSpec((1, tile, SMALL_PAD), lambda k: (at(k), 0, 0), pipeline_mode=mode),
                pl.BlockSpec((1, nchunk, 2 * DN_HEADS, chunk), lambda k: (at(k), 0, 0, 0), pipeline_mode=mode)]

    (q0, c0, r0), (qa, ca, ra), (qb, cb, rb) = tile_specs(None), tile_specs(odd), tile_specs(nxt)
    const2 = lambda k: (0, 0)
    od, qkv_state, s_final = pl.pallas_call(
        functools.partial(_dn_pipelined_kernel, tile=tile, chunk=chunk, nt=nt, n_tiles=n_tiles),
        grid=(n_tiles // 2,),
        in_specs=[q0, qa, qb, c0, ca, cb, r0, ra, rb,
                  pl.BlockSpec((1, 2 * tile, DN_WIDTH), lambda k: (k, 0, DNGATE_BLK)),
                  pl.BlockSpec((DN_CONV_K, 3 * DN_WIDTH), const2),
                  pl.BlockSpec((2, DN_HEADS), const2),
                  pl.BlockSpec((DN_HEADS, 2), const2),
                  pl.BlockSpec((1, hd), const2)],
        out_specs=[pl.BlockSpec((1, 2 * tile, DN_WIDTH), lambda k: (k, 0, 0)),
                   pl.BlockSpec((1, DN_CONV_K - 1, 3 * DN_WIDTH), lambda k: (2 * k // nt, 0, 0)),
                   pl.BlockSpec((1, DN_HEADS, hd, hd), lambda k: (2 * k // nt, 0, 0, 0))],
        out_shape=[jax.ShapeDtypeStruct((n_tiles // 2, 2 * tile, DN_WIDTH), BF16),
                   jax.ShapeDtypeStruct((b, DN_CONV_K - 1, 3 * DN_WIDTH), F32),
                   jax.ShapeDtypeStruct((b, DN_HEADS, hd, hd), F32)],
        scratch_shapes=[pltpu.VMEM((tile + QKV_HALO, 3 * DN_WIDTH), F32),
                        pltpu.VMEM((2, n_prob, 2 * chunk, hd), BF16),
                        pltpu.VMEM((2, n_prob, chunk, hd), BF16),
                        pltpu.VMEM((2, n_prob, chunk, 2 * hd), BF16),
                        pltpu.VMEM((2, n_prob, chunk, hd), F32),
                        pltpu.VMEM((2, n_prob, chunk, hd), BF16),
                        pltpu.VMEM((2, nchunk, chunk, DN_HEADS), F32),
                        pltpu.VMEM((2, nchunk, DN_HEADS, chunk), F32),
                        pltpu.VMEM((tile, DN_WIDTH), F32)],
        compiler_params=_cparams(("arbitrary",)),
        name="deltanet_pipelined",
    )(proj_t, proj_t, proj_t, bdc_t, bdc_t, bdc_t, bdr_t, bdr_t, bdr_t, proj_p, qkv_w, p_row, p_col, norm_w)
    return od.reshape(b, l, DN_WIDTH), qkv_state, s_final


def _mem_kernel(q_ref, g_ref, k_ref, v_ref, o_ref, *, nb):
    hd = MEM_HEAD_DIM
    scale = hd ** -0.5
    probs = [(i, h) for i in range(nb) for h in range(MEM_HEADS)]
    lanes = [slice(h * hd, (h + 1) * hd) for h in range(MEM_HEADS)]
    qs = [q_ref[i, :, lanes[h]] for i, h in probs]
    ks = [k_ref[i, pl.ds(h, MEM_TOKENS, stride=MEM_HEADS), :] for i, h in probs]
    vs = [v_ref[i, pl.ds(h, MEM_TOKENS, stride=MEM_HEADS), :] for i, h in probs]
    ss = [_bdot(q, k, _NT) * scale for q, k in zip(qs, ks)]
    es = [jnp.exp(s - jnp.max(s, axis=-1, keepdims=True)) for s in ss]
    oms = [_bdot(e, v) / jnp.sum(e, axis=-1, keepdims=True) for e, v in zip(es, vs)]
    for (i, h), om in zip(probs, oms):
        o_ref[i, :, lanes[h]] = (om * _silu(g_ref[i, :, lanes[h]])).astype(o_ref.dtype)


def _mem_group(proj3, mem_k, mem_v, tile, nb):
    b, l, _ = proj3.shape
    cb = COL_BLOCK
    rows = MEM_TOKENS * MEM_HEADS
    return pl.pallas_call(
        functools.partial(_mem_kernel, nb=nb),
        grid=(b // nb, l // tile),
        in_specs=[pl.BlockSpec((nb, tile, cb), lambda i, t: (i, t, MEMQ_BLK)),
                  pl.BlockSpec((nb, tile, cb), lambda i, t: (i, t, MEMG_BLK)),
                  pl.BlockSpec((nb, rows, MEM_HEAD_DIM), lambda i, t: (i, 0, 0)),
                  pl.BlockSpec((nb, rows, MEM_HEAD_DIM), lambda i, t: (i, 0, 0))],
        out_specs=pl.BlockSpec((nb, tile, MEM_WIDTH), lambda i, t: (i, t, 0)),
        out_shape=jax.ShapeDtypeStruct((b, l, MEM_WIDTH), BF16),
        compiler_params=_cparams(("arbitrary", "arbitrary")),
        name="memory_group",
    )(proj3, proj3, mem_k, mem_v)


def _out_kernel(oc_ref, od_ref, om_ref, x_ref, w_ref, p_ref, y_ref):
    half = x_ref.shape[0] // 2
    for r in range(2):
        rows = slice(r * half, (r + 1) * half)
        h = jnp.dot(oc_ref[rows, :], w_ref[0:C_CONV, :], preferred_element_type=F32)
        h = h + jnp.dot(od_ref[rows, :], w_ref[C_CONV:C_CONV + DN_WIDTH, :], preferred_element_type=F32)
        h = h + jnp.dot(om_ref[rows, :], w_ref[C_CONV + DN_WIDTH:, :], preferred_element_type=F32)
        z = DEEPNORM_ALPHA * x_ref[rows, :] + h
        mu = jnp.mean(z, axis=-1, keepdims=True)
        zc = z - mu
        var = jnp.mean(zc * zc, axis=-1, keepdims=True)
        y_ref[rows, :] = zc * lax.rsqrt(var + LN_EPS) * p_ref[0:1, :] + p_ref[1:2, :]


def _out_proj(oc, od, om, x2, w_out, ln_p, bm):
    m = x2.shape[0]
    return pl.pallas_call(
        _out_kernel,
        grid=(m // bm,),
        in_specs=[pl.BlockSpec((bm, C_CONV), lambda i: (i, 0)),
                  pl.BlockSpec((bm, DN_WIDTH), lambda i: (i, 0)),
                  pl.BlockSpec((bm, MEM_WIDTH), lambda i: (i, 0)),
                  pl.BlockSpec((bm, D_MODEL), lambda i: (i, 0)),
                  pl.BlockSpec((D_MODEL, D_MODEL), lambda i: (0, 0)),
                  pl.BlockSpec((PARAM_ROWS, D_MODEL), lambda i: (0, 0))],
        out_specs=pl.BlockSpec((bm, D_MODEL), lambda i: (i, 0)),
        out_shape=jax.ShapeDtypeStruct((m, D_MODEL), F32),
        compiler_params=_cparams(("arbitrary",)),
        name="out_proj_ln",
    )(oc, od, om, x2, w_out, ln_p)


def _pad_rows(rows, n):
    a = jnp.stack(rows).astype(F32)
    return jnp.pad(a, ((0, n - a.shape[0]), (0, 0)))


def _project(x, w_in_t, layer, bm, dn_chunk):
    b, l, _ = x.shape
    proj, bdc, bdr = _in_proj(x.reshape(b * l, D_MODEL), w_in_t, layer, bm)
    bdr4 = bdr.reshape(2 * DN_HEADS, b, l // dn_chunk, dn_chunk).transpose(1, 2, 0, 3)
    return proj.reshape(b, l, N_MAIN), bdc.reshape(b, l, SMALL_PAD), bdr4


def _mix(x, oc, od, om, w_out, ln_p, out_bm):
    b, l, _ = x.shape
    y = _out_proj(oc.reshape(b * l, C_CONV), od.reshape(b * l, DN_WIDTH), om.reshape(b * l, MEM_WIDTH),
                  x.reshape(b * l, D_MODEL), w_out, ln_p, out_bm)
    return y.reshape(b, l, D_MODEL)


def kernel(x_prompt, x_sample, mem_prompt, state_conv, state_qkv_conv, state_delta, cache_mem_k, cache_mem_v,
           w_in, conv_w, conv_b, conv_ln_g, conv_ln_b, qkv_conv_w, a_log, dt_bias, delta_norm_w,
           w_mem_k, w_mem_v, w_out, ln_g, ln_b):
    bp = x_prompt.shape[0]
    bs = x_sample.shape[0]
    hp, hs = x_prompt, x_sample
    outs = [[] for _ in range(8)]
    w_in_t = jnp.swapaxes(w_in, 1, 2)
    kv_rows = MEM_TOKENS * MEM_HEADS
    ls = x_sample.shape[1]
    for i in range(DEPTH):
        conv_p = _pad_rows([conv_b[i], conv_ln_g[i], conv_ln_b[i]], PARAM_ROWS)
        p_row = jnp.stack([a_log[i], dt_bias[i]]).astype(F32)
        p_col = p_row.T
        norm_w = delta_norm_w[i].reshape(1, DN_HEAD_DIM)
        w_out_i = w_out[i].astype(BF16)
        ln_p = _pad_rows([ln_g[i], ln_b[i]], PARAM_ROWS)

        w_kv = jnp.concatenate([w_mem_k[i], w_mem_v[i]], axis=1).astype(BF16)
        kv = _matmul(mem_prompt.reshape(bp * MEM_TOKENS, D_MODEL), w_kv, KV_PROJ_BLOCK, KV_PROJ_BLOCK)
        mk = kv[:, :MEM_WIDTH].reshape(bp, MEM_TOKENS, MEM_HEADS, MEM_HEAD_DIM)
        mv = kv[:, MEM_WIDTH:].reshape(bp, MEM_TOKENS, MEM_HEADS, MEM_HEAD_DIM)
        sk = cache_mem_k[i].reshape(bs, kv_rows, MEM_HEAD_DIM)
        sv = cache_mem_v[i].reshape(bs, kv_rows, MEM_HEAD_DIM)

        proj_p, bdc_p, bdr_p = _project(hp, w_in_t, i, PROMPT_PROJ_ROWS, PROMPT_DN_CHUNK)
        proj_s, bdc_s, bdr_s = _project(hs, w_in_t, i, bs * ls, ls)

        oc_p, conv_st_p, om_s = _conv_group(proj_p, None, i, conv_w[i], conv_p, PROMPT_CONV_TILE, 1,
                                            guest=(proj_s, sk, sv, SAMPLE_SEQS))
        od_p, qkv_st_p, s_p = _dn_group_pipelined(proj_p, bdc_p, bdr_p, qkv_conv_w[i], p_row, p_col, norm_w,
                                                  PROMPT_DN_TILE, PROMPT_DN_CHUNK)
        om_p = _mem_group(proj_p, mk.reshape(bp, kv_rows, MEM_HEAD_DIM), mv.reshape(bp, kv_rows, MEM_HEAD_DIM),
                          PROMPT_MEM_TILE, 1)
        hp_next = _mix(hp, oc_p, od_p, om_p, w_out_i, ln_p, OUT_ROWS)
        outs[0].append(conv_st_p); outs[1].append(qkv_st_p); outs[2].append(s_p)
        outs[3].append(mk)
        outs[4].append(mv)

        oc_s, conv_st_s = _conv_group(proj_s, state_conv, i, conv_w[i], conv_p, ls, SAMPLE_SEQS)
        od_s, qkv_st_s, s_s = _dn_group(proj_s, bdc_s, bdr_s, state_qkv_conv, state_delta, i, qkv_conv_w[i],
                                        p_row, p_col, norm_w, ls, ls, SAMPLE_SEQS)
        hs_next = _mix(hs, oc_s, od_s, om_s, w_out_i, ln_p, OUT_ROWS)
        outs[5].append(conv_st_s); outs[6].append(qkv_st_s); outs[7].append(s_s)
        hp, hs = hp_next, hs_next
    return (hp, hs) + tuple(jnp.stack(o) for o in outs)
```

```python
import functools

import jax
import jax.numpy as jnp
from jax import lax
from jax.experimental import pallas as pl
from jax.experimental.pallas import tpu as pltpu

F32 = jnp.float32
BF16 = jnp.bfloat16

D_MODEL = 2048
C_CONV = 512
CONV_K = 31
DN_HEADS = 8
DN_HEAD_DIM = 128
DN_WIDTH = DN_HEADS * DN_HEAD_DIM
DN_CONV_K = 4
MEM_HEADS = 4
MEM_HEAD_DIM = 128
MEM_WIDTH = MEM_HEADS * MEM_HEAD_DIM
MEM_TOKENS = 256
DEPTH = 1
DEEPNORM_ALPHA = (2 * DEPTH) ** 0.25
LN_EPS = 1e-5
NORM_EPS = 1e-6

N_MAIN = 3 * DN_WIDTH + DN_WIDTH + 3 * C_CONV + 2 * MEM_WIDTH
COL_BLOCK = 512
QKV_BLK = 0
DNGATE_BLK = 3
GLUA_BLK, GLUB_BLK, CGATE_BLK, MEMQ_BLK, MEMG_BLK = 8, 9, 10, 11, 12
SMALL_PAD = 128
N_DIRECT_BLOCKS = (3 * C_CONV + 4 * DN_WIDTH) // COL_BLOCK
CONV_HALO = 32
QKV_HALO = 8
SUBLANES = 8
CONV_ROW_BLOCK = 64
PARAM_ROWS = SUBLANES

PROMPT_PROJ_ROWS = 2048
PROMPT_CONV_TILE = 512
PROMPT_DN_TILE = 256
PROMPT_DN_CHUNK = 64
PROMPT_MEM_TILE = 512
OUT_ROWS = 512
KV_PROJ_BLOCK = 512
SAMPLE_SEQS = 8
SAMPLE_DN_SEQS = 16

VMEM_LIMIT = 56 * 1024 * 1024


def _sigmoid(x):
    return 1.0 / (1.0 + jnp.exp(-x))


def _silu(x):
    return x * _sigmoid(x)


def _softplus(x):
    return jnp.maximum(x, 0.0) + jnp.log(1.0 + jnp.exp(-jnp.abs(x)))


def _cparams(sem):
    return pltpu.CompilerParams(dimension_semantics=sem, vmem_limit_bytes=VMEM_LIMIT)


def _proj_kernel(x_hbm, wa_ref, wb_ref, o_ref, bdc_ref, bdr_ref, xf_ref, xb_ref, sem):
    i = pl.program_id(0)
    j = pl.program_id(1)
    bm = xf_ref.shape[0]
    nt = (((1,), (1,)), ((), ()))
    n_bd = 2 * DN_HEADS

    def x_copy(block):
        return pltpu.make_async_copy(x_hbm.at[pl.ds(block * bm, bm), :], xf_ref, sem)

    @pl.when(j == 0)
    def _():
        @pl.when(i == 0)
        def _():
            x_copy(0).start()

        x_copy(i).wait()
        xb_ref[...] = xf_ref[...].astype(BF16)

        @pl.when(i + 1 < pl.num_programs(0))
        def _():
            x_copy(i + 1).start()

    @pl.when(j < N_DIRECT_BLOCKS)
    def _():
        o_ref[...] = lax.dot_general(xb_ref[...], wa_ref[...].astype(BF16), nt, preferred_element_type=F32)

    @pl.when(j >= N_DIRECT_BLOCKS)
    def _():
        w = jnp.concatenate([wa_ref[n_bd:, :], wb_ref[...]], axis=0).astype(BF16)
        o_ref[...] = lax.dot_general(xb_ref[...], w, nt, preferred_element_type=F32)

    @pl.when(j == N_DIRECT_BLOCKS)
    def _():
        bd = lax.dot_general(xb_ref[...], wa_ref[0:SMALL_PAD, :].astype(BF16), nt, preferred_element_type=F32)
        bdc_ref[...] = bd
        bdr_ref[...] = bd.T[0:n_bd, :]


def _proj_out_block(j):
    n_conv = 3 * C_CONV // COL_BLOCK
    return jnp.where(j < n_conv, j + GLUA_BLK, jnp.where(j < N_DIRECT_BLOCKS, j - n_conv, j))


def _in_proj(x2, w_in_t, layer, bm):
    m = x2.shape[0]
    bn = COL_BLOCK
    n_blocks = N_MAIN // bn
    n_bd = 2 * DN_HEADS
    return pl.pallas_call(
        _proj_kernel,
        grid=(m // bm, n_blocks),
        in_specs=[
            pl.BlockSpec(memory_space=pl.ANY),
            pl.BlockSpec((None, bn, D_MODEL), lambda i, j: (layer, j, 0)),
            pl.BlockSpec((None, n_bd, D_MODEL),
                         lambda i, j: (layer, (jnp.maximum(j + 1, n_blocks - 1)) * (bn // n_bd), 0)),
        ],
        out_specs=[
            pl.BlockSpec((bm, bn), lambda i, j: (i, _proj_out_block(j))),
            pl.BlockSpec((bm, SMALL_PAD), lambda i, j: (i, 0)),
            pl.BlockSpec((2 * DN_HEADS, bm), lambda i, j: (0, i)),
        ],
        out_shape=[
            jax.ShapeDtypeStruct((m, N_MAIN), F32),
            jax.ShapeDtypeStruct((m, SMALL_PAD), F32),
            jax.ShapeDtypeStruct((2 * DN_HEADS, m), F32),
        ],
        scratch_shapes=[pltpu.VMEM((bm, D_MODEL), F32), pltpu.VMEM((bm, D_MODEL), BF16),
                        pltpu.SemaphoreType.DMA(())],
        compiler_params=_cparams(("arbitrary", "arbitrary")),
        name="in_proj",
    )(x2, w_in_t, w_in_t)


def _mm_kernel(x_ref, w_ref, o_ref):
    o_ref[...] = jnp.dot(x_ref[...].astype(BF16), w_ref[...], preferred_element_type=F32)


def _matmul(x2, w, bm, bn):
    m, k = x2.shape
    n = w.shape[1]
    return pl.pallas_call(
        _mm_kernel,
        grid=(m // bm, n // bn),
        in_specs=[pl.BlockSpec((bm, k), lambda i, j: (i, 0)),
                  pl.BlockSpec((k, bn), lambda i, j: (0, j))],
        out_specs=pl.BlockSpec((bm, bn), lambda i, j: (i, j)),
        out_shape=jax.ShapeDtypeStruct((m, n), F32),
        compiler_params=_cparams(("arbitrary", "arbitrary")),
        name="mem_kv_proj",
    )(x2, w)


def _conv_kernel(*refs, tile, nb, has_state, guest_nb):
    refs = list(refs)
    a_ref, b_ref, g_ref = refs[:3]
    del refs[:3]
    st_ref = refs.pop(0) if has_state else None
    w_ref, p_ref = refs[:2]
    del refs[:2]
    if guest_nb:
        gq_ref, gg_ref, gk_ref, gv_ref = refs[:4]
        del refs[:4]
        o_ref, so_ref, go_ref, xp_ref = refs
        _mem_kernel(gq_ref, gg_ref, gk_ref, gv_ref, go_ref, nb=guest_nb)
    else:
        o_ref, so_ref, xp_ref = refs
    t = pl.program_id(1)
    last = pl.num_programs(1) - 1
    lo = CONV_HALO - (CONV_K - 1)
    n = tile + CONV_HALO
    rb = min(CONV_ROW_BLOCK, tile)
    w = w_ref[...]
    conv_b = p_ref[0:1, :]
    ln_g = p_ref[1:2, :]
    ln_b = p_ref[2:3, :]
    for i in range(nb):
        @pl.when(t == 0)
        def _():
            xp_ref[i, 0, 0:CONV_HALO, :] = jnp.zeros((CONV_HALO, C_CONV), F32)
            if has_state:
                xp_ref[i, 0, lo:CONV_HALO, :] = st_ref[:, i, :]

        xp_ref[i, 0, CONV_HALO:n, :] = a_ref[i] * _sigmoid(b_ref[i])
        xp = xp_ref[i, 0]
        for s in range(1, SUBLANES):
            xp_ref[i, s] = pltpu.roll(xp, n - s, axis=0)

        for r in range(tile // rb):
            acc = None
            for j in range(CONV_K):
                s = (lo + j) % SUBLANES
                start = lo + j - s + r * rb
                term = xp_ref[i, s, start:start + rb, :] * w[j:j + 1, :]
                acc = term if acc is None else acc + term
            hc = acc + conv_b
            mu = jnp.mean(hc, axis=-1, keepdims=True)
            xc = hc - mu
            var = jnp.mean(xc * xc, axis=-1, keepdims=True)
            hn = xc * lax.rsqrt(var + LN_EPS) * ln_g + ln_b
            rows = slice(r * rb, (r + 1) * rb)
            o_ref[i, rows, :] = (_silu(hn) * _silu(g_ref[i, rows, :])).astype(o_ref.dtype)

        @pl.when(t == last)
        def _():
            so_ref[i] = xp_ref[i, 0, tile + lo:n, :]

        xp_ref[i, 0, 0:CONV_HALO, :] = xp_ref[i, 0, tile:n, :]


def _conv_group(proj3, state, layer, conv_w, conv_p, tile, nb, guest=None):
    b, l, _ = proj3.shape
    has_state = state is not None
    cb = COL_BLOCK
    nrow = CONV_K - 1
    nt = l // tile
    in_specs = [
        pl.BlockSpec((nb, tile, cb), lambda i, t: (i, t, GLUA_BLK)),
        pl.BlockSpec((nb, tile, cb), lambda i, t: (i, t, GLUB_BLK)),
        pl.BlockSpec((nb, tile, cb), lambda i, t: (i, t, CGATE_BLK)),
    ]
    args = [proj3, proj3, proj3]
    if has_state:
        in_specs.append(pl.BlockSpec((None, nrow, nb, C_CONV), lambda i, t: (layer, 0, i, 0)))
        args.append(jnp.swapaxes(state, 1, 2))
    in_specs += [pl.BlockSpec((CONV_K, C_CONV), lambda i, t: (0, 0)),
                 pl.BlockSpec((PARAM_ROWS, C_CONV), lambda i, t: (0, 0))]
    args += [conv_w, conv_p]
    out_specs = [pl.BlockSpec((nb, tile, C_CONV), lambda i, t: (i, t, 0)),
                 pl.BlockSpec((nb, nrow, C_CONV), lambda i, t: (i, 0, 0))]
    out_shape = [jax.ShapeDtypeStruct((b, l, C_CONV), BF16),
                 jax.ShapeDtypeStruct((b, nrow, C_CONV), F32)]
    guest_nb = 0
    if guest is not None:
        g_proj3, g_k, g_v, guest_nb = guest
        gb, gl, _ = g_proj3.shape
        assert gb // guest_nb == (b // nb) * nt
        kv_rows = MEM_TOKENS * MEM_HEADS
        step = lambda i, t: i * nt + t
        in_specs += [pl.BlockSpec((guest_nb, gl, cb), lambda i, t: (step(i, t), 0, MEMQ_BLK)),
                     pl.BlockSpec((guest_nb, gl, cb), lambda i, t: (step(i, t), 0, MEMG_BLK)),
                     pl.BlockSpec((guest_nb, kv_rows, MEM_HEAD_DIM), lambda i, t: (step(i, t), 0, 0)),
                     pl.BlockSpec((guest_nb, kv_rows, MEM_HEAD_DIM), lambda i, t: (step(i, t), 0, 0))]
        args += [g_proj3, g_proj3, g_k, g_v]
        out_specs.append(pl.BlockSpec((guest_nb, gl, MEM_WIDTH), lambda i, t: (step(i, t), 0, 0)))
        out_shape.append(jax.ShapeDtypeStruct((gb, gl, MEM_WIDTH), BF16))
    return pl.pallas_call(
        functools.partial(_conv_kernel, tile=tile, nb=nb, has_state=has_state, guest_nb=guest_nb),
        grid=(b // nb, nt),
        in_specs=in_specs,
        out_specs=out_specs,
        out_shape=out_shape,
        scratch_shapes=[pltpu.VMEM((nb, SUBLANES, tile + CONV_HALO, C_CONV), F32)],
        compiler_params=_cparams(("arbitrary", "arbitrary")),
        name="conv_group",
    )(*args)


def _bdot(a, b, dims=None):
    a = a.astype(BF16)
    b = b.astype(BF16)
    if dims is None:
        return jnp.dot(a, b, preferred_element_type=F32)
    return lax.dot_general(a, b, (dims, ((), ())), preferred_element_type=F32)


_NT = ((1,), (1,))
_TN = ((0,), (0,))


def _split_bf16(a):
    hi = a.astype(BF16)
    return hi, (a - hi.astype(F32)).astype(BF16)


def _unit_lower_inverses(ms, c):
    row = lax.broadcasted_iota(jnp.int32, (c, c), 0)
    col = lax.broadcasted_iota(jnp.int32, (c, c), 1)
    eye = (row == col).astype(F32)
    ps = [-m for m in ms]
    ts = [eye + p for p in ps]
    span = 2
    while span < c:
        ps = [_bdot(p, p) for p in ps]
        ts = [t + _bdot(t, p) for t, p in zip(ts, ps)]
        span *= 2
    msp = [_split_bf16(m) for m in ms]
    tsp = [_split_bf16(t) for t in ts]
    mts = [_bdot(mh, th) + _bdot(mh, tl) + _bdot(ml, th) for (mh, ml), (th, tl) in zip(msp, tsp)]
    rs = [eye - t - mt for t, mt in zip(ts, mts)]
    return [t + _bdot(th, r) for t, (th, _), r in zip(ts, tsp, rs)]


def _dn_kernel(qkv_ref, gate_ref, bdc_ref, bdr_ref, prev_ref, s0_ref, w_ref, pr_ref, pc_ref, nw_ref,
               o_ref, st_ref, s_ref, xp_ref, xs_ref, osc_ref, *, tile, chunk, nb):
    c = chunk
    nchunk = tile // c
    hd = DN_HEAD_DIM
    t = pl.program_id(1)
    last = pl.num_programs(1) - 1
    lo = QKV_HALO - (DN_CONV_K - 1)
    hi = lax.Precision.HIGHEST
    w = w_ref[...]

    for i in range(nb):
        @pl.when(t == 0)
        def _():
            xp_ref[i, 0:QKV_HALO, :] = jnp.zeros((QKV_HALO, 3 * DN_WIDTH), F32)
            xp_ref[i, lo:QKV_HALO, :] = prev_ref[:, i, :]
            s_ref[i] = s0_ref[i]

        xp_ref[i, QKV_HALO:QKV_HALO + tile, :] = qkv_ref[i]
        xp = xp_ref[i]
        for j in range(DN_CONV_K - 1):
            xs_ref[i, j] = pltpu.roll(xp, tile + QKV_HALO - (lo + j), axis=0)

        @pl.when(t == last)
        def _():
            st_ref[i] = xp_ref[i, tile + lo:tile + QKV_HALO, :]

    neg_a_row = -jnp.exp(pr_ref[0:1, :])
    dtb_row = pr_ref[1:2, :]
    neg_a_col = -jnp.exp(pc_ref[:, 0:1])
    dtb_col = pc_ref[:, 1:2]
    norm_w = nw_ref[...]

    row = lax.broadcasted_iota(jnp.int32, (c, c), 0)
    col = lax.broadcasted_iota(jnp.int32, (c, c), 1)
    incl = row >= col
    strict = row > col
    tri_l = incl.astype(F32)
    tri_u = (row <= col).astype(F32)

    seq_chunks = [(i, n) for i in range(nb) for n in range(nchunk)]
    probs = [(i, n, h) for i, n in seq_chunks for h in range(DN_HEADS)]
    beta, gcum_col, gcum_row, prep = {}, {}, {}, {}
    for i, n in seq_chunks:
        rows = slice(n * c, (n + 1) * c)
        bd = bdc_ref[i, rows, :]
        beta[i, n] = _sigmoid(bd[:, 0:DN_HEADS])
        g_col = neg_a_row * _softplus(bd[:, DN_HEADS:2 * DN_HEADS] + dtb_row)
        gcum_col[i, n] = jnp.dot(tri_l, g_col, precision=hi, preferred_element_type=F32)
        br = bdr_ref[i, n]
        g_row = neg_a_col * _softplus(br[DN_HEADS:2 * DN_HEADS, :] + dtb_col)
        gcum_row[i, n] = jnp.dot(g_row, tri_u, precision=hi, preferred_element_type=F32)
        acc = xp_ref[i, QKV_HALO + n * c:QKV_HALO + (n + 1) * c, :] * w[DN_CONV_K - 1:DN_CONV_K, :]
        for j in range(DN_CONV_K - 1):
            acc = acc + xs_ref[i, j, rows, :] * w[j:j + 1, :]
        qkvc = _silu(acc)
        for h in range(DN_HEADS):
            q, k, v = (qkvc[:, part * DN_WIDTH + h * hd:part * DN_WIDTH + (h + 1) * hd] for part in range(3))
            q = q * (lax.rsqrt(jnp.sum(q * q, axis=-1, keepdims=True) + NORM_EPS) * (hd ** -0.5))
            k = k * lax.rsqrt(jnp.sum(k * k, axis=-1, keepdims=True) + NORM_EPS)
            gc = gcum_col[i, n][:, h:h + 1]
            gr = gcum_row[i, n][h:h + 1, :]
            bt = beta[i, n][:, h:h + 1]
            gl = gcum_col[i, n][c - 1:c, h:h + 1]
            e = jnp.exp(gc)
            kb = k * bt
            prep[i, n, h] = (
                jnp.concatenate([kb, q], axis=0).astype(BF16),
                k.astype(BF16),
                jnp.exp(jnp.where(incl, gc - gr, -jnp.inf)),
                jnp.concatenate([v * bt, kb * e], axis=1).astype(BF16),
                q * e,
                (k * jnp.exp(gl - gc)).astype(BF16),
                jnp.exp(gl))

    preps = [prep[p] for p in probs]
    kqs = [_bdot(p[0], p[1], _NT) for p in preps]
    aqks = [(kq[c:] * p[2]).astype(BF16) for kq, p in zip(kqs, preps)]
    ms = [jnp.where(strict, kq[:c] * p[2], 0.0) for kq, p in zip(kqs, preps)]
    tinvs = _unit_lower_inverses(ms, c)
    uws = [_bdot(tinv, p[3]) for tinv, p in zip(tinvs, preps)]
    wqs = [jnp.concatenate([uw[:, hd:], p[4]], axis=0).astype(BF16) for uw, p in zip(uws, preps)]
    local = dict(zip(probs, zip(uws, wqs, aqks, [p[5] for p in preps], [p[6] for p in preps])))

    heads = [(i, h) for i in range(nb) for h in range(DN_HEADS)]
    state = {(i, h): s_ref[i, h] for i, h in heads}
    for n in range(nchunk):
        rows = slice(n * c, (n + 1) * c)
        items = [(i, h) + local[i, n, h] for i, h in heads]
        s16s = [state[i, h].astype(BF16) for i, h in heads]
        wss = [_bdot(it[3], s16) for it, s16 in zip(items, s16s)]
        u16s = [(it[2][:, :hd] - ws[:c]).astype(BF16) for it, ws in zip(items, wss)]
        outs = [ws[c:] + _bdot(it[4], u16) for it, ws, u16 in zip(items, wss, u16s)]
        for it, u16 in zip(items, u16s):
            i, h = it[0], it[1]
            state[i, h] = state[i, h] * it[6] + _bdot(it[5], u16, _TN)
        for it, o in zip(items, outs):
            i, h = it[0], it[1]
            o = o * lax.rsqrt(jnp.mean(o * o, axis=-1, keepdims=True) + NORM_EPS) * norm_w
            osc_ref[i, rows, h * hd:(h + 1) * hd] = o
    for i, h in heads:
        s_ref[i, h] = state[i, h]

    for i in range(nb):
        xp_ref[i, 0:QKV_HALO, :] = xp_ref[i, tile:tile + QKV_HALO, :]
        o_ref[i] = (osc_ref[i] * _silu(gate_ref[i])).astype(o_ref.dtype)


def _dn_group(proj3, bdc3, bdr4, prev, s0, layer, qkv_w, p_row, p_col, norm_w, tile, chunk, nb):
    b, l, _ = proj3.shape
    nchunk = tile // chunk
    hd = DN_HEAD_DIM
    nrow = DN_CONV_K - 1
    const2 = lambda i, t: (0, 0)
    od, qkv_state, s_final = pl.pallas_call(
        functools.partial(_dn_kernel, tile=tile, chunk=chunk, nb=nb),
        grid=(b // nb, l // tile),
        in_specs=[
            pl.BlockSpec((nb, tile, 3 * DN_WIDTH), lambda i, t: (i, t, QKV_BLK)),
            pl.BlockSpec((nb, tile, DN_WIDTH), lambda i, t: (i, t, DNGATE_BLK)),
            pl.BlockSpec((nb, tile, SMALL_PAD), lambda i, t: (i, t, 0)),
            pl.BlockSpec((nb, nchunk, 2 * DN_HEADS, chunk), lambda i, t: (i, t, 0, 0)),
            pl.BlockSpec((None, nrow, nb, 3 * DN_WIDTH), lambda i, t: (layer, 0, i, 0)),
            pl.BlockSpec((None, nb, DN_HEADS, hd, hd), lambda i, t: (layer, i, 0, 0, 0)),
            pl.BlockSpec((DN_CONV_K, 3 * DN_WIDTH), const2),
            pl.BlockSpec((2, DN_HEADS), const2),
            pl.BlockSpec((DN_HEADS, 2), const2),
            pl.BlockSpec((1, hd), const2),
        ],
        out_specs=[pl.BlockSpec((nb, tile, DN_WIDTH), lambda i, t: (i, t, 0)),
                   pl.BlockSpec((nb, nrow, 3 * DN_WIDTH), lambda i, t: (i, 0, 0)),
                   pl.BlockSpec((nb, DN_HEADS, hd, hd), lambda i, t: (i, 0, 0, 0))],
        out_shape=[jax.ShapeDtypeStruct((b, l, DN_WIDTH), BF16),
                   jax.ShapeDtypeStruct((b, nrow, 3 * DN_WIDTH), F32),
                   jax.ShapeDtypeStruct((b, DN_HEADS, hd, hd), F32)],
        scratch_shapes=[pltpu.VMEM((nb, tile + QKV_HALO, 3 * DN_WIDTH), F32),
                        pltpu.VMEM((nb, DN_CONV_K - 1, tile + QKV_HALO, 3 * DN_WIDTH), F32),
                        pltpu.VMEM((nb, tile, DN_WIDTH), F32)],
        compiler_params=_cparams(("arbitrary", "arbitrary")),
        name="deltanet_group",
    )(proj3, proj3, bdc3, bdr4, jnp.swapaxes(prev, 1, 2), s0, qkv_w, p_row, p_col, norm_w)
    return od, qkv_state, s_final


def _dn_pipelined_kernel(q0_ref, qa_ref, qb_ref, c0_ref, ca_ref, cb_ref, r0_ref, ra_ref, rb_ref, gate_ref,
                         w_ref, pr_ref, pc_ref, nw_ref, o_ref, st_ref, s_ref,
                         xp_ref, kq_ref, k16_ref, uwr_ref, qe_ref, kd_ref, gcc_ref, gcr_ref, osc_ref,
                         *, tile, chunk, nt, n_tiles):
    c = chunk
    nchunk = tile // c
    hd = DN_HEAD_DIM
    lo = QKV_HALO - (DN_CONV_K - 1)
    hi = lax.Precision.HIGHEST
    k = pl.program_id(0)
    w = w_ref[...]
    neg_a_row = -jnp.exp(pr_ref[0:1, :])
    dtb_row = pr_ref[1:2, :]
    neg_a_col = -jnp.exp(pc_ref[:, 0:1])
    dtb_col = pc_ref[:, 1:2]
    norm_w = nw_ref[...]
    row = lax.broadcasted_iota(jnp.int32, (c, c), 0)
    col = lax.broadcasted_iota(jnp.int32, (c, c), 1)
    incl = row >= col
    strict = row > col
    tri_l = incl.astype(F32)
    tri_u = (row <= col).astype(F32)
    probs = [(n, h) for n in range(nchunk) for h in range(DN_HEADS)]

    def prepare(buf, q_ref, c_ref, r_ref, first, write_state):
        halo = xp_ref[0:QKV_HALO, :]
        xp_ref[0:QKV_HALO, :] = jnp.zeros_like(halo) if first is True else jnp.where(first, 0.0, halo)
        xp_ref[QKV_HALO:QKV_HALO + tile, :] = q_ref[0]
        if write_state:
            st_ref[0] = xp_ref[tile + lo:tile + QKV_HALO, :]
        for n in range(nchunk):
            rows = slice(n * c, (n + 1) * c)
            bd = c_ref[0, rows, :]
            beta = _sigmoid(bd[:, 0:DN_HEADS])
            g_col = neg_a_row * _softplus(bd[:, DN_HEADS:2 * DN_HEADS] + dtb_row)
            gcum_col = jnp.dot(tri_l, g_col, precision=hi, preferred_element_type=F32)
            br = r_ref[0, n]
            g_row = neg_a_col * _softplus(br[DN_HEADS:2 * DN_HEADS, :] + dtb_col)
            gcc_ref[buf, n] = gcum_col
            gcr_ref[buf, n] = jnp.dot(g_row, tri_u, precision=hi, preferred_element_type=F32)
            window = xp_ref[n * c:(n + 1) * c + QKV_HALO, :]
            acc = window[QKV_HALO:, :] * w[DN_CONV_K - 1:DN_CONV_K, :]
            for j in range(DN_CONV_K - 1):
                acc = acc + pltpu.roll(window, c + QKV_HALO - (lo + j), axis=0)[0:c, :] * w[j:j + 1, :]
            qkvc = _silu(acc)
            for h in range(DN_HEADS):
                p = n * DN_HEADS + h
                q, kk, v = (qkvc[:, part * DN_WIDTH + h * hd:part * DN_WIDTH + (h + 1) * hd] for part in range(3))
                q = q * (lax.rsqrt(jnp.sum(q * q, axis=-1, keepdims=True) + NORM_EPS) * (hd ** -0.5))
                kk = kk * lax.rsqrt(jnp.sum(kk * kk, axis=-1, keepdims=True) + NORM_EPS)
                gc = gcum_col[:, h:h + 1]
                bt = beta[:, h:h + 1]
                gl = gcum_col[c - 1:c, h:h + 1]
                e = jnp.exp(gc)
                kb = kk * bt
                kq_ref[buf, p] = jnp.concatenate([kb, q], axis=0).astype(BF16)
                k16_ref[buf, p] = kk.astype(BF16)
                uwr_ref[buf, p] = jnp.concatenate([v * bt, kb * e], axis=1).astype(BF16)
                qe_ref[buf, p] = q * e
                kd_ref[buf, p] = (kk * jnp.exp(gl - gc)).astype(BF16)
        xp_ref[0:QKV_HALO, :] = xp_ref[tile:tile + QKV_HALO, :]

    def finish(buf, half):
        idx = [n * DN_HEADS + h for n, h in probs]
        kqs = [_bdot(kq_ref[buf, p], k16_ref[buf, p], _NT) for p in idx]
        gcs = [gcc_ref[buf, n][:, h:h + 1] for n, h in probs]
        decays = [jnp.exp(jnp.where(incl, gc - gcr_ref[buf, n][h:h + 1, :], -jnp.inf))
                  for gc, (n, h) in zip(gcs, probs)]
        aqks = [(kq[c:] * dc).astype(BF16) for kq, dc in zip(kqs, decays)]
        ms = [jnp.where(strict, kq[:c] * dc, 0.0) for kq, dc in zip(kqs, decays)]
        tinvs = _unit_lower_inverses(ms, c)
        uws = [_bdot(tinv, uwr_ref[buf, p]) for tinv, p in zip(tinvs, idx)]
        wqs = [jnp.concatenate([uw[:, hd:], qe_ref[buf, p]], axis=0).astype(BF16) for uw, p in zip(uws, idx)]
        egls = [jnp.exp(gc[c - 1:c, :]) for gc in gcs]
        local = dict(zip(probs, zip(uws, wqs, aqks, egls)))
        state = [s_ref[0, h] for h in range(DN_HEADS)]
        for n in range(nchunk):
            rows = slice(n * c, (n + 1) * c)
            items = [local[n, h] for h in range(DN_HEADS)]
            s16s = [s.astype(BF16) for s in state]
            wss = [_bdot(it[1], s16) for it, s16 in zip(items, s16s)]
            u16s = [(it[0][:, :hd] - ws[:c]).astype(BF16) for it, ws in zip(items, wss)]
            outs = [ws[c:] + _bdot(it[2], u16) for it, ws, u16 in zip(items, wss, u16s)]
            state = [s * it[3] + _bdot(kd_ref[buf, n * DN_HEADS + h], u16, _TN)
                     for h, (s, it, u16) in enumerate(zip(state, items, u16s))]
            for h, o in enumerate(outs):
                o = o * lax.rsqrt(jnp.mean(o * o, axis=-1, keepdims=True) + NORM_EPS) * norm_w
                osc_ref[rows, h * hd:(h + 1) * hd] = o
        for h in range(DN_HEADS):
            s_ref[0, h] = state[h]
        rows_out = slice(half * tile, (half + 1) * tile)
        o_ref[0, rows_out, :] = (osc_ref[...] * _silu(gate_ref[0, rows_out, :])).astype(o_ref.dtype)

    @pl.when(k == 0)
    def _():
        prepare(0, q0_ref, c0_ref, r0_ref, True, False)

    @pl.when(lax.rem(2 * k, nt) == 0)
    def _():
        s_ref[0] = jnp.zeros((DN_HEADS, hd, hd), F32)

    prepare(1, qa_ref, ca_ref, ra_ref, False, True)
    finish(0, 0)
    prepare(0, qb_ref, cb_ref, rb_ref, lax.rem(jnp.minimum(2 * k + 2, n_tiles - 1), nt) == 0, False)
    finish(1, 1)


def _dn_group_pipelined(proj3, bdc3, bdr4, qkv_w, p_row, p_col, norm_w, tile, chunk):
    b, l, _ = proj3.shape
    nt = l // tile
    assert nt % 2 == 0
    n_tiles = b * nt
    nchunk = tile // chunk
    hd = DN_HEAD_DIM
    n_prob = nchunk * DN_HEADS
    proj_t = proj3.reshape(n_tiles, tile, N_MAIN)
    proj_p = proj3.reshape(n_tiles // 2, 2 * tile, N_MAIN)
    bdc_t = bdc3.reshape(n_tiles, tile, SMALL_PAD)
    bdr_t = bdr4.reshape(n_tiles, nchunk, 2 * DN_HEADS, chunk)
    odd = lambda k: 2 * k + 1
    nxt = lambda k: jnp.minimum(2 * k + 2, n_tiles - 1)
    once = pl.Buffered(1)

    def tile_specs(index):
        mode = once if index is None else None
        at = (lambda k: 0) if index is None else index
        return [pl.BlockSpec((1, tile, 3 * DN_WIDTH), lambda k: (at(k), 0, QKV_BLK), pipeline_mode=mode),
                pl.BlockSpec((1, tile, SMALL_PAD), lambda k: (at(k), 0, 0), pipeline_mode=mode),
                pl.BlockSpec((1, nchunk, 2 * DN_HEADS, chunk), lambda k: (at(k), 0, 0, 0), pipeline_mode=mode)]

    (q0, c0, r0), (qa, ca, ra), (qb, cb, rb) = tile_specs(None), tile_specs(odd), tile_specs(nxt)
    const2 = lambda k: (0, 0)
    od, qkv_state, s_final = pl.pallas_call(
        functools.partial(_dn_pipelined_kernel, tile=tile, chunk=chunk, nt=nt, n_tiles=n_tiles),
        grid=(n_tiles // 2,),
        in_specs=[q0, qa, qb, c0, ca, cb, r0, ra, rb,
                  pl.BlockSpec((1, 2 * tile, DN_WIDTH), lambda k: (k, 0, DNGATE_BLK)),
                  pl.BlockSpec((DN_CONV_K, 3 * DN_WIDTH), const2),
                  pl.BlockSpec((2, DN_HEADS), const2),
                  pl.BlockSpec((DN_HEADS, 2), const2),
                  pl.BlockSpec((1, hd), const2)],
        out_specs=[pl.BlockSpec((1, 2 * tile, DN_WIDTH), lambda k: (k, 0, 0)),
                   pl.BlockSpec((1, DN_CONV_K - 1, 3 * DN_WIDTH), lambda k: (2 * k // nt, 0, 0)),
                   pl.BlockSpec((1, DN_HEADS, hd, hd), lambda k: (2 * k // nt, 0, 0, 0))],
        out_shape=[jax.ShapeDtypeStruct((n_tiles // 2, 2 * tile, DN_WIDTH), BF16),
                   jax.ShapeDtypeStruct((b, DN_CONV_K - 1, 3 * DN_WIDTH), F32),
                   jax.ShapeDtypeStruct((b, DN_HEADS, hd, hd), F32)],
        scratch_shapes=[pltpu.VMEM((tile + QKV_HALO, 3 * DN_WIDTH), F32),
                        pltpu.VMEM((2, n_prob, 2 * chunk, hd), BF16),
                        pltpu.VMEM((2, n_prob, chunk, hd), BF16),
                        pltpu.VMEM((2, n_prob, chunk, 2 * hd), BF16),
                        pltpu.VMEM((2, n_prob, chunk, hd), F32),
                        pltpu.VMEM((2, n_prob, chunk, hd), BF16),
                        pltpu.VMEM((2, nchunk, chunk, DN_HEADS), F32),
                        pltpu.VMEM((2, nchunk, DN_HEADS, chunk), F32),
                        pltpu.VMEM((tile, DN_WIDTH), F32)],
        compiler_params=_cparams(("arbitrary",)),
        name="deltanet_pipelined",
    )(proj_t, proj_t, proj_t, bdc_t, bdc_t, bdc_t, bdr_t, bdr_t, bdr_t, proj_p, qkv_w, p_row, p_col, norm_w)
    return od.reshape(b, l, DN_WIDTH), qkv_state, s_final


def _mem_kernel(q_ref, g_ref, k_ref, v_ref, o_ref, *, nb):
    hd = MEM_HEAD_DIM
    scale = hd ** -0.5
    probs = [(i, h) for i in range(nb) for h in range(MEM_HEADS)]
    lanes = [slice(h * hd, (h + 1) * hd) for h in range(MEM_HEADS)]
    qs = [q_ref[i, :, lanes[h]] for i, h in probs]
    ks = [k_ref[i, pl.ds(h, MEM_TOKENS, stride=MEM_HEADS), :] for i, h in probs]
    vs = [v_ref[i, pl.ds(h, MEM_TOKENS, stride=MEM_HEADS), :] for i, h in probs]
    ss = [_bdot(q, k, _NT) * scale for q, k in zip(qs, ks)]
    es = [jnp.exp(s - jnp.max(s, axis=-1, keepdims=True)) for s in ss]
    oms = [_bdot(e, v) / jnp.sum(e, axis=-1, keepdims=True) for e, v in zip(es, vs)]
    for (i, h), om in zip(probs, oms):
        o_ref[i, :, lanes[h]] = (om * _silu(g_ref[i, :, lanes[h]])).astype(o_ref.dtype)


def _mem_group(proj3, mem_k, mem_v, tile, nb):
    b, l, _ = proj3.shape
    cb = COL_BLOCK
    rows = MEM_TOKENS * MEM_HEADS
    return pl.pallas_call(
        functools.partial(_mem_kernel, nb=nb),
        grid=(b // nb, l // tile),
        in_specs=[pl.BlockSpec((nb, tile, cb), lambda i, t: (i, t, MEMQ_BLK)),
                  pl.BlockSpec((nb, tile, cb), lambda i, t: (i, t, MEMG_BLK)),
                  pl.BlockSpec((nb, rows, MEM_HEAD_DIM), lambda i, t: (i, 0, 0)),
                  pl.BlockSpec((nb, rows, MEM_HEAD_DIM), lambda i, t: (i, 0, 0))],
        out_specs=pl.BlockSpec((nb, tile, MEM_WIDTH), lambda i, t: (i, t, 0)),
        out_shape=jax.ShapeDtypeStruct((b, l, MEM_WIDTH), BF16),
        compiler_params=_cparams(("arbitrary", "arbitrary")),
        name="memory_group",
    )(proj3, proj3, mem_k, mem_v)


def _out_kernel(oc_ref, od_ref, om_ref, x_ref, w_ref, p_ref, y_ref):
    half = x_ref.shape[0] // 2
    for r in range(2):
        rows = slice(r * half, (r + 1) * half)
        h = jnp.dot(oc_ref[rows, :], w_ref[0:C_CONV, :], preferred_element_type=F32)
        h = h + jnp.dot(od_ref[rows, :], w_ref[C_CONV:C_CONV + DN_WIDTH, :], preferred_element_type=F32)
        h = h + jnp.dot(om_ref[rows, :], w_ref[C_CONV + DN_WIDTH:, :], preferred_element_type=F32)
        z = DEEPNORM_ALPHA * x_ref[rows, :] + h
        mu = jnp.mean(z, axis=-1, keepdims=True)
        zc = z - mu
        var = jnp.mean(zc * zc, axis=-1, keepdims=True)
        y_ref[rows, :] = zc * lax.rsqrt(var + LN_EPS) * p_ref[0:1, :] + p_ref[1:2, :]


def _out_proj(oc, od, om, x2, w_out, ln_p, bm):
    m = x2.shape[0]
    return pl.pallas_call(
        _out_kernel,
        grid=(m // bm,),
        in_specs=[pl.BlockSpec((bm, C_CONV), lambda i: (i, 0)),
                  pl.BlockSpec((bm, DN_WIDTH), lambda i: (i, 0)),
                  pl.BlockSpec((bm, MEM_WIDTH), lambda i: (i, 0)),
                  pl.BlockSpec((bm, D_MODEL), lambda i: (i, 0)),
                  pl.BlockSpec((D_MODEL, D_MODEL), lambda i: (0, 0)),
                  pl.BlockSpec((PARAM_ROWS, D_MODEL), lambda i: (0, 0))],
        out_specs=pl.BlockSpec((bm, D_MODEL), lambda i: (i, 0)),
        out_shape=jax.ShapeDtypeStruct((m, D_MODEL), F32),
        compiler_params=_cparams(("arbitrary",)),
        name="out_proj_ln",
    )(oc, od, om, x2, w_out, ln_p)


def _pad_rows(rows, n):
    a = jnp.stack(rows).astype(F32)
    return jnp.pad(a, ((0, n - a.shape[0]), (0, 0)))


def _project(x, w_in_t, layer, bm, dn_chunk):
    b, l, _ = x.shape
    proj, bdc, bdr = _in_proj(x.reshape(b * l, D_MODEL), w_in_t, layer, bm)
    bdr4 = bdr.reshape(2 * DN_HEADS, b, l // dn_chunk, dn_chunk).transpose(1, 2, 0, 3)
    return proj.reshape(b, l, N_MAIN), bdc.reshape(b, l, SMALL_PAD), bdr4


def _mix(x, oc, od, om, w_out, ln_p, out_bm):
    b, l, _ = x.shape
    y = _out_proj(oc.reshape(b * l, C_CONV), od.reshape(b * l, DN_WIDTH), om.reshape(b * l, MEM_WIDTH),
                  x.reshape(b * l, D_MODEL), w_out, ln_p, out_bm)
    return y.reshape(b, l, D_MODEL)


def kernel(x_prompt, x_sample, mem_prompt, state_conv, state_qkv_conv, state_delta, cache_mem_k, cache_mem_v,
           w_in, conv_w, conv_b, conv_ln_g, conv_ln_b, qkv_conv_w, a_log, dt_bias, delta_norm_w,
           w_mem_k, w_mem_v, w_out, ln_g, ln_b):
    bp = x_prompt.shape[0]
    bs = x_sample.shape[0]
    hp, hs = x_prompt, x_sample
    outs = [[] for _ in range(8)]
    w_in_t = jnp.swapaxes(w_in, 1, 2)
    kv_rows = MEM_TOKENS * MEM_HEADS
    ls = x_sample.shape[1]
    for i in range(DEPTH):
        conv_p = _pad_rows([conv_b[i], conv_ln_g[i], conv_ln_b[i]], PARAM_ROWS)
        p_row = jnp.stack([a_log[i], dt_bias[i]]).astype(F32)
        p_col = p_row.T
        norm_w = delta_norm_w[i].reshape(1, DN_HEAD_DIM)
        w_out_i = w_out[i].astype(BF16)
        ln_p = _pad_rows([ln_g[i], ln_b[i]], PARAM_ROWS)

        w_kv = jnp.concatenate([w_mem_k[i], w_mem_v[i]], axis=1).astype(BF16)
        kv = _matmul(mem_prompt.reshape(bp * MEM_TOKENS, D_MODEL), w_kv, KV_PROJ_BLOCK, KV_PROJ_BLOCK)
        mk = kv[:, :MEM_WIDTH].reshape(bp, MEM_TOKENS, MEM_HEADS, MEM_HEAD_DIM)
        mv = kv[:, MEM_WIDTH:].reshape(bp, MEM_TOKENS, MEM_HEADS, MEM_HEAD_DIM)
        sk = cache_mem_k[i].reshape(bs, kv_rows, MEM_HEAD_DIM)
        sv = cache_mem_v[i].reshape(bs, kv_rows, MEM_HEAD_DIM)

        proj_p, bdc_p, bdr_p = _project(hp, w_in_t, i, PROMPT_PROJ_ROWS, PROMPT_DN_CHUNK)
        proj_s, bdc_s, bdr_s = _project(hs, w_in_t, i, bs * ls, ls)

        oc_p, conv_st_p, om_s = _conv_group(proj_p, None, i, conv_w[i], conv_p, PROMPT_CONV_TILE, 1,
                                            guest=(proj_s, sk, sv, SAMPLE_SEQS))
        od_p, qkv_st_p, s_p = _dn_group_pipelined(proj_p, bdc_p, bdr_p, qkv_conv_w[i], p_row, p_col, norm_w,
                                                  PROMPT_DN_TILE, PROMPT_DN_CHUNK)
        om_p = _mem_group(proj_p, mk.reshape(bp, kv_rows, MEM_HEAD_DIM), mv.reshape(bp, kv_rows, MEM_HEAD_DIM),
                          PROMPT_MEM_TILE, 1)
        hp_next = _mix(hp, oc_p, od_p, om_p, w_out_i, ln_p, OUT_ROWS)
        outs[0].append(conv_st_p); outs[1].append(qkv_st_p); outs[2].append(s_p)
        outs[3].append(mk)
        outs[4].append(mv)

        oc_s, conv_st_s = _conv_group(proj_s, state_conv, i, conv_w[i], conv_p, ls, SAMPLE_SEQS)
        od_s, qkv_st_s, s_s = _dn_group(proj_s, bdc_s, bdr_s, state_qkv_conv, state_delta, i, qkv_conv_w[i],
                                        p_row, p_col, norm_w, ls, ls, SAMPLE_DN_SEQS)
        hs_next = _mix(hs, oc_s, od_s, om_s, w_out_i, ln_p, OUT_ROWS)
        outs[5].append(conv_st_s); outs[6].append(qkv_st_s); outs[7].append(s_s)
        hp, hs = hp_next, hs_next
    return (hp, hs) + tuple(jnp.stack(o) for o in outs)
```

```python
import functools

import jax
import jax.numpy as jnp
from jax import lax
from jax.experimental import pallas as pl
from jax.experimental.pallas import tpu as pltpu

F32 = jnp.float32
BF16 = jnp.bfloat16

D_MODEL = 2048
C_CONV = 512
CONV_K = 31
DN_HEADS = 8
DN_HEAD_DIM = 128
DN_WIDTH = DN_HEADS * DN_HEAD_DIM
DN_CONV_K = 4
MEM_HEADS = 4
MEM_HEAD_DIM = 128
MEM_WIDTH = MEM_HEADS * MEM_HEAD_DIM
MEM_TOKENS = 256
DEPTH = 1
DEEPNORM_ALPHA = (2 * DEPTH) ** 0.25
LN_EPS = 1e-5
NORM_EPS = 1e-6

N_MAIN = 3 * DN_WIDTH + DN_WIDTH + 3 * C_CONV + 2 * MEM_WIDTH
COL_BLOCK = 512
QKV_BLK = 0
DNGATE_BLK = 3
GLUA_BLK, GLUB_BLK, CGATE_BLK, MEMQ_BLK, MEMG_BLK = 8, 9, 10, 11, 12
SMALL_PAD = 128
N_DIRECT_BLOCKS = (3 * C_CONV + 4 * DN_WIDTH) // COL_BLOCK
CONV_HALO = 32
QKV_HALO = 8
SUBLANES = 8
CONV_ROW_BLOCK = 64
PARAM_ROWS = SUBLANES

PROMPT_PROJ_ROWS = 2048
PROMPT_CONV_TILE = 512
PROMPT_DN_TILE = 256
PROMPT_DN_CHUNK = 64
PROMPT_MEM_TILE = 1024
OUT_ROWS = 512
KV_PROJ_BLOCK = 512
SAMPLE_SEQS = 8
SAMPLE_CONV_SEQS = 16

VMEM_LIMIT = 56 * 1024 * 1024


def _sigmoid(x):
    return 1.0 / (1.0 + jnp.exp(-x))


def _silu(x):
    return x * _sigmoid(x)


def _softplus(x):
    return jnp.maximum(x, 0.0) + jnp.log(1.0 + jnp.exp(-jnp.abs(x)))


def _cparams(sem):
    return pltpu.CompilerParams(dimension_semantics=sem, vmem_limit_bytes=VMEM_LIMIT)


def _proj_kernel(x_hbm, wa_ref, wb_ref, o_ref, bdc_ref, bdr_ref, xf_ref, xb_ref, sem):
    i = pl.program_id(0)
    j = pl.program_id(1)
    bm = xf_ref.shape[0]
    nt = (((1,), (1,)), ((), ()))
    n_bd = 2 * DN_HEADS

    def x_copy(block):
        return pltpu.make_async_copy(x_hbm.at[pl.ds(block * bm, bm), :], xf_ref, sem)

    @pl.when(j == 0)
    def _():
        @pl.when(i == 0)
        def _():
            x_copy(0).start()

        x_copy(i).wait()
        xb_ref[...] = xf_ref[...].astype(BF16)

        @pl.when(i + 1 < pl.num_programs(0))
        def _():
            x_copy(i + 1).start()

    @pl.when(j < N_DIRECT_BLOCKS)
    def _():
        o_ref[...] = lax.dot_general(xb_ref[...], wa_ref[...].astype(BF16), nt, preferred_element_type=F32)

    @pl.when(j >= N_DIRECT_BLOCKS)
    def _():
        w = jnp.concatenate([wa_ref[n_bd:, :], wb_ref[...]], axis=0).astype(BF16)
        o_ref[...] = lax.dot_general(xb_ref[...], w, nt, preferred_element_type=F32)

    @pl.when(j == N_DIRECT_BLOCKS)
    def _():
        bd = lax.dot_general(xb_ref[...], wa_ref[0:SMALL_PAD, :].astype(BF16), nt, preferred_element_type=F32)
        bdc_ref[...] = bd
        bdr_ref[...] = bd.T[0:n_bd, :]


def _proj_out_block(j):
    n_conv = 3 * C_CONV // COL_BLOCK
    return jnp.where(j < n_conv, j + GLUA_BLK, jnp.where(j < N_DIRECT_BLOCKS, j - n_conv, j))


def _in_proj(x2, w_in_t, layer, bm):
    m = x2.shape[0]
    bn = COL_BLOCK
    n_blocks = N_MAIN // bn
    n_bd = 2 * DN_HEADS
    return pl.pallas_call(
        _proj_kernel,
        grid=(m // bm, n_blocks),
        in_specs=[
            pl.BlockSpec(memory_space=pl.ANY),
            pl.BlockSpec((None, bn, D_MODEL), lambda i, j: (layer, j, 0)),
            pl.BlockSpec((None, n_bd, D_MODEL),
                         lambda i, j: (layer, (jnp.maximum(j + 1, n_blocks - 1)) * (bn // n_bd), 0)),
        ],
        out_specs=[
            pl.BlockSpec((bm, bn), lambda i, j: (i, _proj_out_block(j))),
            pl.BlockSpec((bm, SMALL_PAD), lambda i, j: (i, 0)),
            pl.BlockSpec((2 * DN_HEADS, bm), lambda i, j: (0, i)),
        ],
        out_shape=[
            jax.ShapeDtypeStruct((m, N_MAIN), F32),
            jax.ShapeDtypeStruct((m, SMALL_PAD), F32),
            jax.ShapeDtypeStruct((2 * DN_HEADS, m), F32),
        ],
        scratch_shapes=[pltpu.VMEM((bm, D_MODEL), F32), pltpu.VMEM((bm, D_MODEL), BF16),
                        pltpu.SemaphoreType.DMA(())],
        compiler_params=_cparams(("arbitrary", "arbitrary")),
        name="in_proj",
    )(x2, w_in_t, w_in_t)


def _mm_kernel(x_ref, w_ref, o_ref):
    o_ref[...] = jnp.dot(x_ref[...].astype(BF16), w_ref[...], preferred_element_type=F32)


def _matmul(x2, w, bm, bn):
    m, k = x2.shape
    n = w.shape[1]
    return pl.pallas_call(
        _mm_kernel,
        grid=(m // bm, n // bn),
        in_specs=[pl.BlockSpec((bm, k), lambda i, j: (i, 0)),
                  pl.BlockSpec((k, bn), lambda i, j: (0, j))],
        out_specs=pl.BlockSpec((bm, bn), lambda i, j: (i, j)),
        out_shape=jax.ShapeDtypeStruct((m, n), F32),
        compiler_params=_cparams(("arbitrary", "arbitrary")),
        name="mem_kv_proj",
    )(x2, w)


def _conv_kernel(*refs, tile, nb, has_state, guest_nb):
    refs = list(refs)
    a_ref, b_ref, g_ref = refs[:3]
    del refs[:3]
    st_ref = refs.pop(0) if has_state else None
    w_ref, p_ref = refs[:2]
    del refs[:2]
    if guest_nb:
        gq_ref, gg_ref, gk_ref, gv_ref = refs[:4]
        del refs[:4]
        o_ref, so_ref, go_ref, xp_ref = refs
        _mem_kernel(gq_ref, gg_ref, gk_ref, gv_ref, go_ref, nb=guest_nb)
    else:
        o_ref, so_ref, xp_ref = refs
    t = pl.program_id(1)
    last = pl.num_programs(1) - 1
    lo = CONV_HALO - (CONV_K - 1)
    n = tile + CONV_HALO
    rb = min(CONV_ROW_BLOCK, tile)
    w = w_ref[...]
    conv_b = p_ref[0:1, :]
    ln_g = p_ref[1:2, :]
    ln_b = p_ref[2:3, :]
    for i in range(nb):
        @pl.when(t == 0)
        def _():
            xp_ref[i, 0, 0:CONV_HALO, :] = jnp.zeros((CONV_HALO, C_CONV), F32)
            if has_state:
                xp_ref[i, 0, lo:CONV_HALO, :] = st_ref[:, i, :]

        xp_ref[i, 0, CONV_HALO:n, :] = a_ref[i] * _sigmoid(b_ref[i])
        xp = xp_ref[i, 0]
        for s in range(1, SUBLANES):
            xp_ref[i, s] = pltpu.roll(xp, n - s, axis=0)

        for r in range(tile // rb):
            acc = None
            for j in range(CONV_K):
                s = (lo + j) % SUBLANES
                start = lo + j - s + r * rb
                term = xp_ref[i, s, start:start + rb, :] * w[j:j + 1, :]
                acc = term if acc is None else acc + term
            hc = acc + conv_b
            mu = jnp.mean(hc, axis=-1, keepdims=True)
            xc = hc - mu
            var = jnp.mean(xc * xc, axis=-1, keepdims=True)
            hn = xc * lax.rsqrt(var + LN_EPS) * ln_g + ln_b
            rows = slice(r * rb, (r + 1) * rb)
            o_ref[i, rows, :] = (_silu(hn) * _silu(g_ref[i, rows, :])).astype(o_ref.dtype)

        @pl.when(t == last)
        def _():
            so_ref[i] = xp_ref[i, 0, tile + lo:n, :]

        xp_ref[i, 0, 0:CONV_HALO, :] = xp_ref[i, 0, tile:n, :]


def _conv_group(proj3, state, layer, conv_w, conv_p, tile, nb, guest=None):
    b, l, _ = proj3.shape
    has_state = state is not None
    cb = COL_BLOCK
    nrow = CONV_K - 1
    nt = l // tile
    in_specs = [
        pl.BlockSpec((nb, tile, cb), lambda i, t: (i, t, GLUA_BLK)),
        pl.BlockSpec((nb, tile, cb), lambda i, t: (i, t, GLUB_BLK)),
        pl.BlockSpec((nb, tile, cb), lambda i, t: (i, t, CGATE_BLK)),
    ]
    args = [proj3, proj3, proj3]
    if has_state:
        in_specs.append(pl.BlockSpec((None, nrow, nb, C_CONV), lambda i, t: (layer, 0, i, 0)))
        args.append(jnp.swapaxes(state, 1, 2))
    in_specs += [pl.BlockSpec((CONV_K, C_CONV), lambda i, t: (0, 0)),
                 pl.BlockSpec((PARAM_ROWS, C_CONV), lambda i, t: (0, 0))]
    args += [conv_w, conv_p]
    out_specs = [pl.BlockSpec((nb, tile, C_CONV), lambda i, t: (i, t, 0)),
                 pl.BlockSpec((nb, nrow, C_CONV), lambda i, t: (i, 0, 0))]
    out_shape = [jax.ShapeDtypeStruct((b, l, C_CONV), BF16),
                 jax.ShapeDtypeStruct((b, nrow, C_CONV), F32)]
    guest_nb = 0
    if guest is not None:
        g_proj3, g_k, g_v, guest_nb = guest
        gb, gl, _ = g_proj3.shape
        assert gb // guest_nb == (b // nb) * nt
        kv_rows = MEM_TOKENS * MEM_HEADS
        step = lambda i, t: i * nt + t
        in_specs += [pl.BlockSpec((guest_nb, gl, cb), lambda i, t: (step(i, t), 0, MEMQ_BLK)),
                     pl.BlockSpec((guest_nb, gl, cb), lambda i, t: (step(i, t), 0, MEMG_BLK)),
                     pl.BlockSpec((guest_nb, kv_rows, MEM_HEAD_DIM), lambda i, t: (step(i, t), 0, 0)),
                     pl.BlockSpec((guest_nb, kv_rows, MEM_HEAD_DIM), lambda i, t: (step(i, t), 0, 0))]
        args += [g_proj3, g_proj3, g_k, g_v]
        out_specs.append(pl.BlockSpec((guest_nb, gl, MEM_WIDTH), lambda i, t: (step(i, t), 0, 0)))
        out_shape.append(jax.ShapeDtypeStruct((gb, gl, MEM_WIDTH), BF16))
    return pl.pallas_call(
        functools.partial(_conv_kernel, tile=tile, nb=nb, has_state=has_state, guest_nb=guest_nb),
        grid=(b // nb, nt),
        in_specs=in_specs,
        out_specs=out_specs,
        out_shape=out_shape,
        scratch_shapes=[pltpu.VMEM((nb, SUBLANES, tile + CONV_HALO, C_CONV), F32)],
        compiler_params=_cparams(("arbitrary", "arbitrary")),
        name="conv_group",
    )(*args)


def _bdot(a, b, dims=None):
    a = a.astype(BF16)
    b = b.astype(BF16)
    if dims is None:
        return jnp.dot(a, b, preferred_element_type=F32)
    return lax.dot_general(a, b, (dims, ((), ())), preferred_element_type=F32)


_NT = ((1,), (1,))
_TN = ((0,), (0,))


def _split_bf16(a):
    hi = a.astype(BF16)
    return hi, (a - hi.astype(F32)).astype(BF16)


def _unit_lower_inverses(ms, c):
    row = lax.broadcasted_iota(jnp.int32, (c, c), 0)
    col = lax.broadcasted_iota(jnp.int32, (c, c), 1)
    eye = (row == col).astype(F32)
    ps = [-m for m in ms]
    ts = [eye + p for p in ps]
    span = 2
    while span < c:
        ps = [_bdot(p, p) for p in ps]
        ts = [t + _bdot(t, p) for t, p in zip(ts, ps)]
        span *= 2
    msp = [_split_bf16(m) for m in ms]
    tsp = [_split_bf16(t) for t in ts]
    mts = [_bdot(mh, th) + _bdot(mh, tl) + _bdot(ml, th) for (mh, ml), (th, tl) in zip(msp, tsp)]
    rs = [eye - t - mt for t, mt in zip(ts, mts)]
    return [t + _bdot(th, r) for t, (th, _), r in zip(ts, tsp, rs)]


def _dn_kernel(qkv_ref, gate_ref, bdc_ref, bdr_ref, prev_ref, s0_ref, w_ref, pr_ref, pc_ref, nw_ref,
               o_ref, st_ref, s_ref, xp_ref, xs_ref, osc_ref, *, tile, chunk, nb):
    c = chunk
    nchunk = tile // c
    hd = DN_HEAD_DIM
    t = pl.program_id(1)
    last = pl.num_programs(1) - 1
    lo = QKV_HALO - (DN_CONV_K - 1)
    hi = lax.Precision.HIGHEST
    w = w_ref[...]

    for i in range(nb):
        @pl.when(t == 0)
        def _():
            xp_ref[i, 0:QKV_HALO, :] = jnp.zeros((QKV_HALO, 3 * DN_WIDTH), F32)
            xp_ref[i, lo:QKV_HALO, :] = prev_ref[:, i, :]
            s_ref[i] = s0_ref[i]

        xp_ref[i, QKV_HALO:QKV_HALO + tile, :] = qkv_ref[i]
        xp = xp_ref[i]
        for j in range(DN_CONV_K - 1):
            xs_ref[i, j] = pltpu.roll(xp, tile + QKV_HALO - (lo + j), axis=0)

        @pl.when(t == last)
        def _():
            st_ref[i] = xp_ref[i, tile + lo:tile + QKV_HALO, :]

    neg_a_row = -jnp.exp(pr_ref[0:1, :])
    dtb_row = pr_ref[1:2, :]
    neg_a_col = -jnp.exp(pc_ref[:, 0:1])
    dtb_col = pc_ref[:, 1:2]
    norm_w = nw_ref[...]

    row = lax.broadcasted_iota(jnp.int32, (c, c), 0)
    col = lax.broadcasted_iota(jnp.int32, (c, c), 1)
    incl = row >= col
    strict = row > col
    tri_l = incl.astype(F32)
    tri_u = (row <= col).astype(F32)

    seq_chunks = [(i, n) for i in range(nb) for n in range(nchunk)]
    probs = [(i, n, h) for i, n in seq_chunks for h in range(DN_HEADS)]
    beta, gcum_col, gcum_row, prep = {}, {}, {}, {}
    for i, n in seq_chunks:
        rows = slice(n * c, (n + 1) * c)
        bd = bdc_ref[i, rows, :]
        beta[i, n] = _sigmoid(bd[:, 0:DN_HEADS])
        g_col = neg_a_row * _softplus(bd[:, DN_HEADS:2 * DN_HEADS] + dtb_row)
        gcum_col[i, n] = jnp.dot(tri_l, g_col, precision=hi, preferred_element_type=F32)
        br = bdr_ref[i, n]
        g_row = neg_a_col * _softplus(br[DN_HEADS:2 * DN_HEADS, :] + dtb_col)
        gcum_row[i, n] = jnp.dot(g_row, tri_u, precision=hi, preferred_element_type=F32)
        acc = xp_ref[i, QKV_HALO + n * c:QKV_HALO + (n + 1) * c, :] * w[DN_CONV_K - 1:DN_CONV_K, :]
        for j in range(DN_CONV_K - 1):
            acc = acc + xs_ref[i, j, rows, :] * w[j:j + 1, :]
        qkvc = _silu(acc)
        for h in range(DN_HEADS):
            q, k, v = (qkvc[:, part * DN_WIDTH + h * hd:part * DN_WIDTH + (h + 1) * hd] for part in range(3))
            q = q * (lax.rsqrt(jnp.sum(q * q, axis=-1, keepdims=True) + NORM_EPS) * (hd ** -0.5))
            k = k * lax.rsqrt(jnp.sum(k * k, axis=-1, keepdims=True) + NORM_EPS)
            gc = gcum_col[i, n][:, h:h + 1]
            gr = gcum_row[i, n][h:h + 1, :]
            bt = beta[i, n][:, h:h + 1]
            gl = gcum_col[i, n][c - 1:c, h:h + 1]
            e = jnp.exp(gc)
            kb = k * bt
            prep[i, n, h] = (
                jnp.concatenate([kb, q], axis=0).astype(BF16),
                k.astype(BF16),
                jnp.exp(jnp.where(incl, gc - gr, -jnp.inf)),
                jnp.concatenate([v * bt, kb * e], axis=1).astype(BF16),
                q * e,
                (k * jnp.exp(gl - gc)).astype(BF16),
                jnp.exp(gl))

    preps = [prep[p] for p in probs]
    kqs = [_bdot(p[0], p[1], _NT) for p in preps]
    aqks = [(kq[c:] * p[2]).astype(BF16) for kq, p in zip(kqs, preps)]
    ms = [jnp.where(strict, kq[:c] * p[2], 0.0) for kq, p in zip(kqs, preps)]
    tinvs = _unit_lower_inverses(ms, c)
    uws = [_bdot(tinv, p[3]) for tinv, p in zip(tinvs, preps)]
    wqs = [jnp.concatenate([uw[:, hd:], p[4]], axis=0).astype(BF16) for uw, p in zip(uws, preps)]
    local = dict(zip(probs, zip(uws, wqs, aqks, [p[5] for p in preps], [p[6] for p in preps])))

    heads = [(i, h) for i in range(nb) for h in range(DN_HEADS)]
    state = {(i, h): s_ref[i, h] for i, h in heads}
    for n in range(nchunk):
        rows = slice(n * c, (n + 1) * c)
        items = [(i, h) + local[i, n, h] for i, h in heads]
        s16s = [state[i, h].astype(BF16) for i, h in heads]
        wss = [_bdot(it[3], s16) for it, s16 in zip(items, s16s)]
        u16s = [(it[2][:, :hd] - ws[:c]).astype(BF16) for it, ws in zip(items, wss)]
        outs = [ws[c:] + _bdot(it[4], u16) for it, ws, u16 in zip(items, wss, u16s)]
        for it, u16 in zip(items, u16s):
            i, h = it[0], it[1]
            state[i, h] = state[i, h] * it[6] + _bdot(it[5], u16, _TN)
        for it, o in zip(items, outs):
            i, h = it[0], it[1]
            o = o * lax.rsqrt(jnp.mean(o * o, axis=-1, keepdims=True) + NORM_EPS) * norm_w
            osc_ref[i, rows, h * hd:(h + 1) * hd] = o
    for i, h in heads:
        s_ref[i, h] = state[i, h]

    for i in range(nb):
        xp_ref[i, 0:QKV_HALO, :] = xp_ref[i, tile:tile + QKV_HALO, :]
        o_ref[i] = (osc_ref[i] * _silu(gate_ref[i])).astype(o_ref.dtype)


def _dn_group(proj3, bdc3, bdr4, prev, s0, layer, qkv_w, p_row, p_col, norm_w, tile, chunk, nb):
    b, l, _ = proj3.shape
    nchunk = tile // chunk
    hd = DN_HEAD_DIM
    nrow = DN_CONV_K - 1
    const2 = lambda i, t: (0, 0)
    od, qkv_state, s_final = pl.pallas_call(
        functools.partial(_dn_kernel, tile=tile, chunk=chunk, nb=nb),
        grid=(b // nb, l // tile),
        in_specs=[
            pl.BlockSpec((nb, tile, 3 * DN_WIDTH), lambda i, t: (i, t, QKV_BLK)),
            pl.BlockSpec((nb, tile, DN_WIDTH), lambda i, t: (i, t, DNGATE_BLK)),
            pl.BlockSpec((nb, tile, SMALL_PAD), lambda i, t: (i, t, 0)),
            pl.BlockSpec((nb, nchunk, 2 * DN_HEADS, chunk), lambda i, t: (i, t, 0, 0)),
            pl.BlockSpec((None, nrow, nb, 3 * DN_WIDTH), lambda i, t: (layer, 0, i, 0)),
            pl.BlockSpec((None, nb, DN_HEADS, hd, hd), lambda i, t: (layer, i, 0, 0, 0)),
            pl.BlockSpec((DN_CONV_K, 3 * DN_WIDTH), const2),
            pl.BlockSpec((2, DN_HEADS), const2),
            pl.BlockSpec((DN_HEADS, 2), const2),
            pl.BlockSpec((1, hd), const2),
        ],
        out_specs=[pl.BlockSpec((nb, tile, DN_WIDTH), lambda i, t: (i, t, 0)),
                   pl.BlockSpec((nb, nrow, 3 * DN_WIDTH), lambda i, t: (i, 0, 0)),
                   pl.BlockSpec((nb, DN_HEADS, hd, hd), lambda i, t: (i, 0, 0, 0))],
        out_shape=[jax.ShapeDtypeStruct((b, l, DN_WIDTH), BF16),
                   jax.ShapeDtypeStruct((b, nrow, 3 * DN_WIDTH), F32),
                   jax.ShapeDtypeStruct((b, DN_HEADS, hd, hd), F32)],
        scratch_shapes=[pltpu.VMEM((nb, tile + QKV_HALO, 3 * DN_WIDTH), F32),
                        pltpu.VMEM((nb, DN_CONV_K - 1, tile + QKV_HALO, 3 * DN_WIDTH), F32),
                        pltpu.VMEM((nb, tile, DN_WIDTH), F32)],
        compiler_params=_cparams(("arbitrary", "arbitrary")),
        name="deltanet_group",
    )(proj3, proj3, bdc3, bdr4, jnp.swapaxes(prev, 1, 2), s0, qkv_w, p_row, p_col, norm_w)
    return od, qkv_state, s_final


def _dn_pipelined_kernel(q0_ref, qa_ref, qb_ref, c0_ref, ca_ref, cb_ref, r0_ref, ra_ref, rb_ref, gate_ref,
                         w_ref, pr_ref, pc_ref, nw_ref, o_ref, st_ref, s_ref,
                         xp_ref, kq_ref, k16_ref, uwr_ref, qe_ref, kd_ref, gcc_ref, gcr_ref, osc_ref,
                         *, tile, chunk, nt, n_tiles):
    c = chunk
    nchunk = tile // c
    hd = DN_HEAD_DIM
    lo = QKV_HALO - (DN_CONV_K - 1)
    hi = lax.Precision.HIGHEST
    k = pl.program_id(0)
    w = w_ref[...]
    neg_a_row = -jnp.exp(pr_ref[0:1, :])
    dtb_row = pr_ref[1:2, :]
    neg_a_col = -jnp.exp(pc_ref[:, 0:1])
    dtb_col = pc_ref[:, 1:2]
    norm_w = nw_ref[...]
    row = lax.broadcasted_iota(jnp.int32, (c, c), 0)
    col = lax.broadcasted_iota(jnp.int32, (c, c), 1)
    incl = row >= col
    strict = row > col
    tri_l = incl.astype(F32)
    tri_u = (row <= col).astype(F32)
    probs = [(n, h) for n in range(nchunk) for h in range(DN_HEADS)]

    def prepare(buf, q_ref, c_ref, r_ref, first, write_state):
        halo = xp_ref[0:QKV_HALO, :]
        xp_ref[0:QKV_HALO, :] = jnp.zeros_like(halo) if first is True else jnp.where(first, 0.0, halo)
        xp_ref[QKV_HALO:QKV_HALO + tile, :] = q_ref[0]
        if write_state:
            st_ref[0] = xp_ref[tile + lo:tile + QKV_HALO, :]
        for n in range(nchunk):
            rows = slice(n * c, (n + 1) * c)
            bd = c_ref[0, rows, :]
            beta = _sigmoid(bd[:, 0:DN_HEADS])
            g_col = neg_a_row * _softplus(bd[:, DN_HEADS:2 * DN_HEADS] + dtb_row)
            gcum_col = jnp.dot(tri_l, g_col, precision=hi, preferred_element_type=F32)
            br = r_ref[0, n]
            g_row = neg_a_col * _softplus(br[DN_HEADS:2 * DN_HEADS, :] + dtb_col)
            gcc_ref[buf, n] = gcum_col
            gcr_ref[buf, n] = jnp.dot(g_row, tri_u, precision=hi, preferred_element_type=F32)
            window = xp_ref[n * c:(n + 1) * c + QKV_HALO, :]
            acc = window[QKV_HALO:, :] * w[DN_CONV_K - 1:DN_CONV_K, :]
            for j in range(DN_CONV_K - 1):
                acc = acc + pltpu.roll(window, c + QKV_HALO - (lo + j), axis=0)[0:c, :] * w[j:j + 1, :]
            qkvc = _silu(acc)
            for h in range(DN_HEADS):
                p = n * DN_HEADS + h
                q, kk, v = (qkvc[:, part * DN_WIDTH + h * hd:part * DN_WIDTH + (h + 1) * hd] for part in range(3))
                q = q * (lax.rsqrt(jnp.sum(q * q, axis=-1, keepdims=True) + NORM_EPS) * (hd ** -0.5))
                kk = kk * lax.rsqrt(jnp.sum(kk * kk, axis=-1, keepdims=True) + NORM_EPS)
                gc = gcum_col[:, h:h + 1]
                bt = beta[:, h:h + 1]
                gl = gcum_col[c - 1:c, h:h + 1]
                e = jnp.exp(gc)
                kb = kk * bt
                kq_ref[buf, p] = jnp.concatenate([kb, q], axis=0).astype(BF16)
                k16_ref[buf, p] = kk.astype(BF16)
                uwr_ref[buf, p] = jnp.concatenate([v * bt, kb * e], axis=1).astype(BF16)
                qe_ref[buf, p] = q * e
                kd_ref[buf, p] = (kk * jnp.exp(gl - gc)).astype(BF16)
        xp_ref[0:QKV_HALO, :] = xp_ref[tile:tile + QKV_HALO, :]

    def finish(buf, half):
        idx = [n * DN_HEADS + h for n, h in probs]
        kqs = [_bdot(kq_ref[buf, p], k16_ref[buf, p], _NT) for p in idx]
        gcs = [gcc_ref[buf, n][:, h:h + 1] for n, h in probs]
        decays = [jnp.exp(jnp.where(incl, gc - gcr_ref[buf, n][h:h + 1, :], -jnp.inf))
                  for gc, (n, h) in zip(gcs, probs)]
        aqks = [(kq[c:] * dc).astype(BF16) for kq, dc in zip(kqs, decays)]
        ms = [jnp.where(strict, kq[:c] * dc, 0.0) for kq, dc in zip(kqs, decays)]
        tinvs = _unit_lower_inverses(ms, c)
        uws = [_bdot(tinv, uwr_ref[buf, p]) for tinv, p in zip(tinvs, idx)]
        wqs = [jnp.concatenate([uw[:, hd:], qe_ref[buf, p]], axis=0).astype(BF16) for uw, p in zip(uws, idx)]
        egls = [jnp.exp(gc[c - 1:c, :]) for gc in gcs]
        local = dict(zip(probs, zip(uws, wqs, aqks, egls)))
        state = [s_ref[0, h] for h in range(DN_HEADS)]
        for n in range(nchunk):
            rows = slice(n * c, (n + 1) * c)
            items = [local[n, h] for h in range(DN_HEADS)]
            s16s = [s.astype(BF16) for s in state]
            wss = [_bdot(it[1], s16) for it, s16 in zip(items, s16s)]
            u16s = [(it[0][:, :hd] - ws[:c]).astype(BF16) for it, ws in zip(items, wss)]
            outs = [ws[c:] + _bdot(it[2], u16) for it, ws, u16 in zip(items, wss, u16s)]
            state = [s * it[3] + _bdot(kd_ref[buf, n * DN_HEADS + h], u16, _TN)
                     for h, (s, it, u16) in enumerate(zip(state, items, u16s))]
            for h, o in enumerate(outs):
                o = o * lax.rsqrt(jnp.mean(o * o, axis=-1, keepdims=True) + NORM_EPS) * norm_w
                osc_ref[rows, h * hd:(h + 1) * hd] = o
        for h in range(DN_HEADS):
            s_ref[0, h] = state[h]
        rows_out = slice(half * tile, (half + 1) * tile)
        o_ref[0, rows_out, :] = (osc_ref[...] * _silu(gate_ref[0, rows_out, :])).astype(o_ref.dtype)

    @pl.when(k == 0)
    def _():
        prepare(0, q0_ref, c0_ref, r0_ref, True, False)

    @pl.when(lax.rem(2 * k, nt) == 0)
    def _():
        s_ref[0] = jnp.zeros((DN_HEADS, hd, hd), F32)

    prepare(1, qa_ref, ca_ref, ra_ref, False, True)
    finish(0, 0)
    prepare(0, qb_ref, cb_ref, rb_ref, lax.rem(jnp.minimum(2 * k + 2, n_tiles - 1), nt) == 0, False)
    finish(1, 1)


def _dn_group_pipelined(proj3, bdc3, bdr4, qkv_w, p_row, p_col, norm_w, tile, chunk):
    b, l, _ = proj3.shape
    nt = l // tile
    assert nt % 2 == 0
    n_tiles = b * nt
    nchunk = tile // chunk
    hd = DN_HEAD_DIM
    n_prob = nchunk * DN_HEADS
    proj_t = proj3.reshape(n_tiles, tile, N_MAIN)
    proj_p = proj3.reshape(n_tiles // 2, 2 * tile, N_MAIN)
    bdc_t = bdc3.reshape(n_tiles, tile, SMALL_PAD)
    bdr_t = bdr4.reshape(n_tiles, nchunk, 2 * DN_HEADS, chunk)
    odd = lambda k: 2 * k + 1
    nxt = lambda k: jnp.minimum(2 * k + 2, n_tiles - 1)
    once = pl.Buffered(1)

    def tile_specs(index):
        mode = once if index is None else None
        at = (lambda k: 0) if index is None else index
        return [pl.BlockSpec((1, tile, 3 * DN_WIDTH), lambda k: (at(k), 0, QKV_BLK), pipeline_mode=mode),
                pl.BlockSpec((1, tile, SMALL_PAD), lambda k: (at(k), 0, 0), pipeline_mode=mode),
                pl.BlockSpec((1, nchunk, 2 * DN_HEADS, chunk), lambda k: (at(k), 0, 0, 0), pipeline_mode=mode)]

    (q0, c0, r0), (qa, ca, ra), (qb, cb, rb) = tile_specs(None), tile_specs(odd), tile_specs(nxt)
    const2 = lambda k: (0, 0)
    od, qkv_state, s_final = pl.pallas_call(
        functools.partial(_dn_pipelined_kernel, tile=tile, chunk=chunk, nt=nt, n_tiles=n_tiles),
        grid=(n_tiles // 2,),
        in_specs=[q0, qa, qb, c0, ca, cb, r0, ra, rb,
                  pl.BlockSpec((1, 2 * tile, DN_WIDTH), lambda k: (k, 0, DNGATE_BLK)),
                  pl.BlockSpec((DN_CONV_K, 3 * DN_WIDTH), const2),
                  pl.BlockSpec((2, DN_HEADS), const2),
                  pl.BlockSpec((DN_HEADS, 2), const2),
                  pl.BlockSpec((1, hd), const2)],
        out_specs=[pl.BlockSpec((1, 2 * tile, DN_WIDTH), lambda k: (k, 0, 0)),
                   pl.BlockSpec((1, DN_CONV_K - 1, 3 * DN_WIDTH), lambda k: (2 * k // nt, 0, 0)),
                   pl.BlockSpec((1, DN_HEADS, hd, hd), lambda k: (2 * k // nt, 0, 0, 0))],
        out_shape=[jax.ShapeDtypeStruct((n_tiles // 2, 2 * tile, DN_WIDTH), BF16),
                   jax.ShapeDtypeStruct((b, DN_CONV_K - 1, 3 * DN_WIDTH), F32),
                   jax.ShapeDtypeStruct((b, DN_HEADS, hd, hd), F32)],
        scratch_shapes=[pltpu.VMEM((tile + QKV_HALO, 3 * DN_WIDTH), F32),
                        pltpu.VMEM((2, n_prob, 2 * chunk, hd), BF16),
                        pltpu.VMEM((2, n_prob, chunk, hd), BF16),
                        pltpu.VMEM((2, n_prob, chunk, 2 * hd), BF16),
                        pltpu.VMEM((2, n_prob, chunk, hd), F32),
                        pltpu.VMEM((2, n_prob, chunk, hd), BF16),
                        pltpu.VMEM((2, nchunk, chunk, DN_HEADS), F32),
                        pltpu.VMEM((2, nchunk, DN_HEADS, chunk), F32),
                        pltpu.VMEM((tile, DN_WIDTH), F32)],
        compiler_params=_cparams(("arbitrary",)),
        name="deltanet_pipelined",
    )(proj_t, proj_t, proj_t, bdc_t, bdc_t, bdc_t, bdr_t, bdr_t, bdr_t, proj_p, qkv_w, p_row, p_col, norm_w)
    return od.reshape(b, l, DN_WIDTH), qkv_state, s_final


def _mem_kernel(q_ref, g_ref, k_ref, v_ref, o_ref, *, nb):
    hd = MEM_HEAD_DIM
    scale = hd ** -0.5
    probs = [(i, h) for i in range(nb) for h in range(MEM_HEADS)]
    lanes = [slice(h * hd, (h + 1) * hd) for h in range(MEM_HEADS)]
    qs = [q_ref[i, :, lanes[h]] for i, h in probs]
    ks = [k_ref[i, pl.ds(h, MEM_TOKENS, stride=MEM_HEADS), :] for i, h in probs]
    vs = [v_ref[i, pl.ds(h, MEM_TOKENS, stride=MEM_HEADS), :] for i, h in probs]
    ss = [_bdot(q, k, _NT) * scale for q, k in zip(qs, ks)]
    es = [jnp.exp(s - jnp.max(s, axis=-1, keepdims=True)) for s in ss]
    oms = [_bdot(e, v) / jnp.sum(e, axis=-1, keepdims=True) for e, v in zip(es, vs)]
    for (i, h), om in zip(probs, oms):
        o_ref[i, :, lanes[h]] = (om * _silu(g_ref[i, :, lanes[h]])).astype(o_ref.dtype)


def _mem_group(proj3, mem_k, mem_v, tile, nb):
    b, l, _ = proj3.shape
    cb = COL_BLOCK
    rows = MEM_TOKENS * MEM_HEADS
    return pl.pallas_call(
        functools.partial(_mem_kernel, nb=nb),
        grid=(b // nb, l // tile),
        in_specs=[pl.BlockSpec((nb, tile, cb), lambda i, t: (i, t, MEMQ_BLK)),
                  pl.BlockSpec((nb, tile, cb), lambda i, t: (i, t, MEMG_BLK)),
                  pl.BlockSpec((nb, rows, MEM_HEAD_DIM), lambda i, t: (i, 0, 0)),
                  pl.BlockSpec((nb, rows, MEM_HEAD_DIM), lambda i, t: (i, 0, 0))],
        out_specs=pl.BlockSpec((nb, tile, MEM_WIDTH), lambda i, t: (i, t, 0)),
        out_shape=jax.ShapeDtypeStruct((b, l, MEM_WIDTH), BF16),
        compiler_params=_cparams(("arbitrary", "arbitrary")),
        name="memory_group",
    )(proj3, proj3, mem_k, mem_v)


def _out_kernel(oc_ref, od_ref, om_ref, x_ref, w_ref, p_ref, y_ref):
    half = x_ref.shape[0] // 2
    for r in range(2):
        rows = slice(r * half, (r + 1) * half)
        h = jnp.dot(oc_ref[rows, :], w_ref[0:C_CONV, :], preferred_element_type=F32)
        h = h + jnp.dot(od_ref[rows, :], w_ref[C_CONV:C_CONV + DN_WIDTH, :], preferred_element_type=F32)
        h = h + jnp.dot(om_ref[rows, :], w_ref[C_CONV + DN_WIDTH:, :], preferred_element_type=F32)
        z = DEEPNORM_ALPHA * x_ref[rows, :] + h
        mu = jnp.mean(z, axis=-1, keepdims=True)
        zc = z - mu
        var = jnp.mean(zc * zc, axis=-1, keepdims=True)
        y_ref[rows, :] = zc * lax.rsqrt(var + LN_EPS) * p_ref[0:1, :] + p_ref[1:2, :]


def _out_proj(oc, od, om, x2, w_out, ln_p, bm):
    m = x2.shape[0]
    return pl.pallas_call(
        _out_kernel,
        grid=(m // bm,),
        in_specs=[pl.BlockSpec((bm, C_CONV), lambda i: (i, 0)),
                  pl.BlockSpec((bm, DN_WIDTH), lambda i: (i, 0)),
                  pl.BlockSpec((bm, MEM_WIDTH), lambda i: (i, 0)),
                  pl.BlockSpec((bm, D_MODEL), lambda i: (i, 0)),
                  pl.BlockSpec((D_MODEL, D_MODEL), lambda i: (0, 0)),
                  pl.BlockSpec((PARAM_ROWS, D_MODEL), lambda i: (0, 0))],
        out_specs=pl.BlockSpec((bm, D_MODEL), lambda i: (i, 0)),
        out_shape=jax.ShapeDtypeStruct((m, D_MODEL), F32),
        compiler_params=_cparams(("arbitrary",)),
        name="out_proj_ln",
    )(oc, od, om, x2, w_out, ln_p)


def _pad_rows(rows, n):
    a = jnp.stack(rows).astype(F32)
    return jnp.pad(a, ((0, n - a.shape[0]), (0, 0)))


def _project(x, w_in_t, layer, bm, dn_chunk):
    b, l, _ = x.shape
    proj, bdc, bdr = _in_proj(x.reshape(b * l, D_MODEL), w_in_t, layer, bm)
    bdr4 = bdr.reshape(2 * DN_HEADS, b, l // dn_chunk, dn_chunk).transpose(1, 2, 0, 3)
    return proj.reshape(b, l, N_MAIN), bdc.reshape(b, l, SMALL_PAD), bdr4


def _mix(x, oc, od, om, w_out, ln_p, out_bm):
    b, l, _ = x.shape
    y = _out_proj(oc.reshape(b * l, C_CONV), od.reshape(b * l, DN_WIDTH), om.reshape(b * l, MEM_WIDTH),
                  x.reshape(b * l, D_MODEL), w_out, ln_p, out_bm)
    return y.reshape(b, l, D_MODEL)


def kernel(x_prompt, x_sample, mem_prompt, state_conv, state_qkv_conv, state_delta, cache_mem_k, cache_mem_v,
           w_in, conv_w, conv_b, conv_ln_g, conv_ln_b, qkv_conv_w, a_log, dt_bias, delta_norm_w,
           w_mem_k, w_mem_v, w_out, ln_g, ln_b):
    bp = x_prompt.shape[0]
    bs = x_sample.shape[0]
    hp, hs = x_prompt, x_sample
    outs = [[] for _ in range(8)]
    w_in_t = jnp.swapaxes(w_in, 1, 2)
    kv_rows = MEM_TOKENS * MEM_HEADS
    ls = x_sample.shape[1]
    for i in range(DEPTH):
        conv_p = _pad_rows([conv_b[i], conv_ln_g[i], conv_ln_b[i]], PARAM_ROWS)
        p_row = jnp.stack([a_log[i], dt_bias[i]]).astype(F32)
        p_col = p_row.T
        norm_w = delta_norm_w[i].reshape(1, DN_HEAD_DIM)
        w_out_i = w_out[i].astype(BF16)
        ln_p = _pad_rows([ln_g[i], ln_b[i]], PARAM_ROWS)

        w_kv = jnp.concatenate([w_mem_k[i], w_mem_v[i]], axis=1).astype(BF16)
        kv = _matmul(mem_prompt.reshape(bp * MEM_TOKENS, D_MODEL), w_kv, KV_PROJ_BLOCK, 2 * MEM_WIDTH)
        mk = kv[:, :MEM_WIDTH].reshape(bp, MEM_TOKENS, MEM_HEADS, MEM_HEAD_DIM)
        mv = kv[:, MEM_WIDTH:].reshape(bp, MEM_TOKENS, MEM_HEADS, MEM_HEAD_DIM)
        sk = cache_mem_k[i].reshape(bs, kv_rows, MEM_HEAD_DIM)
        sv = cache_mem_v[i].reshape(bs, kv_rows, MEM_HEAD_DIM)

        proj_p, bdc_p, bdr_p = _project(hp, w_in_t, i, PROMPT_PROJ_ROWS, PROMPT_DN_CHUNK)
        proj_s, bdc_s, bdr_s = _project(hs, w_in_t, i, bs * ls, ls)

        oc_p, conv_st_p, om_s = _conv_group(proj_p, None, i, conv_w[i], conv_p, PROMPT_CONV_TILE, 1,
                                            guest=(proj_s, sk, sv, SAMPLE_SEQS))
        od_p, qkv_st_p, s_p = _dn_group_pipelined(proj_p, bdc_p, bdr_p, qkv_conv_w[i], p_row, p_col, norm_w,
                                                  PROMPT_DN_TILE, PROMPT_DN_CHUNK)
        om_p = _mem_group(proj_p, mk.reshape(bp, kv_rows, MEM_HEAD_DIM), mv.reshape(bp, kv_rows, MEM_HEAD_DIM),
                          PROMPT_MEM_TILE, 1)
        hp_next = _mix(hp, oc_p, od_p, om_p, w_out_i, ln_p, OUT_ROWS)
        outs[0].append(conv_st_p); outs[1].append(qkv_st_p); outs[2].append(s_p)
        outs[3].append(mk)
        outs[4].append(mv)

        oc_s, conv_st_s = _conv_group(proj_s, state_conv, i, conv_w[i], conv_p, ls, SAMPLE_CONV_SEQS)
        od_s, qkv_st_s, s_s = _dn_group(proj_s, bdc_s, bdr_s, state_qkv_conv, state_delta, i, qkv_conv_w[i],
                                        p_row, p_col, norm_w, ls, ls, SAMPLE_SEQS)
        hs_next = _mix(hs, oc_s, od_s, om_s, w_out_i, ln_p, OUT_ROWS)
        outs[5].append(conv_st_s); outs[6].append(qkv_st_s); outs[7].append(s_s)
        hp, hs = hp_next, hs_next
    return (hp, hs) + tuple(jnp.stack(o) for o in outs)
```
